```python
import math
import jax, jax.numpy as jnp
from jax import lax
import numpy as np

D_MODEL = 2048
BATCH = 16
SEQ = 2048
DEPTH = 4

N_MIXERS = 3
N_NSA = (DEPTH + N_MIXERS - 1) // N_MIXERS
N_CONV = (DEPTH + N_MIXERS - 2) // N_MIXERS
N_GLA = DEPTH // N_MIXERS

DEEPNORM_ALPHA = (2.0 * DEPTH) ** 0.25
DEEPNORM_BETA = (8.0 * DEPTH) ** -0.25
LN_EPS = 1e-5
MACARON_WEIGHT = 0.5

D_FF = 5632

NSA_HEADS = 16
NSA_KV_GROUPS = 4
NSA_HEAD_DIM = D_MODEL // NSA_HEADS
NSA_Q = NSA_HEADS * NSA_HEAD_DIM
NSA_KV = NSA_KV_GROUPS * NSA_HEAD_DIM
NSA_IN = NSA_Q + 6 * NSA_KV + 3 * NSA_HEADS
CMP_BLOCK = 32
CMP_STRIDE = 16
SEL_BLOCK = 64
SEL_TOPK = 16
WINDOW = 512
SEL_Q_BLOCK = 16
ATTN_Q_BLOCK = 128
ROPE_THETA = 10000.0
MAX_POS_OFFSET = 4096
NEG = -1e30
FORCE = 1e3

CONV_WIDTH = 3

GLA_HEADS = 4
GLA_KEY_DIM = D_MODEL // 2
GLA_VAL_DIM = D_MODEL
GLA_HEAD_K = GLA_KEY_DIM // GLA_HEADS
GLA_HEAD_V = GLA_VAL_DIM // GLA_HEADS
GLA_GATE_RANK = 16
GLA_GATE_NORM = 16.0
GLA_CHUNK = 64
GLA_IN = 2 * GLA_KEY_DIM + 2 * GLA_VAL_DIM + GLA_GATE_RANK

kernel_name = "hybrid_nsa_shortconv_gla_macaron_deepnorm"


def layer_norm(x, g, b):
    xf = x.astype(jnp.float32)
    mu = jnp.mean(xf, axis=-1, keepdims=True)
    var = jnp.mean(jnp.square(xf - mu), axis=-1, keepdims=True)
    return ((xf - mu) * lax.rsqrt(var + LN_EPS) * g + b).astype(x.dtype)


def swiglu(x, w_in, w_out):
    gate, up = jnp.split(x @ w_in, 2, axis=-1)
    return (jax.nn.silu(gate) * up) @ w_out


def rope(t, positions):
    hd = t.shape[-1]
    inv = ROPE_THETA ** (-jnp.arange(0, hd, 2, dtype=jnp.float32) / hd)
    ang = positions.astype(jnp.float32)[..., None] * inv
    cos = jnp.cos(ang)[:, :, None, :]
    sin = jnp.sin(ang)[:, :, None, :]
    t1, t2 = jnp.split(t.astype(jnp.float32), 2, axis=-1)
    return jnp.concatenate([t1 * cos - t2 * sin, t2 * cos + t1 * sin], axis=-1).astype(t.dtype)


def masked_softmax(s, valid):
    p = jax.nn.softmax(jnp.where(valid, s.astype(jnp.float32), NEG), axis=-1)
    return jnp.where(valid, p, 0.0)


def nsa_mixer(x, positions, w_in, gate_b, cmp_pos, cmp_w1, cmp_w2, w_out):
    B, T, _ = x.shape
    H, G, hd = NSA_HEADS, NSA_KV_GROUPS, NSA_HEAD_DIM
    R = H // G
    scale = hd ** -0.5
    splits = [NSA_Q + i * NSA_KV for i in range(7)]
    q, kc, vc, ks, vs, kw, vw, gl = jnp.split(x @ w_in, splits, axis=-1)
    q = rope(q.reshape(B, T, H, hd), positions)
    kc, ks, kw = [rope(t.reshape(B, T, G, hd), positions) for t in (kc, ks, kw)]
    vc, vs, vw = [t.reshape(B, T, G, hd) for t in (vc, vs, vw)]
    gates = jax.nn.sigmoid(gl + gate_b).reshape(B, T, 3, G, R).transpose(2, 0, 3, 4, 1)
    qg = q.reshape(B, T, G, R, hd).transpose(0, 2, 3, 1, 4) * scale
    tq = jnp.arange(T)

    n_cmp = (T - CMP_BLOCK) // CMP_STRIDE + 1
    tok = np.arange(n_cmp)[:, None] * CMP_STRIDE + np.arange(CMP_BLOCK)[None, :]

    def compress(t, pos_emb, w1, w2):
        blocks = t[:, tok] + pos_emb[:, None, :]
        blocks = blocks.transpose(0, 1, 3, 2, 4).reshape(B, n_cmp, G, CMP_BLOCK * hd)
        return jax.nn.gelu(blocks @ w1) @ w2

    k_cmp = compress(kc, cmp_pos[0], cmp_w1[0], cmp_w2[0])
    v_cmp = compress(vc, cmp_pos[1], cmp_w1[1], cmp_w2[1])
    cmp_end = jnp.arange(n_cmp) * CMP_STRIDE + CMP_BLOCK - 1
    s_cmp = jnp.einsum('bgrtd,bngd->bgrtn', qg, k_cmp)
    p_cmp = masked_softmax(s_cmp, cmp_end[None, :] <= tq[:, None])
    o_cmp = jnp.einsum('bgrtn,bngd->bgrtd', p_cmp.astype(v_cmp.dtype), v_cmp)

    n_sel = T // SEL_BLOCK
    k_sel = min(SEL_TOPK, n_sel)
    c_start = jnp.arange(n_cmp) * CMP_STRIDE
    s_start = jnp.arange(n_sel) * SEL_BLOCK
    overlap = ((c_start[:, None] < s_start[None, :] + SEL_BLOCK)
               & (c_start[:, None] + CMP_BLOCK > s_start[None, :])).astype(jnp.float32)
    importance = jnp.einsum('bgrtn,nm->bgtm', p_cmp, overlap)
    cur = tq // SEL_BLOCK
    blk = jnp.arange(n_sel)
    sel_valid = blk[None, :] <= cur[:, None]
    forced = (blk[None, :] == 0) | (blk[None, :] == cur[:, None]) | (blk[None, :] == cur[:, None] - 1)
    score = jnp.where(sel_valid, importance + jnp.where(forced, FORCE, 0.0), NEG)
    _, idx = lax.top_k(score, k_sel)
    ks_blk = ks.reshape(B, n_sel, SEL_BLOCK, G, hd).transpose(0, 3, 1, 2, 4)
    vs_blk = vs.reshape(B, n_sel, SEL_BLOCK, G, hd).transpose(0, 3, 1, 2, 4)
    bi = jnp.arange(B)[:, None, None, None]
    gi = jnp.arange(G)[None, :, None, None]

    def sel_block(c):
        t0 = c * SEL_Q_BLOCK
        qc = lax.dynamic_slice_in_dim(qg, t0, SEL_Q_BLOCK, axis=3)
        ic = lax.dynamic_slice_in_dim(idx, t0, SEL_Q_BLOCK, axis=2)
        kg = ks_blk[bi, gi, ic]
        vg = vs_blk[bi, gi, ic]
        s = jnp.einsum('bgrqd,bgqkld->bgrqkl', qc, kg).reshape(B, G, R, SEL_Q_BLOCK, k_sel * SEL_BLOCK)
        kpos = ic[..., None] * SEL_BLOCK + jnp.arange(SEL_BLOCK)
        qpos = t0 + jnp.arange(SEL_Q_BLOCK)
        valid = (kpos <= qpos[None, None, :, None, None]).reshape(B, G, 1, SEL_Q_BLOCK, k_sel * SEL_BLOCK)
        p = masked_softmax(s, valid).astype(vg.dtype).reshape(B, G, R, SEL_Q_BLOCK, k_sel, SEL_BLOCK)
        return jnp.einsum('bgrqkl,bgqkld->bgrqd', p, vg)

    o_sel = lax.map(sel_block, jnp.arange(T // SEL_Q_BLOCK))
    o_sel = o_sel.transpose(1, 2, 3, 0, 4, 5).reshape(B, G, R, T, hd)

    pad = ((0, 0), (0, 0), (WINDOW, 0), (0, 0))
    kw_p = jnp.pad(kw.transpose(0, 2, 1, 3), pad)
    vw_p = jnp.pad(vw.transpose(0, 2, 1, 3), pad)
    band = WINDOW + ATTN_Q_BLOCK

    def win_block(c):
        t0 = c * ATTN_Q_BLOCK
        qb = lax.dynamic_slice_in_dim(qg, t0, ATTN_Q_BLOCK, axis=3)
        kb = lax.dynamic_slice_in_dim(kw_p, t0, band, axis=2)
        vb = lax.dynamic_slice_in_dim(vw_p, t0, band, axis=2)
        s = jnp.einsum('bgrqd,bgkd->bgrqk', qb, kb)
        kpos = t0 - WINDOW + jnp.arange(band)
        qpos = t0 + jnp.arange(ATTN_Q_BLOCK)
        valid = ((kpos[None, :] <= qpos[:, None]) & (kpos[None, :] > qpos[:, None] - WINDOW)
                 & (kpos[None, :] >= 0))
        p = masked_softmax(s, valid).astype(vb.dtype)
        return jnp.einsum('bgrqk,bgkd->bgrqd', p, vb)

    o_win = lax.map(win_block, jnp.arange(T // ATTN_Q_BLOCK))
    o_win = o_win.transpose(1, 2, 3, 0, 4, 5).reshape(B, G, R, T, hd)

    o = gates[0][..., None] * o_cmp + gates[1][..., None] * o_sel + gates[2][..., None] * o_win
    o = o.transpose(0, 3, 1, 2, 4).reshape(B, T, H * hd)
    return o @ w_out


def conv_mixer(x, w_in, conv_w, w_out):
    D = x.shape[-1]
    b_gate, c_gate, h = jnp.split(x @ w_in, 3, axis=-1)
    u = c_gate * h
    y = lax.conv_general_dilated(u, conv_w[:, None, :], window_strides=(1,),
                                 padding=[(CONV_WIDTH - 1, 0)],
                                 dimension_numbers=('NWC', 'WIO', 'NWC'),
                                 feature_group_count=D)
    return (b_gate * y) @ w_out


def gla_chunked(q, k, v, g):
    B, T, H, dk = q.shape
    dv = v.shape[-1]
    C = GLA_CHUNK
    N = T // C

    def chunks(t):
        return t.reshape(B, N, C, H, t.shape[-1]).transpose(1, 0, 3, 2, 4)

    q, k, v, g = chunks(q), chunks(k), chunks(v), chunks(g)
    b = jnp.cumsum(g, axis=3)
    b_last = b[..., -1:, :]
    qd = q * jnp.exp(b)
    kd = k * jnp.exp(-b)
    kl = k * jnp.exp(b_last - b)
    causal = jnp.tril(jnp.ones((C, C), dtype=bool))
    a = jnp.where(causal, jnp.einsum('nbhcd,nbhsd->nbhcs', qd, kd), 0.0)
    o_intra = jnp.einsum('nbhcs,nbhsv->nbhcv', a, v)
    decay = jnp.exp(b_last[..., 0, :])

    def step(state, inp):
        qd_n, kl_n, v_n, dl_n = inp
        o_n = jnp.einsum('bhcd,bhdv->bhcv', qd_n, state)
        state = state * dl_n[..., None] + jnp.einsum('bhcd,bhcv->bhdv', kl_n, v_n)
        return state, o_n

    s0 = jnp.zeros((B, H, dk, dv), q.dtype)
    _, o_inter = lax.scan(step, s0, (qd, kl, v, decay))
    return (o_intra + o_inter).transpose(1, 0, 3, 2, 4).reshape(B, T, H, dv)


def gla_mixer(x, w_in, w_a2, b_a, norm_g, w_out):
    B, T, _ = x.shape
    splits = [GLA_KEY_DIM, 2 * GLA_KEY_DIM, 2 * GLA_KEY_DIM + GLA_VAL_DIM, 2 * GLA_KEY_DIM + 2 * GLA_VAL_DIM]
    q, k, v, r, a = jnp.split(x @ w_in, splits, axis=-1)
    gk = jax.nn.log_sigmoid((a @ w_a2 + b_a).astype(jnp.float32)) / GLA_GATE_NORM
    f32 = jnp.float32
    q = q.astype(f32).reshape(B, T, GLA_HEADS, GLA_HEAD_K) * (GLA_HEAD_K ** -0.5)
    k = k.astype(f32).reshape(B, T, GLA_HEADS, GLA_HEAD_K)
    v = v.astype(f32).reshape(B, T, GLA_HEADS, GLA_HEAD_V)
    gk = gk.reshape(B, T, GLA_HEADS, GLA_HEAD_K)
    o = gla_chunked(q, k, v, gk)
    o = o * lax.rsqrt(jnp.mean(jnp.square(o), axis=-1, keepdims=True) + LN_EPS) * norm_g
    o = o.astype(x.dtype).reshape(B, T, GLA_VAL_DIM) * jax.nn.silu(r)
    return o @ w_out


def setup_inputs(seed: int = 0) -> dict:
    key = jax.random.key(seed)
    ks = jax.random.split(key, 20)

    def nrm(k, shape, scale):
        return jax.random.normal(k, shape, jnp.float32) * scale

    D = D_MODEL
    hd = NSA_HEAD_DIM
    x = nrm(ks[0], (BATCH, SEQ, D), 1.0)
    positions = (jax.random.randint(ks[1], (BATCH, 1), 0, MAX_POS_OFFSET, dtype=jnp.int32)
                 + jnp.arange(SEQ, dtype=jnp.int32)[None, :])
    return {
        'x': x,
        'positions': positions,
        'ln_g': 1.0 + nrm(ks[2], (DEPTH, 3, D), 0.02),
        'ln_b': nrm(ks[3], (DEPTH, 3, D), 0.02),
        'ffn_w_in': nrm(ks[4], (DEPTH, 2, D, 2 * D_FF), D ** -0.5),
        'ffn_w_out': nrm(ks[5], (DEPTH, 2, D_FF, D), D_FF ** -0.5 * DEEPNORM_BETA),
        'nsa_w_in': nrm(ks[6], (N_NSA, D, NSA_IN), D ** -0.5),
        'nsa_gate_b': nrm(ks[7], (N_NSA, 3 * NSA_HEADS), 0.1),
        'nsa_cmp_pos': nrm(ks[8], (N_NSA, 2, CMP_BLOCK, hd), 0.1),
        'nsa_cmp_w1': nrm(ks[9], (N_NSA, 2, CMP_BLOCK * hd, hd), (CMP_BLOCK * hd) ** -0.5),
        'nsa_cmp_w2': nrm(ks[10], (N_NSA, 2, hd, hd), hd ** -0.5),
        'nsa_w_out': nrm(ks[11], (N_NSA, NSA_Q, D), NSA_Q ** -0.5 * DEEPNORM_BETA),
        'conv_w_in': nrm(ks[12], (N_CONV, D, 3 * D), D ** -0.5),
        'conv_w': nrm(ks[13], (N_CONV, CONV_WIDTH, D), CONV_WIDTH ** -0.5),
        'conv_w_out': nrm(ks[14], (N_CONV, D, D), D ** -0.5 * DEEPNORM_BETA),
        'gla_w_in': nrm(ks[15], (N_GLA, D, GLA_IN), D ** -0.5),
        'gla_w_a2': nrm(ks[16], (N_GLA, GLA_GATE_RANK, GLA_KEY_DIM), GLA_GATE_RANK ** -0.5),
        'gla_b_a': nrm(ks[17], (N_GLA, GLA_KEY_DIM), 0.1),
        'gla_norm_g': 1.0 + nrm(ks[18], (N_GLA, GLA_HEAD_V), 0.02),
        'gla_w_out': nrm(ks[19], (N_GLA, GLA_VAL_DIM, D), GLA_VAL_DIM ** -0.5 * DEEPNORM_BETA),
    }


def reference(x, positions, ln_g, ln_b, ffn_w_in, ffn_w_out,
              nsa_w_in, nsa_gate_b, nsa_cmp_pos, nsa_cmp_w1, nsa_cmp_w2, nsa_w_out,
              conv_w_in, conv_w, conv_w_out,
              gla_w_in, gla_w_a2, gla_b_a, gla_norm_g, gla_w_out):
    for i in range(DEPTH):
        x = layer_norm(DEEPNORM_ALPHA * x + MACARON_WEIGHT * swiglu(x, ffn_w_in[i, 0], ffn_w_out[i, 0]),
                       ln_g[i, 0], ln_b[i, 0])
        kind, j = i % N_MIXERS, i // N_MIXERS
        if kind == 0:
            y = nsa_mixer(x, positions, nsa_w_in[j], nsa_gate_b[j], nsa_cmp_pos[j],
                          nsa_cmp_w1[j], nsa_cmp_w2[j], nsa_w_out[j])
        elif kind == 1:
            y = conv_mixer(x, conv_w_in[j], conv_w[j], conv_w_out[j])
        else:
            y = gla_mixer(x, gla_w_in[j], gla_w_a2[j], gla_b_a[j], gla_norm_g[j], gla_w_out[j])
        x = layer_norm(DEEPNORM_ALPHA * x + y, ln_g[i, 1], ln_b[i, 1])
        x = layer_norm(DEEPNORM_ALPHA * x + MACARON_WEIGHT * swiglu(x, ffn_w_in[i, 1], ffn_w_out[i, 1]),
                       ln_g[i, 2], ln_b[i, 2])
    return x
```

```python
import functools

import numpy as np
import jax
import jax.numpy as jnp
from jax import lax
from jax.experimental import pallas as pl
from jax.experimental.pallas import tpu as pltpu

F32 = jnp.float32
BF16 = jnp.bfloat16

D_MODEL = 2048
DEPTH = 4
N_MIXERS = 3
DEEPNORM_ALPHA = (2.0 * DEPTH) ** 0.25
LN_EPS = 1e-5
MACARON_WEIGHT = 0.5
D_FF = 5632

NSA_HEADS = 16
NSA_KV_GROUPS = 4
NSA_REP = NSA_HEADS // NSA_KV_GROUPS
NSA_HEAD_DIM = D_MODEL // NSA_HEADS
NSA_Q = NSA_HEADS * NSA_HEAD_DIM
NSA_KV = NSA_KV_GROUPS * NSA_HEAD_DIM
CMP_BLOCK = 32
CMP_STRIDE = 16
SEL_BLOCK = 64
SEL_TOPK = 16
WINDOW = 512
ROPE_THETA = 10000.0
NEG = -1e30
FORCE = 1e3

CONV_WIDTH = 3

GLA_HEADS = 4
GLA_KEY_DIM = D_MODEL // 2
GLA_VAL_DIM = D_MODEL
GLA_HEAD_K = GLA_KEY_DIM // GLA_HEADS
GLA_HEAD_V = GLA_VAL_DIM // GLA_HEADS
GLA_GATE_RANK = 16
GLA_GATE_NORM = 16.0
GLA_CHUNK = 64

LANES = 128
VMEM_LIMIT = 56 * 1024 * 1024


def _params(sem):
    return pltpu.CompilerParams(dimension_semantics=sem, vmem_limit_bytes=VMEM_LIMIT)


def _layer_norm(y, g, b):
    mu = jnp.mean(y, axis=-1, keepdims=True)
    d = y - mu
    var = jnp.mean(d * d, axis=-1, keepdims=True)
    return d * lax.rsqrt(var + LN_EPS) * g + b


def _silu(h):
    return h * jax.nn.sigmoid(h)


def _dot(a, b):
    return jnp.dot(a, b, preferred_element_type=F32)


def _dot_nt(a, b):
    return lax.dot_general(a, b, (((1,), (1,)), ((), ())), preferred_element_type=F32)


def _dot_tn(a, b):
    return lax.dot_general(a, b, (((0,), (0,)), ((), ())), preferred_element_type=F32)


def _ffn_kernel(x_ref, wg_ref, wu_ref, wo_ref, g_ref, b_ref, o_ref, xb_ref, acc_ref):
    j = pl.program_id(1)

    @pl.when(j == 0)
    def _():
        xb_ref[...] = x_ref[...].astype(BF16)
        acc_ref[...] = jnp.zeros_like(acc_ref)

    xb = xb_ref[...]
    h = _dot(xb, wg_ref[...])
    u = _dot(xb, wu_ref[...])
    a = (_silu(h) * u).astype(BF16)
    acc_ref[...] += _dot(a, wo_ref[...])

    @pl.when(j == pl.num_programs(1) - 1)
    def _():
        y = DEEPNORM_ALPHA * x_ref[...] + MACARON_WEIGHT * acc_ref[...]
        o_ref[...] = _layer_norm(y, g_ref[...], b_ref[...])


def ffn_ln(x, w_in, w_out, g, b, *, tm=512, tf=512):
    m, d = x.shape
    ff = w_out.shape[0]
    nj = ff // tf
    return pl.pallas_call(
        _ffn_kernel,
        grid=(m // tm, nj),
        in_specs=[
            pl.BlockSpec((tm, d), lambda i, j: (i, 0)),
            pl.BlockSpec((d, tf), lambda i, j: (0, j)),
            pl.BlockSpec((d, tf), lambda i, j: (0, j + nj)),
            pl.BlockSpec((tf, d), lambda i, j: (j, 0)),
            pl.BlockSpec((1, d), lambda i, j: (0, 0)),
            pl.BlockSpec((1, d), lambda i, j: (0, 0)),
        ],
        out_specs=pl.BlockSpec((tm, d), lambda i, j: (i, 0)),
        out_shape=jax.ShapeDtypeStruct((m, d), F32),
        scratch_shapes=[pltpu.VMEM((tm, d), BF16), pltpu.VMEM((tm, d), F32)],
        compiler_params=_params(("parallel", "arbitrary")),
        name="ffn_ln",
    )(x, w_in, w_in, w_out, g, b)


def _proj_kernel(x_ref, w_ref, cos_ref, sin_ref, o_ref, xb_ref, *, n_rope, n_scale, scale):
    j = pl.program_id(1)

    @pl.when(j == 0)
    def _():
        xb_ref[...] = x_ref[...].astype(BF16)

    y = _dot(xb_ref[...], w_ref[...])
    tn = y.shape[1]

    if n_rope > 0:
        @pl.when(j < n_rope)
        def _():
            cos = cos_ref[...]
            sin = sin_ref[...]
            sc = jnp.where(j < n_scale, scale, 1.0).astype(F32)
            for hh in range(tn // LANES):
                t = y[:, hh * LANES:(hh + 1) * LANES]
                rot = pltpu.roll(t, LANES // 2, axis=1)
                o_ref[:, hh * LANES:(hh + 1) * LANES] = (t * cos + rot * sin) * sc

        @pl.when(j >= n_rope)
        def _():
            o_ref[...] = y
    else:
        o_ref[...] = y


def proj(x, w, cos, sin, *, n_rope=0, n_scale=0, scale=1.0, tm=1024, tn=512):
    m, d = x.shape
    n = w.shape[1]
    kern = functools.partial(_proj_kernel, n_rope=n_rope, n_scale=n_scale, scale=scale)
    return pl.pallas_call(
        kern,
        grid=(m // tm, n // tn),
        in_specs=[
            pl.BlockSpec((tm, d), lambda i, j: (i, 0)),
            pl.BlockSpec((d, tn), lambda i, j: (0, j)),
            pl.BlockSpec((tm, LANES), lambda i, j: (i, 0)),
            pl.BlockSpec((tm, LANES), lambda i, j: (i, 0)),
        ],
        out_specs=pl.BlockSpec((tm, tn), lambda i, j: (i, j)),
        out_shape=jax.ShapeDtypeStruct((m, n), F32),
        scratch_shapes=[pltpu.VMEM((tm, d), BF16)],
        compiler_params=_params(("parallel", "arbitrary")),
        name="proj",
    )(x, w, cos, sin)


def _rope_table_kernel(pos_ref, inv_ref, sign_ref, cos_ref, sin_ref):
    ang = pos_ref[...] * inv_ref[...]
    cos_ref[...] = jnp.cos(ang)
    sin_ref[...] = jnp.sin(ang) * sign_ref[...]


def rope_tables(pos_f32, *, tm=2048):
    m = pos_f32.shape[0]
    hd = NSA_HEAD_DIM
    inv = ROPE_THETA ** (-jnp.arange(0, hd, 2, dtype=F32) / hd)
    inv_full = jnp.concatenate([inv, inv])[None, :]
    sign = jnp.concatenate([-jnp.ones((hd // 2,), F32), jnp.ones((hd // 2,), F32)])[None, :]
    return pl.pallas_call(
        _rope_table_kernel,
        grid=(m // tm,),
        in_specs=[
            pl.BlockSpec((tm, 1), lambda i: (i, 0)),
            pl.BlockSpec((1, hd), lambda i: (0, 0)),
            pl.BlockSpec((1, hd), lambda i: (0, 0)),
        ],
        out_specs=[pl.BlockSpec((tm, hd), lambda i: (i, 0))] * 2,
        out_shape=[jax.ShapeDtypeStruct((m, hd), F32)] * 2,
        compiler_params=_params(("parallel",)),
        name="rope_tables",
    )(pos_f32, inv_full, sign)


def _gate_kernel(x_ref, w_ref, b_ref, o_ref):
    z = _dot(x_ref[...].astype(BF16), w_ref[...]) + b_ref[...]
    o_ref[...] = jax.nn.sigmoid(z)


def gate_proj(x, w, b, *, tm=1024):
    m, d = x.shape
    n = w.shape[1]
    return pl.pallas_call(
        _gate_kernel,
        grid=(m // tm,),
        in_specs=[
            pl.BlockSpec((tm, d), lambda i: (i, 0)),
            pl.BlockSpec((d, n), lambda i: (0, 0)),
            pl.BlockSpec((1, n), lambda i: (0, 0)),
        ],
        out_specs=pl.BlockSpec((tm, n), lambda i: (i, 0)),
        out_shape=jax.ShapeDtypeStruct((m, n), F32),
        compiler_params=_params(("parallel",)),
        name="gate_proj",
    )(x, w, b)


def _out_ln_kernel(a_ref, w_ref, x_ref, g_ref, b_ref, o_ref):
    y = DEEPNORM_ALPHA * x_ref[...] + _dot(a_ref[...], w_ref[...])
    o_ref[...] = _layer_norm(y, g_ref[...], b_ref[...])


def out_ln(a, w, x, g, b, *, tm=512):
    m, d = x.shape
    k = a.shape[1]
    return pl.pallas_call(
        _out_ln_kernel,
        grid=(m // tm,),
        in_specs=[
            pl.BlockSpec((tm, k), lambda i: (i, 0)),
            pl.BlockSpec((k, d), lambda i: (0, 0)),
            pl.BlockSpec((tm, d), lambda i: (i, 0)),
            pl.BlockSpec((1, d), lambda i: (0, 0)),
            pl.BlockSpec((1, d), lambda i: (0, 0)),
        ],
        out_specs=pl.BlockSpec((tm, d), lambda i: (i, 0)),
        out_shape=jax.ShapeDtypeStruct((m, d), F32),
        compiler_params=_params(("parallel",)),
        name="out_ln",
    )(a, w, x, g, b)


def _gelu_tanh(x):
    c = np.sqrt(2.0 / np.pi).astype(np.float32)
    return 0.5 * x * (1.0 + jnp.tanh(c * (x + 0.044715 * (x * x * x))))


def _cmp_kernel(kv_ref, vv_ref, pos_ref, w1_ref, w2_ref, ko_ref, vo_ref):
    half = kv_ref.shape[3]
    nrow = kv_ref.shape[2]
    for idx, (src, dst) in enumerate(((kv_ref, ko_ref), (vv_ref, vo_ref))):
        v = src[0, 0]
        a = (v + pos_ref[idx, 0:1, :]).astype(BF16)
        b = (v + pos_ref[idx, 1:2, :]).astype(BF16)
        ya = _dot(a, w1_ref[idx, 0:half, :])
        yb = _dot(b, w1_ref[idx, half:2 * half, :])
        h = _gelu_tanh(ya + pltpu.roll(yb, nrow - 1, axis=0))
        dst[0, 0] = _dot(h.astype(BF16), w2_ref[idx]).astype(BF16)


def nsa_compress(kview, vview, pos, w1, w2):
    bsz, g, nrow, half = kview.shape
    hd = w2.shape[-1]
    blk = pl.BlockSpec((1, 1, nrow, half), lambda b, gg: (b, gg, 0, 0))
    oblk = pl.BlockSpec((1, 1, nrow, hd), lambda b, gg: (b, gg, 0, 0))
    return pl.pallas_call(
        _cmp_kernel,
        grid=(bsz, g),
        in_specs=[
            blk, blk,
            pl.BlockSpec(pos.shape, lambda b, gg: (0, 0, 0)),
            pl.BlockSpec(w1.shape, lambda b, gg: (0, 0, 0)),
            pl.BlockSpec(w2.shape, lambda b, gg: (0, 0, 0)),
        ],
        out_specs=[oblk, oblk],
        out_shape=[jax.ShapeDtypeStruct((bsz, g, nrow, hd), BF16)] * 2,
        compiler_params=_params(("parallel", "parallel")),
        name="nsa_compress",
    )(kview, vview, pos, w1, w2)


def _nsa_attn_kernel(q_ref, ks_ref, vs_ref, kw_ref, vw_ref, kc_ref, vc_ref, gate_ref,
                     ov_ref, e_ref, o_ref, ksb, vsb, kwb, vwb, *, tq, tk, seq):
    i = pl.program_id(2)
    hd = NSA_HEAD_DIM
    rep = NSA_REP
    rows = rep * tq

    @pl.when(i == 0)
    def _():
        ksb[...] = ks_ref[...].astype(BF16)
        vsb[...] = vs_ref[...].astype(BF16)
        kwb[...] = kw_ref[...].astype(BF16)
        vwb[...] = vw_ref[...].astype(BF16)

    t0 = i * tq
    qs = jnp.concatenate([q_ref[:, r * hd:(r + 1) * hd].astype(BF16) for r in range(rep)], axis=0)

    def tpos(shape):
        return t0 + (lax.broadcasted_iota(jnp.int32, shape, 0) & (tq - 1))

    ncmp = kc_ref.shape[2]
    s = _dot_nt(qs, kc_ref[0, 0])
    n_idx = lax.broadcasted_iota(jnp.int32, (rows, ncmp), 1)
    valid = n_idx * CMP_STRIDE + (CMP_BLOCK - 1) <= tpos((rows, ncmp))
    sm = jnp.where(valid, s, NEG)
    mx = jnp.max(sm, axis=1, keepdims=True)
    p = jnp.where(valid, jnp.exp(sm - mx), 0.0)
    den = jnp.sum(p, axis=1, keepdims=True)
    p = p / jnp.where(den > 0.0, den, 1.0)
    o_cmp = _dot(p.astype(BF16), vc_ref[0, 0])

    psum = p[0:tq]
    for r in range(1, rep):
        psum = psum + p[r * tq:(r + 1) * tq]
    hi = psum.astype(BF16)
    r1 = psum - hi.astype(F32)
    mid = r1.astype(BF16)
    lo = (r1 - mid.astype(F32)).astype(BF16)
    ov = ov_ref[...]
    imp = _dot(hi, ov) + _dot(mid, ov) + _dot(lo, ov)
    nblk_pad = imp.shape[1]
    m_idx = lax.broadcasted_iota(jnp.int32, (tq, nblk_pad), 1)
    cur = (t0 + lax.broadcasted_iota(jnp.int32, (tq, nblk_pad), 0)) // SEL_BLOCK
    forced = (m_idx == 0) | (m_idx == cur) | (m_idx == cur - 1)
    score = jnp.where(m_idx <= cur, imp + jnp.where(forced, FORCE, 0.0), NEG)
    n_sel = seq // SEL_BLOCK
    sc_t = score.T[0:n_sel]
    b_idx = lax.broadcasted_iota(jnp.int32, (n_sel, tq), 0)
    rank = jnp.zeros((n_sel, tq), F32)
    for mp in range(n_sel):
        row = sc_t[mp:mp + 1, :]
        beats = (row > sc_t) | ((row == sc_t) & (b_idx > mp))
        rank = rank + jnp.where(beats, 1.0, 0.0)
    sel_t = jnp.where(rank < float(SEL_TOPK), 1.0, 0.0)
    sel_t = jnp.concatenate([sel_t, jnp.zeros((nblk_pad - n_sel, tq), F32)], axis=0)
    sel = sel_t.T.astype(BF16)

    tq_pos = t0 + lax.broadcasted_iota(jnp.int32, (tq, tk), 0)
    k_iota = lax.broadcasted_iota(jnp.int32, (tq, tk), 1)

    def sel_body(c, carry):
        m_i, l_i, acc = carry
        off = pl.multiple_of(c * tk, tk)
        k = ksb[pl.ds(off, tk), :]
        v = vsb[pl.ds(off, tk), :]
        sc = _dot_nt(qs, k)
        picked = _dot(sel, e_ref[c])
        ok = (picked > 0.5) & (k_iota + c * tk <= tq_pos)
        bias = jnp.where(ok, 0.0, NEG)
        sc = sc + jnp.concatenate([bias] * rep, axis=0)
        m_new = jnp.maximum(m_i, jnp.max(sc, axis=1, keepdims=True))
        pp = jnp.exp(sc - m_new)
        alpha = jnp.exp(m_i - m_new)
        l_new = alpha * l_i + jnp.sum(pp, axis=1, keepdims=True)
        acc = alpha * acc + _dot(pp.astype(BF16), v)
        return m_new, l_new, acc

    n_chunks = (t0 + tq + tk - 1) // tk
    init = (jnp.full((rows, 1), NEG, F32), jnp.zeros((rows, 1), F32), jnp.zeros((rows, hd), F32))
    _, l_s, acc_s = lax.fori_loop(0, n_chunks, sel_body, init)
    o_sel = acc_s / l_s

    band = WINDOW + tq
    start = pl.multiple_of(jnp.maximum(t0 - WINDOW, 0), tq)
    kb = kwb[pl.ds(start, band), :]
    vb = vwb[pl.ds(start, band), :]
    sw = _dot_nt(qs, kb)
    kpos = start + lax.broadcasted_iota(jnp.int32, (rows, band), 1)
    tp = tpos((rows, band))
    okw = (kpos <= tp) & (kpos > tp - WINDOW)
    sw = jnp.where(okw, sw, NEG)
    mw = jnp.max(sw, axis=1, keepdims=True)
    pw = jnp.exp(sw - mw)
    lw = jnp.sum(pw, axis=1, keepdims=True)
    o_win = _dot(pw.astype(BF16), vb) / lw

    gt = gate_ref[...]
    for r in range(rep):
        rs = slice(r * tq, (r + 1) * tq)
        o_r = (gt[:, r:r + 1] * o_cmp[rs]
               + gt[:, rep + r:rep + r + 1] * o_sel[rs]
               + gt[:, 2 * rep + r:2 * rep + r + 1] * o_win[rs])
        o_ref[:, r * hd:(r + 1) * hd] = o_r.astype(BF16)


def nsa_attention(p, kcmp, vcmp, gates, overlap, expand, *, bsz, seq, tq=128, tk=512):
    m = p.shape[0]
    hd = NSA_HEAD_DIM
    g = NSA_KV_GROUPS
    nq = seq // tq
    gw = NSA_REP * hd
    c_ks = (NSA_Q + NSA_KV) // hd
    c_kw = (NSA_Q + 2 * NSA_KV) // hd
    c_vs = (NSA_Q + 4 * NSA_KV) // hd
    c_vw = (NSA_Q + 5 * NSA_KV) // hd
    ncmp = kcmp.shape[2]

    def kv_spec(c0):
        return pl.BlockSpec((seq, hd), lambda b, gg, i: (b, c0 + gg))

    kern = functools.partial(_nsa_attn_kernel, tq=tq, tk=tk, seq=seq)
    return pl.pallas_call(
        kern,
        grid=(bsz, g, nq),
        in_specs=[
            pl.BlockSpec((tq, gw), lambda b, gg, i: (b * nq + i, gg)),
            kv_spec(c_ks), kv_spec(c_vs), kv_spec(c_kw), kv_spec(c_vw),
            pl.BlockSpec((1, 1, ncmp, hd), lambda b, gg, i: (b, gg, 0, 0)),
            pl.BlockSpec((1, 1, ncmp, hd), lambda b, gg, i: (b, gg, 0, 0)),
            pl.BlockSpec((tq, LANES), lambda b, gg, i: (b * nq + i, gg)),
            pl.BlockSpec(overlap.shape, lambda b, gg, i: (0, 0)),
            pl.BlockSpec(expand.shape, lambda b, gg, i: (0, 0, 0)),
        ],
        out_specs=pl.BlockSpec((tq, gw), lambda b, gg, i: (b * nq + i, gg)),
        out_shape=jax.ShapeDtypeStruct((m, NSA_Q), BF16),
        scratch_shapes=[pltpu.VMEM((seq, hd), BF16)] * 4,
        compiler_params=_params(("parallel", "parallel", "arbitrary")),
        name="nsa_attention",
    )(p, p, p, p, p, kcmp, vcmp, gates, overlap, expand)


def _nsa_constants(seq, tk):
    n_cmp_pad = LANES
    n = np.arange(n_cmp_pad)[:, None]
    mm = np.arange(LANES)[None, :]
    c_start = n * CMP_STRIDE
    s_start = mm * SEL_BLOCK
    n_cmp = (seq - CMP_BLOCK) // CMP_STRIDE + 1
    overlap = ((c_start < s_start + SEL_BLOCK) & (c_start + CMP_BLOCK > s_start)
               & (n < n_cmp) & (mm < seq // SEL_BLOCK))
    keys = np.arange(seq)[None, :]
    expand = (np.arange(LANES)[:, None] == keys // SEL_BLOCK)
    expand = expand.reshape(LANES, seq // tk, tk).transpose(1, 0, 2)
    return (jnp.asarray(overlap.astype(np.float32), dtype=BF16),
            jnp.asarray(expand.astype(np.float32), dtype=BF16))


def nsa_mixer_ln(x, cos, sin, w_in, gate_b, cmp_pos, cmp_w1, cmp_w2, w_out, g, b, *, bsz, seq):
    hd = NSA_HEAD_DIM
    grp = NSA_KV_GROUPS
    rep = NSA_REP
    kv = NSA_KV
    cols = [slice(0, NSA_Q)] + [slice(NSA_Q + i * kv, NSA_Q + (i + 1) * kv) for i in (0, 2, 4, 1, 3, 5)]
    w_main = jnp.concatenate([w_in[:, c] for c in cols], axis=1).astype(BF16)
    p = proj(x, w_main, cos, sin, n_rope=(NSA_Q + 3 * kv) // 512, n_scale=NSA_Q // 512,
             scale=hd ** -0.5)

    w_gl = w_in[:, NSA_Q + 6 * kv:].reshape(-1, 3, grp, rep).transpose(0, 2, 1, 3).reshape(-1, grp, 3 * rep)
    w_gl = jnp.pad(w_gl, ((0, 0), (0, 0), (0, LANES - 3 * rep))).reshape(-1, grp * LANES).astype(BF16)
    b_gl = gate_b.reshape(3, grp, rep).transpose(1, 0, 2).reshape(grp, 3 * rep)
    b_gl = jnp.pad(b_gl, ((0, 0), (0, LANES - 3 * rep))).reshape(1, grp * LANES)
    gates = gate_proj(x, w_gl, b_gl)

    def view(c0):
        t = p[:, c0:c0 + kv].reshape(bsz, seq, grp, hd).transpose(0, 2, 1, 3)
        return t.reshape(bsz, grp, seq // CMP_STRIDE, CMP_STRIDE * hd)

    pos = cmp_pos.reshape(2, CMP_BLOCK // CMP_STRIDE, CMP_STRIDE * hd)
    kcmp, vcmp = nsa_compress(view(NSA_Q), view(NSA_Q + 3 * kv), pos,
                              cmp_w1.astype(BF16), cmp_w2.astype(BF16))
    tk = 512
    overlap, expand = _nsa_constants(seq, tk)
    o = nsa_attention(p, kcmp, vcmp, gates, overlap, expand, bsz=bsz, seq=seq, tk=tk)
    return out_ln(o, w_out.astype(BF16), x, g, b)


def _conv_kernel(x_ref, wb_ref, wc_ref, wh_ref, cw_ref, wo_ref, g_ref, b_ref, o_ref,
                 xb_ref, acc_ref, tail_ref, *, tiles_per_seq):
    i = pl.program_id(0)
    j = pl.program_id(1)

    @pl.when(j == 0)
    def _():
        xb_ref[...] = x_ref[...].astype(BF16)
        acc_ref[...] = jnp.zeros_like(acc_ref)

    xb = xb_ref[...]
    bg = _dot(xb, wb_ref[...])
    u = _dot(xb, wc_ref[...]) * _dot(xb, wh_ref[...])
    tm = u.shape[0]
    first = (i % tiles_per_seq) == 0
    prev = jnp.where(first, 0.0, tail_ref[j])
    tail_ref[j] = u[tm - 8:tm, :]
    rid = lax.broadcasted_iota(jnp.int32, u.shape, 0)
    u1 = jnp.where(rid == 0, prev[7:8, :], pltpu.roll(u, 1, axis=0))
    u2 = pltpu.roll(u, 2, axis=0)
    u2 = jnp.where(rid == 0, prev[6:7, :], jnp.where(rid == 1, prev[7:8, :], u2))
    cw = cw_ref[...]
    y = cw[0:1, :] * u2 + cw[1:2, :] * u1 + cw[2:3, :] * u
    acc_ref[...] += _dot((bg * y).astype(BF16), wo_ref[...])

    @pl.when(j == pl.num_programs(1) - 1)
    def _():
        yy = DEEPNORM_ALPHA * x_ref[...] + acc_ref[...]
        o_ref[...] = _layer_norm(yy, g_ref[...], b_ref[...])


def conv_mixer_ln(x, w_in, conv_w, w_out, g, b, *, seq, tm=512, tn=512):
    m, d = x.shape
    nj = d // tn
    kern = functools.partial(_conv_kernel, tiles_per_seq=seq // tm)
    return pl.pallas_call(
        kern,
        grid=(m // tm, nj),
        in_specs=[
            pl.BlockSpec((tm, d), lambda i, j: (i, 0)),
            pl.BlockSpec((d, tn), lambda i, j: (0, j)),
            pl.BlockSpec((d, tn), lambda i, j: (0, j + nj)),
            pl.BlockSpec((d, tn), lambda i, j: (0, j + 2 * nj)),
            pl.BlockSpec((CONV_WIDTH, tn), lambda i, j: (0, j)),
            pl.BlockSpec((tn, d), lambda i, j: (j, 0)),
            pl.BlockSpec((1, d), lambda i, j: (0, 0)),
            pl.BlockSpec((1, d), lambda i, j: (0, 0)),
        ],
        out_specs=pl.BlockSpec((tm, d), lambda i, j: (i, 0)),
        out_shape=jax.ShapeDtypeStruct((m, d), F32),
        scratch_shapes=[pltpu.VMEM((tm, d), BF16), pltpu.VMEM((tm, d), F32),
                        pltpu.VMEM((nj, 8, tn), F32)],
        compiler_params=_params(("arbitrary", "arbitrary")),
        name="conv_mixer_ln",
    )(x, w_in, w_in, w_in, conv_w, w_out, g, b)


def _log_sigmoid(z):
    return jnp.minimum(z, 0.0) - jnp.log1p(jnp.exp(-jnp.abs(z)))


def _gla_kernel(q_ref, k_ref, v_ref, r_ref, a_ref, wa_ref, ba_ref, ng_ref, o_ref, st_ref):
    i = pl.program_id(2)
    c_sz = GLA_CHUNK

    @pl.when(i == 0)
    def _():
        st_ref[...] = jnp.zeros_like(st_ref)

    rr = lax.broadcasted_iota(jnp.int32, (c_sz, c_sz), 0)
    cc = lax.broadcasted_iota(jnp.int32, (c_sz, c_sz), 1)
    causal = cc <= rr
    tril = jnp.where(causal, 1.0, 0.0).astype(BF16)
    wa = wa_ref[...]
    ba = ba_ref[...]
    ng = ng_ref[...]
    for c in range(q_ref.shape[0] // c_sz):
        rs = slice(c * c_sz, (c + 1) * c_sz)
        z = _dot(a_ref[rs, :].astype(BF16), wa) + ba
        gk = _log_sigmoid(z) / GLA_GATE_NORM
        hi = gk.astype(BF16)
        r1 = gk - hi.astype(F32)
        mid = r1.astype(BF16)
        lo = (r1 - mid.astype(F32)).astype(BF16)
        bcum = _dot(tril, hi) + _dot(tril, mid) + _dot(tril, lo)
        b_last = bcum[c_sz - 1:c_sz, :]
        qc = q_ref[rs, :] * (GLA_HEAD_K ** -0.5)
        kc = k_ref[rs, :]
        qd = (qc * jnp.exp(bcum)).astype(BF16)
        kd = (kc * jnp.exp(-bcum)).astype(BF16)
        kl = (kc * jnp.exp(b_last - bcum)).astype(BF16)
        vc = v_ref[rs, :].astype(BF16)
        att = jnp.where(causal, _dot_nt(qd, kd), 0.0).astype(BF16)
        st = st_ref[...]
        o = _dot(att, vc) + _dot_nt(qd, st.astype(BF16))
        st_ref[...] = st * jnp.exp(b_last) + _dot_tn(vc, kl)
        o = o * lax.rsqrt(jnp.mean(o * o, axis=-1, keepdims=True) + LN_EPS) * ng
        o_ref[rs, :] = (o * _silu(r_ref[rs, :])).astype(BF16)


def gla_scan(p, a, wa, ba, ng, *, bsz, seq, tt=512):
    m = p.shape[0]
    nh, dk, dv = GLA_HEADS, GLA_HEAD_K, GLA_HEAD_V
    nt = seq // tt
    c_k = GLA_KEY_DIM // dk
    c_v = 2 * GLA_KEY_DIM // dv
    c_r = (2 * GLA_KEY_DIM + GLA_VAL_DIM) // dv
    row = lambda b, h, i: b * nt + i
    return pl.pallas_call(
        _gla_kernel,
        grid=(bsz, nh, nt),
        in_specs=[
            pl.BlockSpec((tt, dk), lambda b, h, i: (row(b, h, i), h)),
            pl.BlockSpec((tt, dk), lambda b, h, i: (row(b, h, i), c_k + h)),
            pl.BlockSpec((tt, dv), lambda b, h, i: (row(b, h, i), c_v + h)),
            pl.BlockSpec((tt, dv), lambda b, h, i: (row(b, h, i), c_r + h)),
            pl.BlockSpec((tt, LANES), lambda b, h, i: (row(b, h, i), 0)),
            pl.BlockSpec((LANES, dk), lambda b, h, i: (0, h)),
            pl.BlockSpec((1, dk), lambda b, h, i: (0, h)),
            pl.BlockSpec((1, dv), lambda b, h, i: (0, 0)),
        ],
        out_specs=pl.BlockSpec((tt, dv), lambda b, h, i: (row(b, h, i), h)),
        out_shape=jax.ShapeDtypeStruct((m, GLA_VAL_DIM), BF16),
        scratch_shapes=[pltpu.VMEM((dv, dk), F32)],
        compiler_params=_params(("parallel", "parallel", "arbitrary")),
        name="gla_scan",
    )(p, p, p, p, a, wa, ba, ng)


def gla_mixer_ln(x, w_in, w_a2, b_a, norm_g, w_out, g, b, *, bsz, seq):
    n_main = 2 * GLA_KEY_DIM + 2 * GLA_VAL_DIM
    dummy = jnp.zeros((x.shape[0], LANES), F32)
    p = proj(x, w_in[:, :n_main].astype(BF16), dummy, dummy)
    w_a = jnp.pad(w_in[:, n_main:], ((0, 0), (0, LANES - GLA_GATE_RANK))).astype(BF16)
    a = proj(x, w_a, dummy, dummy, tn=LANES)
    wa2 = jnp.pad(w_a2, ((0, LANES - GLA_GATE_RANK), (0, 0))).astype(BF16)
    o = gla_scan(p, a, wa2, b_a[None, :], norm_g[None, :], bsz=bsz, seq=seq)
    return out_ln(o, w_out.astype(BF16), x, g, b)


def kernel(x, positions, ln_g, ln_b, ffn_w_in, ffn_w_out, nsa_w_in, nsa_gate_b, nsa_cmp_pos,
           nsa_cmp_w1, nsa_cmp_w2, nsa_w_out, conv_w_in, conv_w, conv_w_out, gla_w_in, gla_w_a2,
           gla_b_a, gla_norm_g, gla_w_out):
    bsz, seq, d = x.shape
    m = bsz * seq
    h = x.reshape(m, d)
    cos, sin = rope_tables(positions.reshape(m, 1).astype(F32))
    for i in range(DEPTH):
        lg = lambda k: ln_g[i, k][None, :]
        lb = lambda k: ln_b[i, k][None, :]
        h = ffn_ln(h, ffn_w_in[i, 0].astype(BF16), ffn_w_out[i, 0].astype(BF16), lg(0), lb(0))
        kind, j = i % N_MIXERS, i // N_MIXERS
        if kind == 0:
            h = nsa_mixer_ln(h, cos, sin, nsa_w_in[j], nsa_gate_b[j], nsa_cmp_pos[j], nsa_cmp_w1[j],
                             nsa_cmp_w2[j], nsa_w_out[j], lg(1), lb(1), bsz=bsz, seq=seq)
        elif kind == 1:
            h = conv_mixer_ln(h, conv_w_in[j].astype(BF16), conv_w[j], conv_w_out[j].astype(BF16),
                              lg(1), lb(1), seq=seq)
        else:
            h = gla_mixer_ln(h, gla_w_in[j], gla_w_a2[j], gla_b_a[j], gla_norm_g[j], gla_w_out[j],
                             lg(1), lb(1), bsz=bsz, seq=seq)
        h = ffn_ln(h, ffn_w_in[i, 1].astype(BF16), ffn_w_out[i, 1].astype(BF16), lg(2), lb(2))
    return h.reshape(bsz, seq, d)
```

```python
import functools

import numpy as np
import jax
import jax.numpy as jnp
from jax import lax
from jax.experimental import pallas as pl
from jax.experimental.pallas import tpu as pltpu

F32 = jnp.float32
BF16 = jnp.bfloat16

D_MODEL = 2048
DEPTH = 4
N_MIXERS = 3
DEEPNORM_ALPHA = (2.0 * DEPTH) ** 0.25
LN_EPS = 1e-5
MACARON_WEIGHT = 0.5
D_FF = 5632

NSA_HEADS = 16
NSA_KV_GROUPS = 4
NSA_REP = NSA_HEADS // NSA_KV_GROUPS
NSA_HEAD_DIM = D_MODEL // NSA_HEADS
NSA_Q = NSA_HEADS * NSA_HEAD_DIM
NSA_KV = NSA_KV_GROUPS * NSA_HEAD_DIM
CMP_BLOCK = 32
CMP_STRIDE = 16
SEL_BLOCK = 64
SEL_TOPK = 16
WINDOW = 512
ROPE_THETA = 10000.0
NEG = -1e30
FORCE = 1e3
LOG2E = float(np.log2(np.e))

CONV_WIDTH = 3

GLA_HEADS = 4
GLA_KEY_DIM = D_MODEL // 2
GLA_VAL_DIM = D_MODEL
GLA_HEAD_K = GLA_KEY_DIM // GLA_HEADS
GLA_HEAD_V = GLA_VAL_DIM // GLA_HEADS
GLA_GATE_RANK = 16
GLA_GATE_NORM = 16.0
GLA_CHUNK = 64
GLA_HEADS_PER_STEP = 2

LANES = 128
FFN_SLABS = 1
PROJ_SLABS = 4
PROJ_WIDE = 1024
VMEM_LIMIT = 56 * 1024 * 1024


def _params(sem):
    return pltpu.CompilerParams(dimension_semantics=sem, vmem_limit_bytes=VMEM_LIMIT)


def _layer_norm(y, g, b):
    mu = jnp.mean(y, axis=-1, keepdims=True)
    d = y - mu
    var = jnp.mean(d * d, axis=-1, keepdims=True)
    return d * lax.rsqrt(var + LN_EPS) * g + b


def _silu(h):
    return h * jax.nn.sigmoid(h)


def _dot(a, b):
    return jnp.dot(a, b, preferred_element_type=F32)


def _dot_nt(a, b):
    return lax.dot_general(a, b, (((1,), (1,)), ((), ())), preferred_element_type=F32)


def _dot_tn(a, b):
    return lax.dot_general(a, b, (((0,), (0,)), ((), ())), preferred_element_type=F32)


def _split3(x):
    hi = x.astype(BF16)
    r1 = x - hi.astype(F32)
    mid = r1.astype(BF16)
    lo = (r1 - mid.astype(F32)).astype(BF16)
    return hi, mid, lo


def _ffn_kernel(x_ref, wg_ref, wu_ref, wo_ref, g_ref, b_ref, o_ref, xb_ref, acc_ref, y_ref):
    i = pl.program_id(0)
    j = pl.program_id(1)
    n_tiles = pl.num_programs(0) - 1
    active = i < n_tiles
    norm_step = (j == 1) & (i > 0)

    @pl.when(active & (j == 0))
    def _():
        xb_ref[...] = x_ref[...].astype(BF16)
        acc_ref[...] = jnp.zeros_like(acc_ref)

    def norm_previous():
        o_ref[...] = _layer_norm(y_ref[...], g_ref[...], b_ref[...])

    def matmuls():
        tm = x_ref.shape[0]
        slabs = [slice(r * tm // FFN_SLABS, (r + 1) * tm // FFN_SLABS) for r in range(FFN_SLABS)]
        hu = [(_dot(xb_ref[rs, :], wg_ref[...]), _dot(xb_ref[rs, :], wu_ref[...])) for rs in slabs]
        for rs, (h, u) in zip(slabs, hu):
            a = (_silu(h) * u).astype(BF16)
            acc_ref[rs, :] += _dot(a, wo_ref[...])

    @pl.when(active & jnp.logical_not(norm_step))
    def _():
        matmuls()

    @pl.when(active & norm_step)
    def _():
        matmuls()
        norm_previous()

    @pl.when(jnp.logical_not(active) & norm_step)
    def _():
        norm_previous()

    @pl.when(active & (j == pl.num_programs(1) - 1))
    def _():
        y_ref[...] = DEEPNORM_ALPHA * x_ref[...] + MACARON_WEIGHT * acc_ref[...]


def ffn_ln(x, w_in, w_out, g, b, layer, half, *, tm=512, tf=512):
    m, d = x.shape
    ff = w_out.shape[2]
    nj = ff // tf
    ni = m // tm

    def wj(i, j):
        return jnp.where(i < ni, j, nj - 1)

    def out_row(i, j):
        return jnp.clip(jnp.where(j >= 2, i, i - 1), 0, ni - 1)

    return pl.pallas_call(
        _ffn_kernel,
        grid=(ni + 1, nj),
        in_specs=[
            pl.BlockSpec((tm, d), lambda i, j: (jnp.minimum(i, ni - 1), 0)),
            pl.BlockSpec((None, None, d, tf), lambda i, j: (layer, half, 0, wj(i, j))),
            pl.BlockSpec((None, None, d, tf), lambda i, j: (layer, half, 0, wj(i, j) + nj)),
            pl.BlockSpec((None, None, tf, d), lambda i, j: (layer, half, wj(i, j), 0)),
            pl.BlockSpec((1, d), lambda i, j: (0, 0)),
            pl.BlockSpec((1, d), lambda i, j: (0, 0)),
        ],
        out_specs=pl.BlockSpec((tm, d), lambda i, j: (out_row(i, j), 0)),
        out_shape=jax.ShapeDtypeStruct((m, d), F32),
        scratch_shapes=[pltpu.VMEM((tm, d), BF16), pltpu.VMEM((tm, d), F32), pltpu.VMEM((tm, d), F32)],
        compiler_params=_params(("arbitrary", "arbitrary")),
        name="ffn_ln",
    )(x, w_in, w_in, w_out, g, b)


def _proj_kernel(x_ref, w_ref, cos_ref, sin_ref, o_ref, xb_ref, *, n_rope, n_scale, scale):
    j = pl.program_id(1)

    @pl.when(j == 0)
    def _():
        xb_ref[...] = x_ref[...].astype(BF16)

    tm, tn = o_ref.shape
    slabs = [slice(r * tm // PROJ_SLABS, (r + 1) * tm // PROJ_SLABS) for r in range(PROJ_SLABS)]
    ys = [_dot(xb_ref[rs, :], w_ref[...]) for rs in slabs]

    if n_rope > 0:
        rotary = j < n_rope
        sc = jnp.where(j < n_scale, scale, 1.0).astype(F32)
        for rs, y in zip(slabs, ys):
            cos = jnp.where(rotary, cos_ref[rs, :], 1.0)
            sin = jnp.where(rotary, sin_ref[rs, :], 0.0)
            for hh in range(tn // LANES):
                t = y[:, hh * LANES:(hh + 1) * LANES]
                rot = pltpu.roll(t, LANES // 2, axis=1)
                o_ref[rs, hh * LANES:(hh + 1) * LANES] = ((t * cos + rot * sin) * sc).astype(o_ref.dtype)
    else:
        for rs, y in zip(slabs, ys):
            o_ref[rs, :] = y.astype(o_ref.dtype)


def proj(x, w, cos, sin, *, n_rope=0, n_scale=0, scale=1.0, out_dtype=F32, tm=1024, tn=512):
    m, d = x.shape
    n = w.shape[1]
    kern = functools.partial(_proj_kernel, n_rope=n_rope, n_scale=n_scale, scale=scale)
    return pl.pallas_call(
        kern,
        grid=(m // tm, n // tn),
        in_specs=[
            pl.BlockSpec((tm, d), lambda i, j: (i, 0)),
            pl.BlockSpec((d, tn), lambda i, j: (0, j)),
            pl.BlockSpec((tm, LANES), lambda i, j: (i, 0)),
            pl.BlockSpec((tm, LANES), lambda i, j: (i, 0)),
        ],
        out_specs=pl.BlockSpec((tm, tn), lambda i, j: (i, j)),
        out_shape=jax.ShapeDtypeStruct((m, n), out_dtype),
        scratch_shapes=[pltpu.VMEM((tm, d), BF16)],
        compiler_params=_params(("parallel", "arbitrary")),
        name="proj",
    )(x, w, cos, sin)


def _rope_table_kernel(pos_ref, inv_ref, sign_ref, cos_ref, sin_ref):
    ang = pos_ref[...] * inv_ref[...]
    cos_ref[...] = jnp.cos(ang)
    sin_ref[...] = jnp.sin(ang) * sign_ref[...]


def rope_tables(pos_f32, *, tm=2048):
    m = pos_f32.shape[0]
    hd = NSA_HEAD_DIM
    inv = ROPE_THETA ** (-jnp.arange(0, hd, 2, dtype=F32) / hd)
    inv_full = jnp.concatenate([inv, inv])[None, :]
    sign = jnp.concatenate([-jnp.ones((hd // 2,), F32), jnp.ones((hd // 2,), F32)])[None, :]
    return pl.pallas_call(
        _rope_table_kernel,
        grid=(m // tm,),
        in_specs=[
            pl.BlockSpec((tm, 1), lambda i: (i, 0)),
            pl.BlockSpec((1, hd), lambda i: (0, 0)),
            pl.BlockSpec((1, hd), lambda i: (0, 0)),
        ],
        out_specs=[pl.BlockSpec((tm, hd), lambda i: (i, 0))] * 2,
        out_shape=[jax.ShapeDtypeStruct((m, hd), F32)] * 2,
        compiler_params=_params(("parallel",)),
        name="rope_tables",
    )(pos_f32, inv_full, sign)


def _gate_kernel(x_ref, w_ref, b_ref, o_ref):
    z = _dot(x_ref[...].astype(BF16), w_ref[...]) + b_ref[...]
    o_ref[...] = jax.nn.sigmoid(z)


def gate_proj(x, w, b, *, tm=1024):
    m, d = x.shape
    n = w.shape[1]
    return pl.pallas_call(
        _gate_kernel,
        grid=(m // tm,),
        in_specs=[
            pl.BlockSpec((tm, d), lambda i: (i, 0)),
            pl.BlockSpec((d, n), lambda i: (0, 0)),
            pl.BlockSpec((1, n), lambda i: (0, 0)),
        ],
        out_specs=pl.BlockSpec((tm, n), lambda i: (i, 0)),
        out_shape=jax.ShapeDtypeStruct((m, n), F32),
        compiler_params=_params(("parallel",)),
        name="gate_proj",
    )(x, w, b)


def _out_ln_kernel(a_ref, w_ref, x_ref, g_ref, b_ref, o_ref):
    y = DEEPNORM_ALPHA * x_ref[...] + _dot(a_ref[...], w_ref[...])
    o_ref[...] = _layer_norm(y, g_ref[...], b_ref[...])


def out_ln(a, w, x, g, b, *, tm=512):
    m, d = x.shape
    k = a.shape[1]
    return pl.pallas_call(
        _out_ln_kernel,
        grid=(m // tm,),
        in_specs=[
            pl.BlockSpec((tm, k), lambda i: (i, 0)),
            pl.BlockSpec((k, d), lambda i: (0, 0)),
            pl.BlockSpec((tm, d), lambda i: (i, 0)),
            pl.BlockSpec((1, d), lambda i: (0, 0)),
            pl.BlockSpec((1, d), lambda i: (0, 0)),
        ],
        out_specs=pl.BlockSpec((tm, d), lambda i: (i, 0)),
        out_shape=jax.ShapeDtypeStruct((m, d), F32),
        compiler_params=_params(("parallel",)),
        name="out_ln",
    )(a, w, x, g, b)


def _gelu_tanh(x):
    c = float(np.sqrt(2.0 / np.pi))
    return 0.5 * x * (1.0 + jnp.tanh(c * (x + 0.044715 * (x * x * x))))


def _cmp_kernel(kc_ref, vc_ref, pos_ref, w1_ref, w2_ref, ko_ref, vo_ref):
    seq, hd = kc_ref.shape
    nrow = seq // CMP_STRIDE
    half = CMP_STRIDE * hd
    for idx, (src, dst) in enumerate(((kc_ref, ko_ref), (vc_ref, vo_ref))):
        first, second = [], []
        for l in range(CMP_STRIDE):
            xl = src[pl.ds(l, nrow, stride=CMP_STRIDE), :]
            first.append(xl + pos_ref[idx, l:l + 1, :])
            second.append(xl + pos_ref[idx, CMP_STRIDE + l:CMP_STRIDE + l + 1, :])
        ya = _dot(jnp.concatenate(first, axis=1).astype(BF16), w1_ref[idx, 0:half, :])
        yb = _dot(jnp.concatenate(second, axis=1).astype(BF16), w1_ref[idx, half:2 * half, :])
        h = _gelu_tanh(ya + pltpu.roll(yb, nrow - 1, axis=0))
        out = _dot(h.astype(BF16), w2_ref[idx])
        if idx == 0:
            dst[0, 0] = out.astype(BF16)
        else:
            dst[0, 0] = out.astype(BF16).astype(F32).T.astype(BF16)


def nsa_compress(pb, pos, w1, w2, *, bsz, seq):
    g = NSA_KV_GROUPS
    hd = NSA_HEAD_DIM
    nrow = seq // CMP_STRIDE
    oblk = pl.BlockSpec((1, 1, nrow, hd), lambda b, gg: (b, gg, 0, 0))
    return pl.pallas_call(
        _cmp_kernel,
        grid=(bsz, g),
        in_specs=[
            pl.BlockSpec((seq, hd), lambda b, gg: (b, gg)),
            pl.BlockSpec((seq, hd), lambda b, gg: (b, g + gg)),
            pl.BlockSpec(pos.shape, lambda b, gg: (0, 0, 0)),
            pl.BlockSpec(w1.shape, lambda b, gg: (0, 0, 0)),
            pl.BlockSpec(w2.shape, lambda b, gg: (0, 0, 0)),
        ],
        out_specs=[oblk, oblk],
        out_shape=[jax.ShapeDtypeStruct((bsz, g, nrow, hd), BF16)] * 2,
        compiler_params=_params(("parallel", "parallel")),
        name="nsa_compress",
    )(pb, pb, pos, w1, w2)


def _nsa_attn_kernel(q_ref, ks_ref, kw_ref, vs_ref, vw_ref, kc_ref, vct_ref, gate_ref,
                     ovt_ref, hot_ref, o_ref, ksa, vst, kwp, vwt, pbuf, *, tq, tk, seq):
    i = pl.program_id(2)
    hd = NSA_HEAD_DIM
    rep = NSA_REP
    cols = rep * tq
    nblk = seq // tq
    wblk = WINDOW // tq

    @pl.when(i == 0)
    def _():
        ksa[:, 0:hd] = ks_ref[...]
        ksa[:, hd:2 * hd] = hot_ref[...]
        kwp[0:WINDOW, :] = jnp.zeros((WINDOW, hd), BF16)
        kwp[WINDOW:WINDOW + seq, :] = kw_ref[...]
        for blk in range(wblk):
            vwt[blk] = jnp.zeros((hd, tq), BF16)
        for blk in range(nblk):
            rs = slice(blk * tq, (blk + 1) * tq)
            vst[blk] = vs_ref[rs, :].astype(F32).T.astype(BF16)
            vwt[wblk + blk] = vw_ref[rs, :].astype(F32).T.astype(BF16)

    t0 = i * tq
    blk0 = i
    q_t = jnp.concatenate([q_ref[:, r * hd:(r + 1) * hd].astype(F32).T for r in range(rep)],
                          axis=1).astype(BF16)

    def lane_t(shape):
        return t0 + (lax.broadcasted_iota(jnp.int32, shape, 1) & (tq - 1))

    def softmax_cols(s):
        mx = jnp.max(s, axis=0, keepdims=True)
        p = jnp.exp2(s - mx)
        return mx, p, jnp.sum(p, axis=0, keepdims=True)

    ku = lax.broadcasted_iota(jnp.int32, (tq, cols), 0)
    qu = lax.broadcasted_iota(jnp.int32, (tq, cols), 1) & (tq - 1)
    bias_diag = jnp.where(ku <= qu, 0.0, NEG)
    bias_first = jnp.where(ku > qu, 0.0, NEG)

    ncmp = kc_ref.shape[2]
    band = WINDOW + tq
    dstart = pl.multiple_of(t0, tq)
    s = _dot(kc_ref[0, 0], q_t)
    diag_forced = tq <= 2 * SEL_BLOCK
    if diag_forced:
        sd = _dot(ksa[pl.ds(dstart, tq), 0:hd], q_t) + bias_diag
    sw = _dot(kwp[pl.ds(dstart, band), :], q_t)

    n_idx = lax.broadcasted_iota(jnp.int32, (ncmp, cols), 0)
    valid = n_idx * CMP_STRIDE + (CMP_BLOCK - 1) <= lane_t((ncmp, cols))
    sm = jnp.where(valid, s, NEG)
    mx = jnp.max(sm, axis=0, keepdims=True)
    p = jnp.where(valid, jnp.exp2(sm - mx), 0.0)
    den = jnp.sum(p, axis=0, keepdims=True)
    p = p * (1.0 / jnp.where(den > 0.0, den, 1.0))
    o_cmp = _dot(vct_ref[0, 0], p.astype(BF16))

    pieces = []
    for jb in range(wblk + 1):
        piece = sw[jb * tq:(jb + 1) * tq]
        if jb == wblk:
            piece = piece + bias_diag
        else:
            piece = piece + jnp.where(t0 - WINDOW + jb * tq < 0, NEG, 0.0).astype(F32)
            if jb == 0:
                piece = piece + bias_first
        pieces.append(piece)
    _, pw, l_w = softmax_cols(jnp.concatenate(pieces, axis=0))
    vw_t = jnp.concatenate([vwt[blk0 + k] for k in range(wblk + 1)], axis=1)
    o_win = _dot(vw_t, pw.astype(BF16)) * (1.0 / l_w)

    psum = p[:, 0:tq]
    for r in range(1, rep):
        psum = psum + p[:, r * tq:(r + 1) * tq]
    hi, mid, lo = _split3(psum)
    ovt = ovt_ref[...]
    imp = _dot(ovt, hi) + _dot(ovt, mid) + _dot(ovt, lo)
    n_sel = seq // SEL_BLOCK
    b_idx = lax.broadcasted_iota(jnp.int32, (n_sel, tq), 0)
    cur = (t0 + lax.broadcasted_iota(jnp.int32, (n_sel, tq), 1)) // SEL_BLOCK
    forced = (b_idx == 0) | (b_idx == cur) | (b_idx == cur - 1)
    score = jnp.where(b_idx <= cur, imp[0:n_sel] + jnp.where(forced, FORCE, 0.0), NEG)
    rank = jnp.zeros((n_sel, tq), F32)
    for mp in range(n_sel):
        row = score[mp:mp + 1, :]
        beats = (row > score) | ((row == score) & (b_idx > mp))
        rank = rank + jnp.where(beats, 1.0, 0.0)
    selected = rank < float(SEL_TOPK)
    before_tile = b_idx < t0 // SEL_BLOCK

    def augment(keep):
        bias_blk = jnp.where(keep, 0.0, NEG)
        bias_blk = jnp.concatenate([bias_blk, jnp.zeros((hd - n_sel, tq), F32)], axis=0).astype(BF16)
        return jnp.concatenate([q_t, jnp.concatenate([bias_blk] * rep, axis=1)], axis=0)

    q_aug = augment(selected & before_tile)
    if not diag_forced:
        sd = _dot(ksa[pl.ds(dstart, tq), :], augment(selected & jnp.logical_not(before_tile))) + bias_diag

    m_i, pd, l_i = softmax_cols(sd)
    acc = _dot(vst[blk0], pd.astype(BF16))
    per = tk // tq
    pbuf[...] = jnp.zeros_like(pbuf)

    def v_chunk(c):
        return jnp.concatenate([vst[c * per + k] for k in range(per)], axis=1)

    n_chunks = (t0 + tk - 1) // tk
    last = jnp.maximum(n_chunks - 1, 0)

    def sel_body(c, carry):
        m_i, l_i, acc = carry
        off = pl.multiple_of(c * tk, tk)
        sc = _dot(ksa[pl.ds(off, tk), :], q_aug)
        pv = _dot(v_chunk(jnp.maximum(c - 1, 0)), pbuf[...])
        m_new = jnp.maximum(m_i, jnp.max(sc, axis=0, keepdims=True))
        pp = jnp.exp2(sc - m_new)
        pbuf[...] = pp.astype(BF16)
        alpha = jnp.exp2(m_i - m_new)
        l_new = alpha * l_i + jnp.sum(pp, axis=0, keepdims=True)
        return m_new, l_new, alpha * (acc + pv)

    _, l_s, acc_s = lax.fori_loop(0, n_chunks, sel_body, (m_i, l_i, acc))
    acc_s = acc_s + _dot(v_chunk(last), pbuf[...])
    o_sel = acc_s * (1.0 / l_s)

    g_t = gate_ref[...].T
    for r in range(rep):
        cs = slice(r * tq, (r + 1) * tq)
        o_r = (g_t[r:r + 1, :] * o_cmp[:, cs]
               + g_t[rep + r:rep + r + 1, :] * o_sel[:, cs]
               + g_t[2 * rep + r:2 * rep + r + 1, :] * o_win[:, cs])
        o_ref[:, r * hd:(r + 1) * hd] = o_r.T.astype(BF16)


def nsa_attention(pa, kcmp, vcmp_t, gates, overlap_t, onehot, *, bsz, seq, tq=256, tk=512):
    m = pa.shape[0]
    hd = NSA_HEAD_DIM
    g = NSA_KV_GROUPS
    nq = seq // tq
    gw = NSA_REP * hd
    c_ks = NSA_Q // hd
    c_kw = (NSA_Q + NSA_KV) // hd
    c_vs = (NSA_Q + 2 * NSA_KV) // hd
    c_vw = (NSA_Q + 3 * NSA_KV) // hd
    ncmp = kcmp.shape[2]

    def kv_spec(c0):
        return pl.BlockSpec((seq, hd), lambda b, gg, i: (b, c0 + gg))

    kern = functools.partial(_nsa_attn_kernel, tq=tq, tk=tk, seq=seq)
    return pl.pallas_call(
        kern,
        grid=(bsz, g, nq),
        in_specs=[
            pl.BlockSpec((tq, gw), lambda b, gg, i: (b * nq + i, gg)),
            kv_spec(c_ks), kv_spec(c_kw), kv_spec(c_vs), kv_spec(c_vw),
            pl.BlockSpec((1, 1, ncmp, hd), lambda b, gg, i: (b, gg, 0, 0)),
            pl.BlockSpec((1, 1, hd, ncmp), lambda b, gg, i: (b, gg, 0, 0)),
            pl.BlockSpec((tq, LANES), lambda b, gg, i: (b * nq + i, gg)),
            pl.BlockSpec(overlap_t.shape, lambda b, gg, i: (0, 0)),
            pl.BlockSpec(onehot.shape, lambda b, gg, i: (0, 0)),
        ],
        out_specs=pl.BlockSpec((tq, gw), lambda b, gg, i: (b * nq + i, gg)),
        out_shape=jax.ShapeDtypeStruct((m, NSA_Q), BF16),
        scratch_shapes=[
            pltpu.VMEM((seq, 2 * hd), BF16),
            pltpu.VMEM((seq // tq, hd, tq), BF16),
            pltpu.VMEM((seq + WINDOW, hd), BF16),
            pltpu.VMEM((seq // tq + WINDOW // tq, hd, tq), BF16),
            pltpu.VMEM((tk, NSA_REP * tq), BF16),
        ],
        compiler_params=_params(("parallel", "parallel", "arbitrary")),
        name="nsa_attention",
    )(pa, pa, pa, pa, pa, kcmp, vcmp_t, gates, overlap_t, onehot)


def _nsa_constants(seq):
    mm = np.arange(LANES)[:, None]
    n = np.arange(LANES)[None, :]
    c_start = n * CMP_STRIDE
    s_start = mm * SEL_BLOCK
    n_cmp = (seq - CMP_BLOCK) // CMP_STRIDE + 1
    overlap_t = ((c_start < s_start + SEL_BLOCK) & (c_start + CMP_BLOCK > s_start)
                 & (n < n_cmp) & (mm < seq // SEL_BLOCK))
    onehot = (np.arange(seq)[:, None] // SEL_BLOCK) == np.arange(LANES)[None, :]
    return (jnp.asarray(overlap_t.astype(np.float32), dtype=BF16),
            jnp.asarray(onehot.astype(np.float32), dtype=BF16))


def nsa_mixer_ln(x, cos, sin, w_in, gate_b, cmp_pos, cmp_w1, cmp_w2, w_out, g, b, *, bsz, seq):
    hd = NSA_HEAD_DIM
    grp = NSA_KV_GROUPS
    rep = NSA_REP
    kv = NSA_KV

    def wcols(order):
        return jnp.concatenate([w_in[:, NSA_Q + i * kv:NSA_Q + (i + 1) * kv] for i in order], axis=1)

    w_a = jnp.concatenate([w_in[:, :NSA_Q], wcols((2, 4, 3, 5))], axis=1).astype(BF16)
    pa = proj(x, w_a, cos, sin, n_rope=(NSA_Q + 2 * kv) // PROJ_WIDE, n_scale=NSA_Q // PROJ_WIDE,
              scale=hd ** -0.5 * LOG2E, out_dtype=BF16, tn=PROJ_WIDE)
    pb = proj(x, wcols((0, 1)).astype(BF16), cos, sin, n_rope=1, tn=kv)

    w_gl = w_in[:, NSA_Q + 6 * kv:].reshape(-1, 3, grp, rep).transpose(0, 2, 1, 3).reshape(-1, grp, 3 * rep)
    w_gl = jnp.pad(w_gl, ((0, 0), (0, 0), (0, LANES - 3 * rep))).reshape(-1, grp * LANES).astype(BF16)
    b_gl = gate_b.reshape(3, grp, rep).transpose(1, 0, 2).reshape(grp, 3 * rep)
    b_gl = jnp.pad(b_gl, ((0, 0), (0, LANES - 3 * rep))).reshape(1, grp * LANES)
    gates = gate_proj(x, w_gl, b_gl)

    kcmp, vcmp_t = nsa_compress(pb, cmp_pos, cmp_w1.astype(BF16), cmp_w2.astype(BF16), bsz=bsz, seq=seq)
    overlap_t, onehot = _nsa_constants(seq)
    o = nsa_attention(pa, kcmp, vcmp_t, gates, overlap_t, onehot, bsz=bsz, seq=seq)
    return out_ln(o, w_out.astype(BF16), x, g, b)


def _conv_kernel(x_ref, wb_ref, wc_ref, wh_ref, cw_ref, wo_ref, g_ref, b_ref, o_ref,
                 xb_ref, acc_ref, tail_ref, *, tiles_per_seq):
    i = pl.program_id(0)
    j = pl.program_id(1)

    @pl.when(j == 0)
    def _():
        xb_ref[...] = x_ref[...].astype(BF16)
        acc_ref[...] = jnp.zeros_like(acc_ref)

    xb = xb_ref[...]
    bg = _dot(xb, wb_ref[...])
    u = _dot(xb, wc_ref[...]) * _dot(xb, wh_ref[...])
    tm = u.shape[0]
    first = (i % tiles_per_seq) == 0
    prev = jnp.where(first, 0.0, tail_ref[j])
    tail_ref[j] = u[tm - 8:tm, :]
    rid = lax.broadcasted_iota(jnp.int32, u.shape, 0)
    u1 = jnp.where(rid == 0, prev[7:8, :], pltpu.roll(u, 1, axis=0))
    u2 = pltpu.roll(u, 2, axis=0)
    u2 = jnp.where(rid == 0, prev[6:7, :], jnp.where(rid == 1, prev[7:8, :], u2))
    cw = cw_ref[...]
    y = cw[0:1, :] * u2 + cw[1:2, :] * u1 + cw[2:3, :] * u
    acc_ref[...] += _dot((bg * y).astype(BF16), wo_ref[...])

    @pl.when(j == pl.num_programs(1) - 1)
    def _():
        yy = DEEPNORM_ALPHA * x_ref[...] + acc_ref[...]
        o_ref[...] = _layer_norm(yy, g_ref[...], b_ref[...])


def conv_mixer_ln(x, w_in, conv_w, w_out, g, b, *, seq, tm=512, tn=512):
    m, d = x.shape
    nj = d // tn
    kern = functools.partial(_conv_kernel, tiles_per_seq=seq // tm)
    return pl.pallas_call(
        kern,
        grid=(m // tm, nj),
        in_specs=[
            pl.BlockSpec((tm, d), lambda i, j: (i, 0)),
            pl.BlockSpec((d, tn), lambda i, j: (0, j)),
            pl.BlockSpec((d, tn), lambda i, j: (0, j + nj)),
            pl.BlockSpec((d, tn), lambda i, j: (0, j + 2 * nj)),
            pl.BlockSpec((CONV_WIDTH, tn), lambda i, j: (0, j)),
            pl.BlockSpec((tn, d), lambda i, j: (j, 0)),
            pl.BlockSpec((1, d), lambda i, j: (0, 0)),
            pl.BlockSpec((1, d), lambda i, j: (0, 0)),
        ],
        out_specs=pl.BlockSpec((tm, d), lambda i, j: (i, 0)),
        out_shape=jax.ShapeDtypeStruct((m, d), F32),
        scratch_shapes=[pltpu.VMEM((tm, d), BF16), pltpu.VMEM((tm, d), F32),
                        pltpu.VMEM((nj, 8, tn), F32)],
        compiler_params=_params(("arbitrary", "arbitrary")),
        name="conv_mixer_ln",
    )(x, w_in, w_in, w_in, conv_w, w_out, g, b)


def _log_sigmoid(z):
    return jnp.minimum(z, 0.0) - jnp.log1p(jnp.exp(-jnp.abs(z)))


def _gla_kernel(q_ref, k_ref, v_ref, r_ref, a_ref, wa_ref, ba_ref, ng_ref, o_ref, st_ref, upd_ref):
    i = pl.program_id(2)
    c_sz = GLA_CHUNK
    dk, dv = GLA_HEAD_K, GLA_HEAD_V
    n_c = q_ref.shape[0] // c_sz
    heads = range(GLA_HEADS_PER_STEP)

    @pl.when(i == 0)
    def _():
        st_ref[...] = jnp.zeros_like(st_ref)

    rr = lax.broadcasted_iota(jnp.int32, (c_sz, c_sz), 0)
    cc = lax.broadcasted_iota(jnp.int32, (c_sz, c_sz), 1)
    causal = cc <= rr
    tril = jnp.where(causal, 1.0, 0.0).astype(BF16)
    chunks = [slice(c * c_sz, (c + 1) * c_sz) for c in range(n_c)]

    z = _dot(a_ref[...].astype(BF16), wa_ref[...]) + ba_ref[...]
    gk = _log_sigmoid(z) / GLA_GATE_NORM
    hi, mid, lo = _split3(gk)
    bcum = jnp.concatenate(
        [_dot(tril, hi[rs]) + _dot(tril, mid[rs]) + _dot(tril, lo[rs]) for rs in chunks], axis=0)
    b_last = [bcum[rs][c_sz - 1:c_sz, :] for rs in chunks]
    b_last_rows = jnp.concatenate([jnp.broadcast_to(bl, (c_sz, bl.shape[1])) for bl in b_last], axis=0)
    qd = (q_ref[...] * (dk ** -0.5) * jnp.exp(bcum)).astype(BF16)
    k = k_ref[...]
    kd = (k * jnp.exp(-bcum)).astype(BF16)
    kl = (k * jnp.exp(b_last_rows - bcum)).astype(BF16)
    v = v_ref[...].astype(BF16)

    o_intra = []
    for hh in heads:
        ks, vs = slice(hh * dk, (hh + 1) * dk), slice(hh * dv, (hh + 1) * dv)
        outs = []
        for c, rs in enumerate(chunks):
            att = jnp.where(causal, _dot_nt(qd[rs, ks], kd[rs, ks]), 0.0).astype(BF16)
            outs.append(_dot(att, v[rs, vs]))
            upd_ref[hh, c] = _dot_tn(v[rs, vs], kl[rs, ks])
        o_intra.append(outs)

    st = [st_ref[hh] for hh in heads]
    ng = ng_ref[...]
    for c, rs in enumerate(chunks):
        for hh in heads:
            ks, vs = slice(hh * dk, (hh + 1) * dk), slice(hh * dv, (hh + 1) * dv)
            o = o_intra[hh][c] + _dot_nt(qd[rs, ks], st[hh].astype(BF16))
            st[hh] = st[hh] * jnp.exp(b_last[c][:, ks]) + upd_ref[hh, c]
            o = o * lax.rsqrt(jnp.mean(o * o, axis=-1, keepdims=True) + LN_EPS) * ng
            o_ref[rs, vs] = (o * _silu(r_ref[rs, vs])).astype(BF16)
    for hh in heads:
        st_ref[hh] = st[hh]


def gla_scan(p, a, wa, ba, ng, *, bsz, seq, tt=512):
    m = p.shape[0]
    dk, dv = GLA_HEAD_K, GLA_HEAD_V
    hp = GLA_HEADS_PER_STEP
    nt = seq // tt
    c_k = GLA_KEY_DIM // (hp * dk)
    c_v = 2 * GLA_KEY_DIM // (hp * dv)
    c_r = (2 * GLA_KEY_DIM + GLA_VAL_DIM) // (hp * dv)
    row = lambda b, h, i: b * nt + i
    return pl.pallas_call(
        _gla_kernel,
        grid=(bsz, GLA_HEADS // hp, nt),
        in_specs=[
            pl.BlockSpec((tt, hp * dk), lambda b, h, i: (row(b, h, i), h)),
            pl.BlockSpec((tt, hp * dk), lambda b, h, i: (row(b, h, i), c_k + h)),
            pl.BlockSpec((tt, hp * dv), lambda b, h, i: (row(b, h, i), c_v + h)),
            pl.BlockSpec((tt, hp * dv), lambda b, h, i: (row(b, h, i), c_r + h)),
            pl.BlockSpec((tt, LANES), lambda b, h, i: (row(b, h, i), 0)),
            pl.BlockSpec((LANES, hp * dk), lambda b, h, i: (0, h)),
            pl.BlockSpec((1, hp * dk), lambda b, h, i: (0, h)),
            pl.BlockSpec((1, dv), lambda b, h, i: (0, 0)),
        ],
        out_specs=pl.BlockSpec((tt, hp * dv), lambda b, h, i: (row(b, h, i), h)),
        out_shape=jax.ShapeDtypeStruct((m, GLA_VAL_DIM), BF16),
        scratch_shapes=[pltpu.VMEM((hp, dv, dk), F32),
                        pltpu.VMEM((hp, tt // GLA_CHUNK, dv, dk), F32)],
        compiler_params=_params(("parallel", "parallel", "arbitrary")),
        name="gla_scan",
    )(p, p, p, p, a, wa, ba, ng)


def gla_mixer_ln(x, w_in, w_a2, b_a, norm_g, w_out, g, b, *, bsz, seq):
    n_main = 2 * GLA_KEY_DIM + 2 * GLA_VAL_DIM
    dummy = jnp.zeros((x.shape[0], LANES), F32)
    p = proj(x, w_in[:, :n_main].astype(BF16), dummy, dummy, tn=PROJ_WIDE)
    w_a = jnp.pad(w_in[:, n_main:], ((0, 0), (0, LANES - GLA_GATE_RANK))).astype(BF16)
    a = proj(x, w_a, dummy, dummy, tn=LANES)
    wa2 = jnp.pad(w_a2, ((0, LANES - GLA_GATE_RANK), (0, 0))).astype(BF16)
    o = gla_scan(p, a, wa2, b_a[None, :], norm_g[None, :], bsz=bsz, seq=seq)
    return out_ln(o, w_out.astype(BF16), x, g, b)


def kernel(x, positions, ln_g, ln_b, ffn_w_in, ffn_w_out, nsa_w_in, nsa_gate_b, nsa_cmp_pos,
           nsa_cmp_w1, nsa_cmp_w2, nsa_w_out, conv_w_in, conv_w, conv_w_out, gla_w_in, gla_w_a2,
           gla_b_a, gla_norm_g, gla_w_out):
    bsz, seq, d = x.shape
    m = bsz * seq
    h = x.reshape(m, d)
    cos, sin = rope_tables(positions.reshape(m, 1).astype(F32))
    ffn_in = ffn_w_in.astype(BF16)
    ffn_out = ffn_w_out.astype(BF16)
    for i in range(DEPTH):
        lg = lambda k: ln_g[i, k][None, :]
        lb = lambda k: ln_b[i, k][None, :]
        h = ffn_ln(h, ffn_in, ffn_out, lg(0), lb(0), i, 0)
        kind, j = i % N_MIXERS, i // N_MIXERS
        if kind == 0:
            h = nsa_mixer_ln(h, cos, sin, nsa_w_in[j], nsa_gate_b[j], nsa_cmp_pos[j], nsa_cmp_w1[j],
                             nsa_cmp_w2[j], nsa_w_out[j], lg(1), lb(1), bsz=bsz, seq=seq)
        elif kind == 1:
            h = conv_mixer_ln(h, conv_w_in[j].astype(BF16), conv_w[j], conv_w_out[j].astype(BF16),
                              lg(1), lb(1), seq=seq)
        else:
            h = gla_mixer_ln(h, gla_w_in[j], gla_w_a2[j], gla_b_a[j], gla_norm_g[j], gla_w_out[j],
                             lg(1), lb(1), bsz=bsz, seq=seq)
        h = ffn_ln(h, ffn_in, ffn_out, lg(2), lb(2), i, 1)
    return h.reshape(bsz, seq, d)
```

```python
import functools

import numpy as np
import jax
import jax.numpy as jnp
from jax import lax
from jax.experimental import pallas as pl
from jax.experimental.pallas import tpu as pltpu

F32 = jnp.float32
BF16 = jnp.bfloat16

D_MODEL = 2048
DEPTH = 4
N_MIXERS = 3
DEEPNORM_ALPHA = (2.0 * DEPTH) ** 0.25
LN_EPS = 1e-5
MACARON_WEIGHT = 0.5
D_FF = 5632

NSA_HEADS = 16
NSA_KV_GROUPS = 4
NSA_REP = NSA_HEADS // NSA_KV_GROUPS
NSA_HEAD_DIM = D_MODEL // NSA_HEADS
NSA_Q = NSA_HEADS * NSA_HEAD_DIM
NSA_KV = NSA_KV_GROUPS * NSA_HEAD_DIM
CMP_BLOCK = 32
CMP_STRIDE = 16
SEL_BLOCK = 64
SEL_TOPK = 16
WINDOW = 512
ROPE_THETA = 10000.0
NEG = -1e30
FORCE = 1e3
LOG2E = float(np.log2(np.e))

CONV_WIDTH = 3

GLA_HEADS = 4
GLA_KEY_DIM = D_MODEL // 2
GLA_VAL_DIM = D_MODEL
GLA_HEAD_K = GLA_KEY_DIM // GLA_HEADS
GLA_HEAD_V = GLA_VAL_DIM // GLA_HEADS
GLA_GATE_RANK = 16
GLA_GATE_NORM = 16.0
GLA_CHUNK = 64
GLA_HEADS_PER_STEP = 2

LANES = 128
PROJ_SLABS = 4
OUT_SLABS = 2
PROJ_WIDE = 1024
VMEM_LIMIT = 56 * 1024 * 1024


def _params(sem):
    return pltpu.CompilerParams(dimension_semantics=sem, vmem_limit_bytes=VMEM_LIMIT)


def _layer_norm(y, g, b):
    mu = jnp.mean(y, axis=-1, keepdims=True)
    d = y - mu
    var = jnp.mean(d * d, axis=-1, keepdims=True)
    return d * lax.rsqrt(var + LN_EPS) * g + b


def _silu(h):
    return h * jax.nn.sigmoid(h)


def _dot(a, b):
    return jnp.dot(a, b, preferred_element_type=F32)


def _dot_nt(a, b):
    return lax.dot_general(a, b, (((1,), (1,)), ((), ())), preferred_element_type=F32)


def _dot_tn(a, b):
    return lax.dot_general(a, b, (((0,), (0,)), ((), ())), preferred_element_type=F32)


def _split3(x):
    hi = x.astype(BF16)
    r1 = x - hi.astype(F32)
    mid = r1.astype(BF16)
    lo = (r1 - mid.astype(F32)).astype(BF16)
    return hi, mid, lo


def _ffn_kernel(x_ref, wg_ref, wu_ref, wo_ref, g_ref, b_ref, o_ref, xb_ref, acc_ref):
    j = pl.program_id(1)

    @pl.when(j == 0)
    def _():
        xb_ref[...] = x_ref[...].astype(BF16)
        acc_ref[...] = jnp.zeros_like(acc_ref)

    xb = xb_ref[...]
    h = _dot(xb, wg_ref[...])
    u = _dot(xb, wu_ref[...])
    a = (_silu(h) * u).astype(BF16)
    acc_ref[...] += _dot(a, wo_ref[...])

    @pl.when(j == pl.num_programs(1) - 1)
    def _():
        y = DEEPNORM_ALPHA * x_ref[...] + MACARON_WEIGHT * acc_ref[...]
        o_ref[...] = _layer_norm(y, g_ref[...], b_ref[...])


def ffn_ln(x, w_in, w_out, g, b, layer, half, *, tm=512, tf=512):
    m, d = x.shape
    ff = w_out.shape[2]
    nj = ff // tf
    return pl.pallas_call(
        _ffn_kernel,
        grid=(m // tm, nj),
        in_specs=[
            pl.BlockSpec((tm, d), lambda i, j: (i, 0)),
            pl.BlockSpec((None, None, d, tf), lambda i, j: (layer, half, 0, j)),
            pl.BlockSpec((None, None, d, tf), lambda i, j: (layer, half, 0, j + nj)),
            pl.BlockSpec((None, None, tf, d), lambda i, j: (layer, half, j, 0)),
            pl.BlockSpec((1, d), lambda i, j: (0, 0)),
            pl.BlockSpec((1, d), lambda i, j: (0, 0)),
        ],
        out_specs=pl.BlockSpec((tm, d), lambda i, j: (i, 0)),
        out_shape=jax.ShapeDtypeStruct((m, d), F32),
        scratch_shapes=[pltpu.VMEM((tm, d), BF16), pltpu.VMEM((tm, d), F32)],
        compiler_params=_params(("parallel", "arbitrary")),
        name="ffn_ln",
    )(x, w_in, w_in, w_out, g, b)


def _proj_kernel(x_ref, w_ref, cos_ref, sin_ref, o_ref, xb_ref, *, n_rope, n_scale, scale):
    j = pl.program_id(1)

    @pl.when(j == 0)
    def _():
        xb_ref[...] = x_ref[...].astype(BF16)

    tm, tn = o_ref.shape
    slabs = [slice(r * tm // PROJ_SLABS, (r + 1) * tm // PROJ_SLABS) for r in range(PROJ_SLABS)]
    ys = [_dot(xb_ref[rs, :], w_ref[...]) for rs in slabs]

    if n_rope > 0:
        rotary = j < n_rope
        sc = jnp.where(j < n_scale, scale, 1.0).astype(F32)
        for rs, y in zip(slabs, ys):
            cos = jnp.where(rotary, cos_ref[rs, :], 1.0)
            sin = jnp.where(rotary, sin_ref[rs, :], 0.0)
            for hh in range(tn // LANES):
                t = y[:, hh * LANES:(hh + 1) * LANES]
                rot = pltpu.roll(t, LANES // 2, axis=1)
                o_ref[rs, hh * LANES:(hh + 1) * LANES] = ((t * cos + rot * sin) * sc).astype(o_ref.dtype)
    else:
        for rs, y in zip(slabs, ys):
            o_ref[rs, :] = y.astype(o_ref.dtype)


def proj(x, w, cos, sin, *, n_rope=0, n_scale=0, scale=1.0, out_dtype=F32, tm=1024, tn=512):
    m, d = x.shape
    n = w.shape[1]
    kern = functools.partial(_proj_kernel, n_rope=n_rope, n_scale=n_scale, scale=scale)
    return pl.pallas_call(
        kern,
        grid=(m // tm, n // tn),
        in_specs=[
            pl.BlockSpec((tm, d), lambda i, j: (i, 0)),
            pl.BlockSpec((d, tn), lambda i, j: (0, j)),
            pl.BlockSpec((tm, LANES), lambda i, j: (i, 0)),
            pl.BlockSpec((tm, LANES), lambda i, j: (i, 0)),
        ],
        out_specs=pl.BlockSpec((tm, tn), lambda i, j: (i, j)),
        out_shape=jax.ShapeDtypeStruct((m, n), out_dtype),
        scratch_shapes=[pltpu.VMEM((tm, d), BF16)],
        compiler_params=_params(("parallel", "arbitrary")),
        name="proj",
    )(x, w, cos, sin)


def _rope_table_kernel(pos_ref, inv_ref, sign_ref, cos_ref, sin_ref):
    ang = pos_ref[...] * inv_ref[...]
    cos_ref[...] = jnp.cos(ang)
    sin_ref[...] = jnp.sin(ang) * sign_ref[...]


def rope_tables(pos_f32, *, tm=2048):
    m = pos_f32.shape[0]
    hd = NSA_HEAD_DIM
    inv = ROPE_THETA ** (-jnp.arange(0, hd, 2, dtype=F32) / hd)
    inv_full = jnp.concatenate([inv, inv])[None, :]
    sign = jnp.concatenate([-jnp.ones((hd // 2,), F32), jnp.ones((hd // 2,), F32)])[None, :]
    return pl.pallas_call(
        _rope_table_kernel,
        grid=(m // tm,),
        in_specs=[
            pl.BlockSpec((tm, 1), lambda i: (i, 0)),
            pl.BlockSpec((1, hd), lambda i: (0, 0)),
            pl.BlockSpec((1, hd), lambda i: (0, 0)),
        ],
        out_specs=[pl.BlockSpec((tm, hd), lambda i: (i, 0))] * 2,
        out_shape=[jax.ShapeDtypeStruct((m, hd), F32)] * 2,
        compiler_params=_params(("parallel",)),
        name="rope_tables",
    )(pos_f32, inv_full, sign)


def _gate_kernel(x_ref, w_ref, b_ref, o_ref):
    z = _dot(x_ref[...].astype(BF16), w_ref[...]) + b_ref[...]
    o_ref[...] = jax.nn.sigmoid(z)


def gate_proj(x, w, b, *, tm=1024):
    m, d = x.shape
    n = w.shape[1]
    return pl.pallas_call(
        _gate_kernel,
        grid=(m // tm,),
        in_specs=[
            pl.BlockSpec((tm, d), lambda i: (i, 0)),
            pl.BlockSpec((d, n), lambda i: (0, 0)),
            pl.BlockSpec((1, n), lambda i: (0, 0)),
        ],
        out_specs=pl.BlockSpec((tm, n), lambda i: (i, 0)),
        out_shape=jax.ShapeDtypeStruct((m, n), F32),
        compiler_params=_params(("parallel",)),
        name="gate_proj",
    )(x, w, b)


def _out_ln_kernel(a_ref, w_ref, x_ref, g_ref, b_ref, o_ref):
    tm = a_ref.shape[0]
    slabs = [slice(r * tm // OUT_SLABS, (r + 1) * tm // OUT_SLABS) for r in range(OUT_SLABS)]
    ys = [_dot(a_ref[rs, :], w_ref[...]) for rs in slabs]
    for rs, y in zip(slabs, ys):
        o_ref[rs, :] = _layer_norm(DEEPNORM_ALPHA * x_ref[rs, :] + y, g_ref[...], b_ref[...])


def out_ln(a, w, x, g, b, *, tm=512):
    m, d = x.shape
    k = a.shape[1]
    return pl.pallas_call(
        _out_ln_kernel,
        grid=(m // tm,),
        in_specs=[
            pl.BlockSpec((tm, k), lambda i: (i, 0)),
            pl.BlockSpec((k, d), lambda i: (0, 0)),
            pl.BlockSpec((tm, d), lambda i: (i, 0)),
            pl.BlockSpec((1, d), lambda i: (0, 0)),
            pl.BlockSpec((1, d), lambda i: (0, 0)),
        ],
        out_specs=pl.BlockSpec((tm, d), lambda i: (i, 0)),
        out_shape=jax.ShapeDtypeStruct((m, d), F32),
        compiler_params=_params(("parallel",)),
        name="out_ln",
    )(a, w, x, g, b)


def _gelu_tanh(x):
    c = float(np.sqrt(2.0 / np.pi))
    return 0.5 * x * (1.0 + jnp.tanh(c * (x + 0.044715 * (x * x * x))))


def _cmp_kernel(kc_ref, vc_ref, pos_ref, w1_ref, w2_ref, ko_ref, vo_ref):
    seq, hd = kc_ref.shape
    nrow = seq // CMP_STRIDE
    half = CMP_STRIDE * hd
    for idx, (src, dst) in enumerate(((kc_ref, ko_ref), (vc_ref, vo_ref))):
        first, second = [], []
        for l in range(CMP_STRIDE):
            xl = src[pl.ds(l, nrow, stride=CMP_STRIDE), :]
            first.append(xl + pos_ref[idx, l:l + 1, :])
            second.append(xl + pos_ref[idx, CMP_STRIDE + l:CMP_STRIDE + l + 1, :])
        ya = _dot(jnp.concatenate(first, axis=1).astype(BF16), w1_ref[idx, 0:half, :])
        yb = _dot(jnp.concatenate(second, axis=1).astype(BF16), w1_ref[idx, half:2 * half, :])
        h = _gelu_tanh(ya + pltpu.roll(yb, nrow - 1, axis=0))
        out = _dot(h.astype(BF16), w2_ref[idx])
        if idx == 0:
            dst[0, 0] = out.astype(BF16)
        else:
            dst[0, 0] = out.astype(BF16).astype(F32).T.astype(BF16)


def nsa_compress(pb, pos, w1, w2, *, bsz, seq):
    g = NSA_KV_GROUPS
    hd = NSA_HEAD_DIM
    nrow = seq // CMP_STRIDE
    oblk = pl.BlockSpec((1, 1, nrow, hd), lambda b, gg: (b, gg, 0, 0))
    return pl.pallas_call(
        _cmp_kernel,
        grid=(bsz, g),
        in_specs=[
            pl.BlockSpec((seq, hd), lambda b, gg: (b, gg)),
            pl.BlockSpec((seq, hd), lambda b, gg: (b, g + gg)),
            pl.BlockSpec(pos.shape, lambda b, gg: (0, 0, 0)),
            pl.BlockSpec(w1.shape, lambda b, gg: (0, 0, 0)),
            pl.BlockSpec(w2.shape, lambda b, gg: (0, 0, 0)),
        ],
        out_specs=[oblk, oblk],
        out_shape=[jax.ShapeDtypeStruct((bsz, g, nrow, hd), BF16)] * 2,
        compiler_params=_params(("parallel", "parallel")),
        name="nsa_compress",
    )(pb, pb, pos, w1, w2)


def _nsa_attn_kernel(q_ref, ks_ref, kw_ref, vs_ref, vw_ref, kc_ref, vct_ref, gate_ref,
                     ovt_ref, hot_ref, o_ref, ksa, vst, kwp, vwt, pbuf, *, tq, tk, seq):
    i = pl.program_id(2)
    hd = NSA_HEAD_DIM
    rep = NSA_REP
    cols = rep * tq
    nblk = seq // tq
    wblk = WINDOW // tq

    @pl.when(i == 0)
    def _():
        ksa[:, 0:hd] = ks_ref[...]
        ksa[:, hd:2 * hd] = hot_ref[...]
        kwp[0:WINDOW, :] = jnp.zeros((WINDOW, hd), BF16)
        kwp[WINDOW:WINDOW + seq, :] = kw_ref[...]
        for blk in range(wblk):
            vwt[blk] = jnp.zeros((hd, tq), BF16)
        for blk in range(nblk):
            rs = slice(blk * tq, (blk + 1) * tq)
            vst[blk] = vs_ref[rs, :].astype(F32).T.astype(BF16)
            vwt[wblk + blk] = vw_ref[rs, :].astype(F32).T.astype(BF16)

    t0 = i * tq
    blk0 = i
    q_t = jnp.concatenate([q_ref[:, r * hd:(r + 1) * hd].astype(F32).T for r in range(rep)],
                          axis=1).astype(BF16)

    def lane_t(shape):
        return t0 + (lax.broadcasted_iota(jnp.int32, shape, 1) & (tq - 1))

    def softmax_cols(s):
        mx = jnp.max(s, axis=0, keepdims=True)
        p = jnp.exp2(s - mx)
        return mx, p, jnp.sum(p, axis=0, keepdims=True)

    ku = lax.broadcasted_iota(jnp.int32, (tq, cols), 0)
    qu = lax.broadcasted_iota(jnp.int32, (tq, cols), 1) & (tq - 1)
    bias_diag = jnp.where(ku <= qu, 0.0, NEG)
    bias_first = jnp.where(ku > qu, 0.0, NEG)

    ncmp = kc_ref.shape[2]
    band = WINDOW + tq
    dstart = pl.multiple_of(t0, tq)
    s = _dot(kc_ref[0, 0], q_t)
    diag_forced = tq <= 2 * SEL_BLOCK
    if diag_forced:
        sd = _dot(ksa[pl.ds(dstart, tq), 0:hd], q_t) + bias_diag
    sw = _dot(kwp[pl.ds(dstart, band), :], q_t)

    n_idx = lax.broadcasted_iota(jnp.int32, (ncmp, cols), 0)
    valid = n_idx * CMP_STRIDE + (CMP_BLOCK - 1) <= lane_t((ncmp, cols))
    sm = jnp.where(valid, s, NEG)
    mx = jnp.max(sm, axis=0, keepdims=True)
    p = jnp.where(valid, jnp.exp2(sm - mx), 0.0)
    den = jnp.sum(p, axis=0, keepdims=True)
    p = p * (1.0 / jnp.where(den > 0.0, den, 1.0))
    o_cmp = _dot(vct_ref[0, 0], p.astype(BF16))

    pieces = []
    for jb in range(wblk + 1):
        piece = sw[jb * tq:(jb + 1) * tq]
        if jb == wblk:
            piece = piece + bias_diag
        else:
            piece = piece + jnp.where(t0 - WINDOW + jb * tq < 0, NEG, 0.0).astype(F32)
            if jb == 0:
                piece = piece + bias_first
        pieces.append(piece)
    _, pw, l_w = softmax_cols(jnp.concatenate(pieces, axis=0))
    vw_t = jnp.concatenate([vwt[blk0 + k] for k in range(wblk + 1)], axis=1)
    o_win = _dot(vw_t, pw.astype(BF16)) * (1.0 / l_w)

    psum = p[:, 0:tq]
    for r in range(1, rep):
        psum = psum + p[:, r * tq:(r + 1) * tq]
    hi, mid, lo = _split3(psum)
    ovt = ovt_ref[...]
    imp = _dot(ovt, hi) + _dot(ovt, mid) + _dot(ovt, lo)
    n_sel = seq // SEL_BLOCK
    b_idx = lax.broadcasted_iota(jnp.int32, (n_sel, tq), 0)
    cur = (t0 + lax.broadcasted_iota(jnp.int32, (n_sel, tq), 1)) // SEL_BLOCK
    forced = (b_idx == 0) | (b_idx == cur) | (b_idx == cur - 1)
    score = jnp.where(b_idx <= cur, imp[0:n_sel] + jnp.where(forced, FORCE, 0.0), NEG)
    rank = jnp.zeros((n_sel, tq), F32)
    for mp in range(n_sel):
        row = score[mp:mp + 1, :]
        beats = (row > score) | ((row == score) & (b_idx > mp))
        rank = rank + jnp.where(beats, 1.0, 0.0)
    selected = rank < float(SEL_TOPK)
    before_tile = b_idx < t0 // SEL_BLOCK

    def augment(keep):
        bias_blk = jnp.where(keep, 0.0, NEG)
        bias_blk = jnp.concatenate([bias_blk, jnp.zeros((hd - n_sel, tq), F32)], axis=0).astype(BF16)
        return jnp.concatenate([q_t, jnp.concatenate([bias_blk] * rep, axis=1)], axis=0)

    q_aug = augment(selected & before_tile)
    if not diag_forced:
        sd = _dot(ksa[pl.ds(dstart, tq), :], augment(selected & jnp.logical_not(before_tile))) + bias_diag

    m_i, pd, l_i = softmax_cols(sd)
    acc = _dot(vst[blk0], pd.astype(BF16))
    per = tk // tq
    pbuf[...] = jnp.zeros_like(pbuf)

    def v_chunk(c):
        return jnp.concatenate([vst[c * per + k] for k in range(per)], axis=1)

    n_chunks = (t0 + tk - 1) // tk
    last = jnp.maximum(n_chunks - 1, 0)

    def sel_body(c, carry):
        m_i, l_i, acc = carry
        off = pl.multiple_of(c * tk, tk)
        sc = _dot(ksa[pl.ds(off, tk), :], q_aug)
        pv = _dot(v_chunk(jnp.maximum(c - 1, 0)), pbuf[...])
        m_new = jnp.maximum(m_i, jnp.max(sc, axis=0, keepdims=True))
        pp = jnp.exp2(sc - m_new)
        pbuf[...] = pp.astype(BF16)
        alpha = jnp.exp2(m_i - m_new)
        l_new = alpha * l_i + jnp.sum(pp, axis=0, keepdims=True)
        return m_new, l_new, alpha * (acc + pv)

    _, l_s, acc_s = lax.fori_loop(0, n_chunks, sel_body, (m_i, l_i, acc))
    acc_s = acc_s + _dot(v_chunk(last), pbuf[...])
    o_sel = acc_s * (1.0 / l_s)

    g_t = gate_ref[...].T
    for r in range(rep):
        cs = slice(r * tq, (r + 1) * tq)
        o_r = (g_t[r:r + 1, :] * o_cmp[:, cs]
               + g_t[rep + r:rep + r + 1, :] * o_sel[:, cs]
               + g_t[2 * rep + r:2 * rep + r + 1, :] * o_win[:, cs])
        o_ref[:, r * hd:(r + 1) * hd] = o_r.T.astype(BF16)


def nsa_attention(pa, kcmp, vcmp_t, gates, overlap_t, onehot, *, bsz, seq, tq=512, tk=512):
    m = pa.shape[0]
    hd = NSA_HEAD_DIM
    g = NSA_KV_GROUPS
    nq = seq // tq
    gw = NSA_REP * hd
    c_ks = NSA_Q // hd
    c_kw = (NSA_Q + NSA_KV) // hd
    c_vs = (NSA_Q + 2 * NSA_KV) // hd
    c_vw = (NSA_Q + 3 * NSA_KV) // hd
    ncmp = kcmp.shape[2]

    def kv_spec(c0):
        return pl.BlockSpec((seq, hd), lambda b, gg, i: (b, c0 + gg))

    kern = functools.partial(_nsa_attn_kernel, tq=tq, tk=tk, seq=seq)
    return pl.pallas_call(
        kern,
        grid=(bsz, g, nq),
        in_specs=[
            pl.BlockSpec((tq, gw), lambda b, gg, i: (b * nq + i, gg)),
            kv_spec(c_ks), kv_spec(c_kw), kv_spec(c_vs), kv_spec(c_vw),
            pl.BlockSpec((1, 1, ncmp, hd), lambda b, gg, i: (b, gg, 0, 0)),
            pl.BlockSpec((1, 1, hd, ncmp), lambda b, gg, i: (b, gg, 0, 0)),
            pl.BlockSpec((tq, LANES), lambda b, gg, i: (b * nq + i, gg)),
            pl.BlockSpec(overlap_t.shape, lambda b, gg, i: (0, 0)),
            pl.BlockSpec(onehot.shape, lambda b, gg, i: (0, 0)),
        ],
        out_specs=pl.BlockSpec((tq, gw), lambda b, gg, i: (b * nq + i, gg)),
        out_shape=jax.ShapeDtypeStruct((m, NSA_Q), BF16),
        scratch_shapes=[
            pltpu.VMEM((seq, 2 * hd), BF16),
            pltpu.VMEM((seq // tq, hd, tq), BF16),
            pltpu.VMEM((seq + WINDOW, hd), BF16),
            pltpu.VMEM((seq // tq + WINDOW // tq, hd, tq), BF16),
            pltpu.VMEM((tk, NSA_REP * tq), BF16),
        ],
        compiler_params=_params(("parallel", "parallel", "arbitrary")),
        name="nsa_attention",
    )(pa, pa, pa, pa, pa, kcmp, vcmp_t, gates, overlap_t, onehot)


def _nsa_constants(seq):
    mm = np.arange(LANES)[:, None]
    n = np.arange(LANES)[None, :]
    c_start = n * CMP_STRIDE
    s_start = mm * SEL_BLOCK
    n_cmp = (seq - CMP_BLOCK) // CMP_STRIDE + 1
    overlap_t = ((c_start < s_start + SEL_BLOCK) & (c_start + CMP_BLOCK > s_start)
                 & (n < n_cmp) & (mm < seq // SEL_BLOCK))
    onehot = (np.arange(seq)[:, None] // SEL_BLOCK) == np.arange(LANES)[None, :]
    return (jnp.asarray(overlap_t.astype(np.float32), dtype=BF16),
            jnp.asarray(onehot.astype(np.float32), dtype=BF16))


def nsa_mixer_ln(x, cos, sin, w_in, gate_b, cmp_pos, cmp_w1, cmp_w2, w_out, g, b, *, bsz, seq):
    hd = NSA_HEAD_DIM
    grp = NSA_KV_GROUPS
    rep = NSA_REP
    kv = NSA_KV

    def wcols(order):
        return jnp.concatenate([w_in[:, NSA_Q + i * kv:NSA_Q + (i + 1) * kv] for i in order], axis=1)

    w_a = jnp.concatenate([w_in[:, :NSA_Q], wcols((2, 4, 3, 5))], axis=1).astype(BF16)
    pa = proj(x, w_a, cos, sin, n_rope=(NSA_Q + 2 * kv) // PROJ_WIDE, n_scale=NSA_Q // PROJ_WIDE,
              scale=hd ** -0.5 * LOG2E, out_dtype=BF16, tn=PROJ_WIDE)
    pb = proj(x, wcols((0, 1)).astype(BF16), cos, sin, n_rope=1, tn=kv)

    w_gl = w_in[:, NSA_Q + 6 * kv:].reshape(-1, 3, grp, rep).transpose(0, 2, 1, 3).reshape(-1, grp, 3 * rep)
    w_gl = jnp.pad(w_gl, ((0, 0), (0, 0), (0, LANES - 3 * rep))).reshape(-1, grp * LANES).astype(BF16)
    b_gl = gate_b.reshape(3, grp, rep).transpose(1, 0, 2).reshape(grp, 3 * rep)
    b_gl = jnp.pad(b_gl, ((0, 0), (0, LANES - 3 * rep))).reshape(1, grp * LANES)
    gates = gate_proj(x, w_gl, b_gl)

    kcmp, vcmp_t = nsa_compress(pb, cmp_pos, cmp_w1.astype(BF16), cmp_w2.astype(BF16), bsz=bsz, seq=seq)
    overlap_t, onehot = _nsa_constants(seq)
    o = nsa_attention(pa, kcmp, vcmp_t, gates, overlap_t, onehot, bsz=bsz, seq=seq)
    return out_ln(o, w_out.astype(BF16), x, g, b)


def _conv_kernel(x_ref, wb_ref, wc_ref, wh_ref, cw_ref, wo_ref, g_ref, b_ref, o_ref,
                 xb_ref, acc_ref, tail_ref, *, tiles_per_seq):
    i = pl.program_id(0)
    j = pl.program_id(1)

    @pl.when(j == 0)
    def _():
        xb_ref[...] = x_ref[...].astype(BF16)
        acc_ref[...] = jnp.zeros_like(acc_ref)

    xb = xb_ref[...]
    bg = _dot(xb, wb_ref[...])
    u = _dot(xb, wc_ref[...]) * _dot(xb, wh_ref[...])
    tm = u.shape[0]
    first = (i % tiles_per_seq) == 0
    prev = jnp.where(first, 0.0, tail_ref[j])
    tail_ref[j] = u[tm - 8:tm, :]
    rid = lax.broadcasted_iota(jnp.int32, u.shape, 0)
    u1 = jnp.where(rid == 0, prev[7:8, :], pltpu.roll(u, 1, axis=0))
    u2 = pltpu.roll(u, 2, axis=0)
    u2 = jnp.where(rid == 0, prev[6:7, :], jnp.where(rid == 1, prev[7:8, :], u2))
    cw = cw_ref[...]
    y = cw[0:1, :] * u2 + cw[1:2, :] * u1 + cw[2:3, :] * u
    acc_ref[...] += _dot((bg * y).astype(BF16), wo_ref[...])

    @pl.when(j == pl.num_programs(1) - 1)
    def _():
        yy = DEEPNORM_ALPHA * x_ref[...] + acc_ref[...]
        o_ref[...] = _layer_norm(yy, g_ref[...], b_ref[...])


def conv_mixer_ln(x, w_in, conv_w, w_out, g, b, *, seq, tm=512, tn=512):
    m, d = x.shape
    nj = d // tn
    kern = functools.partial(_conv_kernel, tiles_per_seq=seq // tm)
    return pl.pallas_call(
        kern,
        grid=(m // tm, nj),
        in_specs=[
            pl.BlockSpec((tm, d), lambda i, j: (i, 0)),
            pl.BlockSpec((d, tn), lambda i, j: (0, j)),
            pl.BlockSpec((d, tn), lambda i, j: (0, j + nj)),
            pl.BlockSpec((d, tn), lambda i, j: (0, j + 2 * nj)),
            pl.BlockSpec((CONV_WIDTH, tn), lambda i, j: (0, j)),
            pl.BlockSpec((tn, d), lambda i, j: (j, 0)),
            pl.BlockSpec((1, d), lambda i, j: (0, 0)),
            pl.BlockSpec((1, d), lambda i, j: (0, 0)),
        ],
        out_specs=pl.BlockSpec((tm, d), lambda i, j: (i, 0)),
        out_shape=jax.ShapeDtypeStruct((m, d), F32),
        scratch_shapes=[pltpu.VMEM((tm, d), BF16), pltpu.VMEM((tm, d), F32),
                        pltpu.VMEM((nj, 8, tn), F32)],
        compiler_params=_params(("arbitrary", "arbitrary")),
        name="conv_mixer_ln",
    )(x, w_in, w_in, w_in, conv_w, w_out, g, b)


def _log_sigmoid(z):
    return jnp.minimum(z, 0.0) - jnp.log1p(jnp.exp(-jnp.abs(z)))


def _gla_kernel(q_ref, k_ref, v_ref, r_ref, a_ref, wa_ref, ba_ref, ng_ref, o_ref, st_ref, upd_ref):
    i = pl.program_id(2)
    c_sz = GLA_CHUNK
    dk, dv = GLA_HEAD_K, GLA_HEAD_V
    n_c = q_ref.shape[0] // c_sz
    heads = range(GLA_HEADS_PER_STEP)

    @pl.when(i == 0)
    def _():
        st_ref[...] = jnp.zeros_like(st_ref)

    rr = lax.broadcasted_iota(jnp.int32, (c_sz, c_sz), 0)
    cc = lax.broadcasted_iota(jnp.int32, (c_sz, c_sz), 1)
    causal = cc <= rr
    tril = jnp.where(causal, 1.0, 0.0).astype(BF16)
    chunks = [slice(c * c_sz, (c + 1) * c_sz) for c in range(n_c)]

    z = _dot(a_ref[...].astype(BF16), wa_ref[...]) + ba_ref[...]
    gk = _log_sigmoid(z) / GLA_GATE_NORM
    hi, mid, lo = _split3(gk)
    bcum = jnp.concatenate(
        [_dot(tril, hi[rs]) + _dot(tril, mid[rs]) + _dot(tril, lo[rs]) for rs in chunks], axis=0)
    b_last = [bcum[rs][c_sz - 1:c_sz, :] for rs in chunks]
    b_last_rows = jnp.concatenate([jnp.broadcast_to(bl, (c_sz, bl.shape[1])) for bl in b_last], axis=0)
    qd = (q_ref[...] * (dk ** -0.5) * jnp.exp(bcum)).astype(BF16)
    k = k_ref[...]
    kd = (k * jnp.exp(-bcum)).astype(BF16)
    kl = (k * jnp.exp(b_last_rows - bcum)).astype(BF16)
    v = v_ref[...].astype(BF16)

    o_intra = []
    for hh in heads:
        ks, vs = slice(hh * dk, (hh + 1) * dk), slice(hh * dv, (hh + 1) * dv)
        outs = []
        for c, rs in enumerate(chunks):
            att = jnp.where(causal, _dot_nt(qd[rs, ks], kd[rs, ks]), 0.0).astype(BF16)
            outs.append(_dot(att, v[rs, vs]))
            upd_ref[hh, c] = _dot_tn(v[rs, vs], kl[rs, ks])
        o_intra.append(outs)

    st = [st_ref[hh] for hh in heads]
    ng = ng_ref[...]
    for c, rs in enumerate(chunks):
        for hh in heads:
            ks, vs = slice(hh * dk, (hh + 1) * dk), slice(hh * dv, (hh + 1) * dv)
            o = o_intra[hh][c] + _dot_nt(qd[rs, ks], st[hh].astype(BF16))
            st[hh] = st[hh] * jnp.exp(b_last[c][:, ks]) + upd_ref[hh, c]
            o = o * lax.rsqrt(jnp.mean(o * o, axis=-1, keepdims=True) + LN_EPS) * ng
            o_ref[rs, vs] = (o * _silu(r_ref[rs, vs])).astype(BF16)
    for hh in heads:
        st_ref[hh] = st[hh]


def gla_scan(p, a, wa, ba, ng, *, bsz, seq, tt=512):
    m = p.shape[0]
    dk, dv = GLA_HEAD_K, GLA_HEAD_V
    hp = GLA_HEADS_PER_STEP
    nt = seq // tt
    c_k = GLA_KEY_DIM // (hp * dk)
    c_v = 2 * GLA_KEY_DIM // (hp * dv)
    c_r = (2 * GLA_KEY_DIM + GLA_VAL_DIM) // (hp * dv)
    row = lambda b, h, i: b * nt + i
    return pl.pallas_call(
        _gla_kernel,
        grid=(bsz, GLA_HEADS // hp, nt),
        in_specs=[
            pl.BlockSpec((tt, hp * dk), lambda b, h, i: (row(b, h, i), h)),
            pl.BlockSpec((tt, hp * dk), lambda b, h, i: (row(b, h, i), c_k + h)),
            pl.BlockSpec((tt, hp * dv), lambda b, h, i: (row(b, h, i), c_v + h)),
            pl.BlockSpec((tt, hp * dv), lambda b, h, i: (row(b, h, i), c_r + h)),
            pl.BlockSpec((tt, LANES), lambda b, h, i: (row(b, h, i), 0)),
            pl.BlockSpec((LANES, hp * dk), lambda b, h, i: (0, h)),
            pl.BlockSpec((1, hp * dk), lambda b, h, i: (0, h)),
            pl.BlockSpec((1, dv), lambda b, h, i: (0, 0)),
        ],
        out_specs=pl.BlockSpec((tt, hp * dv), lambda b, h, i: (row(b, h, i), h)),
        out_shape=jax.ShapeDtypeStruct((m, GLA_VAL_DIM), BF16),
        scratch_shapes=[pltpu.VMEM((hp, dv, dk), F32),
                        pltpu.VMEM((hp, tt // GLA_CHUNK, dv, dk), F32)],
        compiler_params=_params(("parallel", "parallel", "arbitrary")),
        name="gla_scan",
    )(p, p, p, p, a, wa, ba, ng)


def gla_mixer_ln(x, w_in, w_a2, b_a, norm_g, w_out, g, b, *, bsz, seq):
    n_main = 2 * GLA_KEY_DIM + 2 * GLA_VAL_DIM
    dummy = jnp.zeros((x.shape[0], LANES), F32)
    p = proj(x, w_in[:, :n_main].astype(BF16), dummy, dummy, tn=PROJ_WIDE)
    w_a = jnp.pad(w_in[:, n_main:], ((0, 0), (0, LANES - GLA_GATE_RANK))).astype(BF16)
    a = proj(x, w_a, dummy, dummy, tn=LANES)
    wa2 = jnp.pad(w_a2, ((0, LANES - GLA_GATE_RANK), (0, 0))).astype(BF16)
    o = gla_scan(p, a, wa2, b_a[None, :], norm_g[None, :], bsz=bsz, seq=seq)
    return out_ln(o, w_out.astype(BF16), x, g, b)


def kernel(x, positions, ln_g, ln_b, ffn_w_in, ffn_w_out, nsa_w_in, nsa_gate_b, nsa_cmp_pos,
           nsa_cmp_w1, nsa_cmp_w2, nsa_w_out, conv_w_in, conv_w, conv_w_out, gla_w_in, gla_w_a2,
           gla_b_a, gla_norm_g, gla_w_out):
    bsz, seq, d = x.shape
    m = bsz * seq
    h = x.reshape(m, d)
    cos, sin = rope_tables(positions.reshape(m, 1).astype(F32))
    ffn_in = ffn_w_in.astype(BF16)
    ffn_out = ffn_w_out.astype(BF16)
    for i in range(DEPTH):
        lg = lambda k: ln_g[i, k][None, :]
        lb = lambda k: ln_b[i, k][None, :]
        h = ffn_ln(h, ffn_in, ffn_out, lg(0), lb(0), i, 0)
        kind, j = i % N_MIXERS, i // N_MIXERS
        if kind == 0:
            h = nsa_mixer_ln(h, cos, sin, nsa_w_in[j], nsa_gate_b[j], nsa_cmp_pos[j], nsa_cmp_w1[j],
                             nsa_cmp_w2[j], nsa_w_out[j], lg(1), lb(1), bsz=bsz, seq=seq)
        elif kind == 1:
            h = conv_mixer_ln(h, conv_w_in[j].astype(BF16), conv_w[j], conv_w_out[j].astype(BF16),
                              lg(1), lb(1), seq=seq)
        else:
            h = gla_mixer_ln(h, gla_w_in[j], gla_w_a2[j], gla_b_a[j], gla_norm_g[j], gla_w_out[j],
                             lg(1), lb(1), bsz=bsz, seq=seq)
        h = ffn_ln(h, ffn_in, ffn_out, lg(2), lb(2), i, 1)
    return h.reshape(bsz, seq, d)
```

```python
import functools

import numpy as np
import jax
import jax.numpy as jnp
from jax import lax
from jax.experimental import pallas as pl
from jax.experimental.pallas import tpu as pltpu

F32 = jnp.float32
BF16 = jnp.bfloat16

D_MODEL = 2048
DEPTH = 4
N_MIXERS = 3
DEEPNORM_ALPHA = (2.0 * DEPTH) ** 0.25
LN_EPS = 1e-5
MACARON_WEIGHT = 0.5
D_FF = 5632

NSA_HEADS = 16
NSA_KV_GROUPS = 4
NSA_REP = NSA_HEADS // NSA_KV_GROUPS
NSA_HEAD_DIM = D_MODEL // NSA_HEADS
NSA_Q = NSA_HEADS * NSA_HEAD_DIM
NSA_KV = NSA_KV_GROUPS * NSA_HEAD_DIM
CMP_BLOCK = 32
CMP_STRIDE = 16
SEL_BLOCK = 64
SEL_TOPK = 16
WINDOW = 512
ROPE_THETA = 10000.0
NEG = -1e30
FORCE = 1e3
LOG2E = float(np.log2(np.e))

CONV_WIDTH = 3

GLA_HEADS = 4
GLA_KEY_DIM = D_MODEL // 2
GLA_VAL_DIM = D_MODEL
GLA_HEAD_K = GLA_KEY_DIM // GLA_HEADS
GLA_HEAD_V = GLA_VAL_DIM // GLA_HEADS
GLA_GATE_RANK = 16
GLA_GATE_NORM = 16.0
GLA_CHUNK = 64
GLA_HEADS_PER_STEP = 2

LANES = 128
PROJ_SLABS = 4
OUT_SLABS = 2
PROJ_WIDE = 1024
VMEM_LIMIT = 56 * 1024 * 1024


def _params(sem):
    return pltpu.CompilerParams(dimension_semantics=sem, vmem_limit_bytes=VMEM_LIMIT)


def _layer_norm(y, g, b):
    mu = jnp.mean(y, axis=-1, keepdims=True)
    d = y - mu
    var = jnp.mean(d * d, axis=-1, keepdims=True)
    return d * lax.rsqrt(var + LN_EPS) * g + b


def _silu(h):
    return h * jax.nn.sigmoid(h)


def _dot(a, b):
    return jnp.dot(a, b, preferred_element_type=F32)


def _dot_nt(a, b):
    return lax.dot_general(a, b, (((1,), (1,)), ((), ())), preferred_element_type=F32)


def _dot_tn(a, b):
    return lax.dot_general(a, b, (((0,), (0,)), ((), ())), preferred_element_type=F32)


def _split3(x):
    hi = x.astype(BF16)
    r1 = x - hi.astype(F32)
    mid = r1.astype(BF16)
    lo = (r1 - mid.astype(F32)).astype(BF16)
    return hi, mid, lo


def _ffn_kernel(x_ref, wga_ref, wua_ref, woa_ref, wgb_ref, wub_ref, wob_ref, g_ref, b_ref, o_ref,
                xb_ref, acc_ref, *, n_ff_tiles):
    j = pl.program_id(1)

    @pl.when(j == 0)
    def _():
        xb_ref[...] = x_ref[...].astype(BF16)
        acc_ref[...] = jnp.zeros_like(acc_ref)

    def ff_tile(wg_ref, wu_ref, wo_ref):
        xb = xb_ref[...]
        h = _dot(xb, wg_ref[...])
        u = _dot(xb, wu_ref[...])
        return _dot((_silu(h) * u).astype(BF16), wo_ref[...])

    paired = 2 * j + 1 < n_ff_tiles

    @pl.when(paired)
    def _():
        acc_ref[...] += ff_tile(wga_ref, wua_ref, woa_ref) + ff_tile(wgb_ref, wub_ref, wob_ref)

    @pl.when(jnp.logical_not(paired))
    def _():
        acc_ref[...] += ff_tile(wga_ref, wua_ref, woa_ref)

    @pl.when(j == pl.num_programs(1) - 1)
    def _():
        y = DEEPNORM_ALPHA * x_ref[...] + MACARON_WEIGHT * acc_ref[...]
        o_ref[...] = _layer_norm(y, g_ref[...], b_ref[...])


def ffn_ln(x, w_in, w_out, g, b, layer, half, *, tm=512, tf=512):
    m, d = x.shape
    ff = w_out.shape[2]
    nt = ff // tf
    nj = (nt + 1) // 2

    def tile_a(j):
        return 2 * j

    def tile_b(j):
        return jnp.minimum(2 * j + 1, nt - 1)

    def w_specs(tile):
        return [
            pl.BlockSpec((None, None, d, tf), lambda i, j: (layer, half, 0, tile(j))),
            pl.BlockSpec((None, None, d, tf), lambda i, j: (layer, half, 0, tile(j) + nt)),
            pl.BlockSpec((None, None, tf, d), lambda i, j: (layer, half, tile(j), 0)),
        ]

    return pl.pallas_call(
        functools.partial(_ffn_kernel, n_ff_tiles=nt),
        grid=(m // tm, nj),
        in_specs=[pl.BlockSpec((tm, d), lambda i, j: (i, 0))] + w_specs(tile_a) + w_specs(tile_b) + [
            pl.BlockSpec((1, d), lambda i, j: (0, 0)),
            pl.BlockSpec((1, d), lambda i, j: (0, 0)),
        ],
        out_specs=pl.BlockSpec((tm, d), lambda i, j: (i, 0)),
        out_shape=jax.ShapeDtypeStruct((m, d), F32),
        scratch_shapes=[pltpu.VMEM((tm, d), BF16), pltpu.VMEM((tm, d), F32)],
        compiler_params=_params(("parallel", "arbitrary")),
        name="ffn_ln",
    )(x, w_in, w_in, w_out, w_in, w_in, w_out, g, b)


def _proj_kernel(x_ref, w_ref, cos_ref, sin_ref, o_ref, xb_ref, *, n_rope, n_scale, scale):
    j = pl.program_id(1)

    @pl.when(j == 0)
    def _():
        xb_ref[...] = x_ref[...].astype(BF16)

    tm, tn = o_ref.shape
    slabs = [slice(r * tm // PROJ_SLABS, (r + 1) * tm // PROJ_SLABS) for r in range(PROJ_SLABS)]
    ys = [_dot(xb_ref[rs, :], w_ref[...]) for rs in slabs]

    if n_rope > 0:
        rotary = j < n_rope
        sc = jnp.where(j < n_scale, scale, 1.0).astype(F32)
        for rs, y in zip(slabs, ys):
            cos = jnp.where(rotary, cos_ref[rs, :], 1.0)
            sin = jnp.where(rotary, sin_ref[rs, :], 0.0)
            for hh in range(tn // LANES):
                t = y[:, hh * LANES:(hh + 1) * LANES]
                rot = pltpu.roll(t, LANES // 2, axis=1)
                o_ref[rs, hh * LANES:(hh + 1) * LANES] = ((t * cos + rot * sin) * sc).astype(o_ref.dtype)
    else:
        for rs, y in zip(slabs, ys):
            o_ref[rs, :] = y.astype(o_ref.dtype)


def proj(x, w, cos, sin, *, n_rope=0, n_scale=0, scale=1.0, out_dtype=F32, tm=1024, tn=512):
    m, d = x.shape
    n = w.shape[1]
    kern = functools.partial(_proj_kernel, n_rope=n_rope, n_scale=n_scale, scale=scale)
    return pl.pallas_call(
        kern,
        grid=(m // tm, n // tn),
        in_specs=[
            pl.BlockSpec((tm, d), lambda i, j: (i, 0)),
            pl.BlockSpec((d, tn), lambda i, j: (0, j)),
            pl.BlockSpec((tm, LANES), lambda i, j: (i, 0)),
            pl.BlockSpec((tm, LANES), lambda i, j: (i, 0)),
        ],
        out_specs=pl.BlockSpec((tm, tn), lambda i, j: (i, j)),
        out_shape=jax.ShapeDtypeStruct((m, n), out_dtype),
        scratch_shapes=[pltpu.VMEM((tm, d), BF16)],
        compiler_params=_params(("parallel", "arbitrary")),
        name="proj",
    )(x, w, cos, sin)


def _rope_table_kernel(pos_ref, inv_ref, sign_ref, cos_ref, sin_ref):
    ang = pos_ref[...] * inv_ref[...]
    cos_ref[...] = jnp.cos(ang)
    sin_ref[...] = jnp.sin(ang) * sign_ref[...]


def rope_tables(pos_f32, *, tm=2048):
    m = pos_f32.shape[0]
    hd = NSA_HEAD_DIM
    inv = ROPE_THETA ** (-jnp.arange(0, hd, 2, dtype=F32) / hd)
    inv_full = jnp.concatenate([inv, inv])[None, :]
    sign = jnp.concatenate([-jnp.ones((hd // 2,), F32), jnp.ones((hd // 2,), F32)])[None, :]
    return pl.pallas_call(
        _rope_table_kernel,
        grid=(m // tm,),
        in_specs=[
            pl.BlockSpec((tm, 1), lambda i: (i, 0)),
            pl.BlockSpec((1, hd), lambda i: (0, 0)),
            pl.BlockSpec((1, hd), lambda i: (0, 0)),
        ],
        out_specs=[pl.BlockSpec((tm, hd), lambda i: (i, 0))] * 2,
        out_shape=[jax.ShapeDtypeStruct((m, hd), F32)] * 2,
        compiler_params=_params(("parallel",)),
        name="rope_tables",
    )(pos_f32, inv_full, sign)


def _gate_kernel(x_ref, w_ref, b_ref, o_ref):
    z = _dot(x_ref[...].astype(BF16), w_ref[...]) + b_ref[...]
    o_ref[...] = jax.nn.sigmoid(z)


def gate_proj(x, w, b, *, tm=1024):
    m, d = x.shape
    n = w.shape[1]
    return pl.pallas_call(
        _gate_kernel,
        grid=(m // tm,),
        in_specs=[
            pl.BlockSpec((tm, d), lambda i: (i, 0)),
            pl.BlockSpec((d, n), lambda i: (0, 0)),
            pl.BlockSpec((1, n), lambda i: (0, 0)),
        ],
        out_specs=pl.BlockSpec((tm, n), lambda i: (i, 0)),
        out_shape=jax.ShapeDtypeStruct((m, n), F32),
        compiler_params=_params(("parallel",)),
        name="gate_proj",
    )(x, w, b)


def _out_ln_kernel(a_ref, w_ref, x_ref, g_ref, b_ref, o_ref):
    tm = a_ref.shape[0]
    slabs = [slice(r * tm // OUT_SLABS, (r + 1) * tm // OUT_SLABS) for r in range(OUT_SLABS)]
    ys = [_dot(a_ref[rs, :], w_ref[...]) for rs in slabs]
    for rs, y in zip(slabs, ys):
        o_ref[rs, :] = _layer_norm(DEEPNORM_ALPHA * x_ref[rs, :] + y, g_ref[...], b_ref[...])


def out_ln(a, w, x, g, b, *, tm=512):
    m, d = x.shape
    k = a.shape[1]
    return pl.pallas_call(
        _out_ln_kernel,
        grid=(m // tm,),
        in_specs=[
            pl.BlockSpec((tm, k), lambda i: (i, 0)),
            pl.BlockSpec((k, d), lambda i: (0, 0)),
            pl.BlockSpec((tm, d), lambda i: (i, 0)),
            pl.BlockSpec((1, d), lambda i: (0, 0)),
            pl.BlockSpec((1, d), lambda i: (0, 0)),
        ],
        out_specs=pl.BlockSpec((tm, d), lambda i: (i, 0)),
        out_shape=jax.ShapeDtypeStruct((m, d), F32),
        compiler_params=_params(("parallel",)),
        name="out_ln",
    )(a, w, x, g, b)


def _gelu_tanh(x):
    c = float(np.sqrt(2.0 / np.pi))
    return 0.5 * x * (1.0 + jnp.tanh(c * (x + 0.044715 * (x * x * x))))


def _cmp_kernel(kc_ref, vc_ref, pos_ref, w1_ref, w2_ref, ko_ref, vo_ref):
    seq, hd = kc_ref.shape
    nrow = seq // CMP_STRIDE
    half = CMP_STRIDE * hd
    for idx, (src, dst) in enumerate(((kc_ref, ko_ref), (vc_ref, vo_ref))):
        first, second = [], []
        for l in range(CMP_STRIDE):
            xl = src[pl.ds(l, nrow, stride=CMP_STRIDE), :]
            first.append(xl + pos_ref[idx, l:l + 1, :])
            second.append(xl + pos_ref[idx, CMP_STRIDE + l:CMP_STRIDE + l + 1, :])
        ya = _dot(jnp.concatenate(first, axis=1).astype(BF16), w1_ref[idx, 0:half, :])
        yb = _dot(jnp.concatenate(second, axis=1).astype(BF16), w1_ref[idx, half:2 * half, :])
        h = _gelu_tanh(ya + pltpu.roll(yb, nrow - 1, axis=0))
        out = _dot(h.astype(BF16), w2_ref[idx])
        if idx == 0:
            dst[0, 0] = out.astype(BF16)
        else:
            dst[0, 0] = out.astype(BF16).astype(F32).T.astype(BF16)


def nsa_compress(pb, pos, w1, w2, *, bsz, seq):
    g = NSA_KV_GROUPS
    hd = NSA_HEAD_DIM
    nrow = seq // CMP_STRIDE
    oblk = pl.BlockSpec((1, 1, nrow, hd), lambda b, gg: (b, gg, 0, 0))
    return pl.pallas_call(
        _cmp_kernel,
        grid=(bsz, g),
        in_specs=[
            pl.BlockSpec((seq, hd), lambda b, gg: (b, gg)),
            pl.BlockSpec((seq, hd), lambda b, gg: (b, g + gg)),
            pl.BlockSpec(pos.shape, lambda b, gg: (0, 0, 0)),
            pl.BlockSpec(w1.shape, lambda b, gg: (0, 0, 0)),
            pl.BlockSpec(w2.shape, lambda b, gg: (0, 0, 0)),
        ],
        out_specs=[oblk, oblk],
        out_shape=[jax.ShapeDtypeStruct((bsz, g, nrow, hd), BF16)] * 2,
        compiler_params=_params(("parallel", "parallel")),
        name="nsa_compress",
    )(pb, pb, pos, w1, w2)


def _nsa_attn_kernel(q_ref, ks_ref, kw_ref, vs_ref, vw_ref, kc_ref, vct_ref, gate_ref,
                     ovt_ref, hot_ref, o_ref, ksa, vst, kwp, vwt, pbuf, *, tq, tk, seq):
    i = pl.program_id(2)
    hd = NSA_HEAD_DIM
    rep = NSA_REP
    cols = rep * tq
    nblk = seq // tq
    wblk = WINDOW // tq

    @pl.when(i == 0)
    def _():
        ksa[:, 0:hd] = ks_ref[...]
        ksa[:, hd:2 * hd] = hot_ref[...]
        kwp[0:WINDOW, :] = jnp.zeros((WINDOW, hd), BF16)
        kwp[WINDOW:WINDOW + seq, :] = kw_ref[...]
        for blk in range(wblk):
            vwt[blk] = jnp.zeros((hd, tq), BF16)
        for blk in range(nblk):
            rs = slice(blk * tq, (blk + 1) * tq)
            vst[blk] = vs_ref[rs, :].astype(F32).T.astype(BF16)
            vwt[wblk + blk] = vw_ref[rs, :].astype(F32).T.astype(BF16)

    t0 = i * tq
    blk0 = i
    q_t = jnp.concatenate([q_ref[:, r * hd:(r + 1) * hd].astype(F32).T for r in range(rep)],
                          axis=1).astype(BF16)

    def lane_t(shape):
        return t0 + (lax.broadcasted_iota(jnp.int32, shape, 1) & (tq - 1))

    def softmax_cols(s):
        mx = jnp.max(s, axis=0, keepdims=True)
        p = jnp.exp2(s - mx)
        return mx, p, jnp.sum(p, axis=0, keepdims=True)

    ku = lax.broadcasted_iota(jnp.int32, (tq, cols), 0)
    qu = lax.broadcasted_iota(jnp.int32, (tq, cols), 1) & (tq - 1)
    bias_diag = jnp.where(ku <= qu, 0.0, NEG)
    bias_first = jnp.where(ku > qu, 0.0, NEG)

    ncmp = kc_ref.shape[2]
    band = WINDOW + tq
    dstart = pl.multiple_of(t0, tq)
    s = _dot(kc_ref[0, 0], q_t)
    diag_forced = tq <= 2 * SEL_BLOCK
    if diag_forced:
        sd = _dot(ksa[pl.ds(dstart, tq), 0:hd], q_t) + bias_diag
    sw = _dot(kwp[pl.ds(dstart, band), :], q_t)

    n_idx = lax.broadcasted_iota(jnp.int32, (ncmp, cols), 0)
    valid = n_idx * CMP_STRIDE + (CMP_BLOCK - 1) <= lane_t((ncmp, cols))
    sm = jnp.where(valid, s, NEG)
    mx = jnp.max(sm, axis=0, keepdims=True)
    p = jnp.where(valid, jnp.exp2(sm - mx), 0.0)
    den = jnp.sum(p, axis=0, keepdims=True)
    p = p * (1.0 / jnp.where(den > 0.0, den, 1.0))
    o_cmp = _dot(vct_ref[0, 0], p.astype(BF16))

    pieces = []
    for jb in range(wblk + 1):
        piece = sw[jb * tq:(jb + 1) * tq]
        if jb == wblk:
            piece = piece + bias_diag
        else:
            piece = piece + jnp.where(t0 - WINDOW + jb * tq < 0, NEG, 0.0).astype(F32)
            if jb == 0:
                piece = piece + bias_first
        pieces.append(piece)
    _, pw, l_w = softmax_cols(jnp.concatenate(pieces, axis=0))
    vw_t = jnp.concatenate([vwt[blk0 + k] for k in range(wblk + 1)], axis=1)
    o_win = _dot(vw_t, pw.astype(BF16)) * (1.0 / l_w)

    psum = p[:, 0:tq]
    for r in range(1, rep):
        psum = psum + p[:, r * tq:(r + 1) * tq]
    hi, mid, lo = _split3(psum)
    ovt = ovt_ref[...]
    imp = _dot(ovt, hi) + _dot(ovt, mid) + _dot(ovt, lo)
    n_sel = seq // SEL_BLOCK
    b_idx = lax.broadcasted_iota(jnp.int32, (n_sel, tq), 0)
    cur = (t0 + lax.broadcasted_iota(jnp.int32, (n_sel, tq), 1)) // SEL_BLOCK
    forced = (b_idx == 0) | (b_idx == cur) | (b_idx == cur - 1)
    score = jnp.where(b_idx <= cur, imp[0:n_sel] + jnp.where(forced, FORCE, 0.0), NEG)
    rank = jnp.zeros((n_sel, tq), F32)
    for mp in range(n_sel):
        row = score[mp:mp + 1, :]
        beats = (row > score) | ((row == score) & (b_idx > mp))
        rank = rank + jnp.where(beats, 1.0, 0.0)
    selected = rank < float(SEL_TOPK)
    before_tile = b_idx < t0 // SEL_BLOCK

    def augment(keep):
        bias_blk = jnp.where(keep, 0.0, NEG)
        bias_blk = jnp.concatenate([bias_blk, jnp.zeros((hd - n_sel, tq), F32)], axis=0).astype(BF16)
        return jnp.concatenate([q_t, jnp.concatenate([bias_blk] * rep, axis=1)], axis=0)

    q_aug = augment(selected & before_tile)
    if not diag_forced:
        sd = _dot(ksa[pl.ds(dstart, tq), :], augment(selected & jnp.logical_not(before_tile))) + bias_diag

    m_i, pd, l_i = softmax_cols(sd)
    acc = _dot(vst[blk0], pd.astype(BF16))
    per = tk // tq
    pbuf[...] = jnp.zeros_like(pbuf)

    def v_chunk(c):
        return jnp.concatenate([vst[c * per + k] for k in range(per)], axis=1)

    n_chunks = (t0 + tk - 1) // tk
    last = jnp.maximum(n_chunks - 1, 0)

    def sel_body(c, carry):
        m_i, l_i, acc = carry
        off = pl.multiple_of(c * tk, tk)
        sc = _dot(ksa[pl.ds(off, tk), :], q_aug)
        pv = _dot(v_chunk(jnp.maximum(c - 1, 0)), pbuf[...])
        m_new = jnp.maximum(m_i, jnp.max(sc, axis=0, keepdims=True))
        pp = jnp.exp2(sc - m_new)
        pbuf[...] = pp.astype(BF16)
        alpha = jnp.exp2(m_i - m_new)
        l_new = alpha * l_i + jnp.sum(pp, axis=0, keepdims=True)
        return m_new, l_new, alpha * (acc + pv)

    _, l_s, acc_s = lax.fori_loop(0, n_chunks, sel_body, (m_i, l_i, acc))
    acc_s = acc_s + _dot(v_chunk(last), pbuf[...])
    o_sel = acc_s * (1.0 / l_s)

    g_t = gate_ref[...].T
    for r in range(rep):
        cs = slice(r * tq, (r + 1) * tq)
        o_r = (g_t[r:r + 1, :] * o_cmp[:, cs]
               + g_t[rep + r:rep + r + 1, :] * o_sel[:, cs]
               + g_t[2 * rep + r:2 * rep + r + 1, :] * o_win[:, cs])
        o_ref[:, r * hd:(r + 1) * hd] = o_r.T.astype(BF16)


def nsa_attention(pa, kcmp, vcmp_t, gates, overlap_t, onehot, *, bsz, seq, tq=512, tk=512):
    m = pa.shape[0]
    hd = NSA_HEAD_DIM
    g = NSA_KV_GROUPS
    nq = seq // tq
    gw = NSA_REP * hd
    c_ks = NSA_Q // hd
    c_kw = (NSA_Q + NSA_KV) // hd
    c_vs = (NSA_Q + 2 * NSA_KV) // hd
    c_vw = (NSA_Q + 3 * NSA_KV) // hd
    ncmp = kcmp.shape[2]

    def kv_spec(c0):
        return pl.BlockSpec((seq, hd), lambda b, gg, i: (b, c0 + gg))

    kern = functools.partial(_nsa_attn_kernel, tq=tq, tk=tk, seq=seq)
    return pl.pallas_call(
        kern,
        grid=(bsz, g, nq),
        in_specs=[
            pl.BlockSpec((tq, gw), lambda b, gg, i: (b * nq + i, gg)),
            kv_spec(c_ks), kv_spec(c_kw), kv_spec(c_vs), kv_spec(c_vw),
            pl.BlockSpec((1, 1, ncmp, hd), lambda b, gg, i: (b, gg, 0, 0)),
            pl.BlockSpec((1, 1, hd, ncmp), lambda b, gg, i: (b, gg, 0, 0)),
            pl.BlockSpec((tq, LANES), lambda b, gg, i: (b * nq + i, gg)),
            pl.BlockSpec(overlap_t.shape, lambda b, gg, i: (0, 0)),
            pl.BlockSpec(onehot.shape, lambda b, gg, i: (0, 0)),
        ],
        out_specs=pl.BlockSpec((tq, gw), lambda b, gg, i: (b * nq + i, gg)),
        out_shape=jax.ShapeDtypeStruct((m, NSA_Q), BF16),
        scratch_shapes=[
            pltpu.VMEM((seq, 2 * hd), BF16),
            pltpu.VMEM((seq // tq, hd, tq), BF16),
            pltpu.VMEM((seq + WINDOW, hd), BF16),
            pltpu.VMEM((seq // tq + WINDOW // tq, hd, tq), BF16),
            pltpu.VMEM((tk, NSA_REP * tq), BF16),
        ],
        compiler_params=_params(("parallel", "parallel", "arbitrary")),
        name="nsa_attention",
    )(pa, pa, pa, pa, pa, kcmp, vcmp_t, gates, overlap_t, onehot)


def _nsa_constants(seq):
    mm = np.arange(LANES)[:, None]
    n = np.arange(LANES)[None, :]
    c_start = n * CMP_STRIDE
    s_start = mm * SEL_BLOCK
    n_cmp = (seq - CMP_BLOCK) // CMP_STRIDE + 1
    overlap_t = ((c_start < s_start + SEL_BLOCK) & (c_start + CMP_BLOCK > s_start)
                 & (n < n_cmp) & (mm < seq // SEL_BLOCK))
    onehot = (np.arange(seq)[:, None] // SEL_BLOCK) == np.arange(LANES)[None, :]
    return (jnp.asarray(overlap_t.astype(np.float32), dtype=BF16),
            jnp.asarray(onehot.astype(np.float32), dtype=BF16))


def nsa_mixer_ln(x, cos, sin, w_in, gate_b, cmp_pos, cmp_w1, cmp_w2, w_out, g, b, *, bsz, seq):
    hd = NSA_HEAD_DIM
    grp = NSA_KV_GROUPS
    rep = NSA_REP
    kv = NSA_KV

    def wcols(order):
        return jnp.concatenate([w_in[:, NSA_Q + i * kv:NSA_Q + (i + 1) * kv] for i in order], axis=1)

    w_a = jnp.concatenate([w_in[:, :NSA_Q], wcols((2, 4, 3, 5))], axis=1).astype(BF16)
    pa = proj(x, w_a, cos, sin, n_rope=(NSA_Q + 2 * kv) // PROJ_WIDE, n_scale=NSA_Q // PROJ_WIDE,
              scale=hd ** -0.5 * LOG2E, out_dtype=BF16, tn=PROJ_WIDE)
    pb = proj(x, wcols((0, 1)).astype(BF16), cos, sin, n_rope=1, tn=kv)

    w_gl = w_in[:, NSA_Q + 6 * kv:].reshape(-1, 3, grp, rep).transpose(0, 2, 1, 3).reshape(-1, grp, 3 * rep)
    w_gl = jnp.pad(w_gl, ((0, 0), (0, 0), (0, LANES - 3 * rep))).reshape(-1, grp * LANES).astype(BF16)
    b_gl = gate_b.reshape(3, grp, rep).transpose(1, 0, 2).reshape(grp, 3 * rep)
    b_gl = jnp.pad(b_gl, ((0, 0), (0, LANES - 3 * rep))).reshape(1, grp * LANES)
    gates = gate_proj(x, w_gl, b_gl)

    kcmp, vcmp_t = nsa_compress(pb, cmp_pos, cmp_w1.astype(BF16), cmp_w2.astype(BF16), bsz=bsz, seq=seq)
    overlap_t, onehot = _nsa_constants(seq)
    o = nsa_attention(pa, kcmp, vcmp_t, gates, overlap_t, onehot, bsz=bsz, seq=seq)
    return out_ln(o, w_out.astype(BF16), x, g, b)


def _conv_kernel(x_ref, wb_ref, wc_ref, wh_ref, cw_ref, wo_ref, g_ref, b_ref, o_ref,
                 xb_ref, acc_ref, tail_ref, *, tiles_per_seq):
    i = pl.program_id(0)
    j = pl.program_id(1)

    @pl.when(j == 0)
    def _():
        xb_ref[...] = x_ref[...].astype(BF16)
        acc_ref[...] = jnp.zeros_like(acc_ref)

    xb = xb_ref[...]
    bg = _dot(xb, wb_ref[...])
    u = _dot(xb, wc_ref[...]) * _dot(xb, wh_ref[...])
    tm = u.shape[0]
    first = (i % tiles_per_seq) == 0
    prev = jnp.where(first, 0.0, tail_ref[j])
    tail_ref[j] = u[tm - 8:tm, :]
    rid = lax.broadcasted_iota(jnp.int32, u.shape, 0)
    u1 = jnp.where(rid == 0, prev[7:8, :], pltpu.roll(u, 1, axis=0))
    u2 = pltpu.roll(u, 2, axis=0)
    u2 = jnp.where(rid == 0, prev[6:7, :], jnp.where(rid == 1, prev[7:8, :], u2))
    cw = cw_ref[...]
    y = cw[0:1, :] * u2 + cw[1:2, :] * u1 + cw[2:3, :] * u
    acc_ref[...] += _dot((bg * y).astype(BF16), wo_ref[...])

    @pl.when(j == pl.num_programs(1) - 1)
    def _():
        yy = DEEPNORM_ALPHA * x_ref[...] + acc_ref[...]
        o_ref[...] = _layer_norm(yy, g_ref[...], b_ref[...])


def conv_mixer_ln(x, w_in, conv_w, w_out, g, b, *, seq, tm=512, tn=512):
    m, d = x.shape
    nj = d // tn
    kern = functools.partial(_conv_kernel, tiles_per_seq=seq // tm)
    return pl.pallas_call(
        kern,
        grid=(m // tm, nj),
        in_specs=[
            pl.BlockSpec((tm, d), lambda i, j: (i, 0)),
            pl.BlockSpec((d, tn), lambda i, j: (0, j)),
            pl.BlockSpec((d, tn), lambda i, j: (0, j + nj)),
            pl.BlockSpec((d, tn), lambda i, j: (0, j + 2 * nj)),
            pl.BlockSpec((CONV_WIDTH, tn), lambda i, j: (0, j)),
            pl.BlockSpec((tn, d), lambda i, j: (j, 0)),
            pl.BlockSpec((1, d), lambda i, j: (0, 0)),
            pl.BlockSpec((1, d), lambda i, j: (0, 0)),
        ],
        out_specs=pl.BlockSpec((tm, d), lambda i, j: (i, 0)),
        out_shape=jax.ShapeDtypeStruct((m, d), F32),
        scratch_shapes=[pltpu.VMEM((tm, d), BF16), pltpu.VMEM((tm, d), F32),
                        pltpu.VMEM((nj, 8, tn), F32)],
        compiler_params=_params(("arbitrary", "arbitrary")),
        name="conv_mixer_ln",
    )(x, w_in, w_in, w_in, conv_w, w_out, g, b)


def _log_sigmoid(z):
    return jnp.minimum(z, 0.0) - jnp.log1p(jnp.exp(-jnp.abs(z)))


def _gla_kernel(q_ref, k_ref, v_ref, r_ref, a_ref, wa_ref, ba_ref, ng_ref, o_ref, st_ref, upd_ref):
    i = pl.program_id(2)
    c_sz = GLA_CHUNK
    dk, dv = GLA_HEAD_K, GLA_HEAD_V
    n_c = q_ref.shape[0] // c_sz
    heads = range(GLA_HEADS_PER_STEP)

    @pl.when(i == 0)
    def _():
        st_ref[...] = jnp.zeros_like(st_ref)

    rr = lax.broadcasted_iota(jnp.int32, (c_sz, c_sz), 0)
    cc = lax.broadcasted_iota(jnp.int32, (c_sz, c_sz), 1)
    causal = cc <= rr
    tril = jnp.where(causal, 1.0, 0.0).astype(BF16)
    chunks = [slice(c * c_sz, (c + 1) * c_sz) for c in range(n_c)]

    z = _dot(a_ref[...].astype(BF16), wa_ref[...]) + ba_ref[...]
    gk = _log_sigmoid(z) / GLA_GATE_NORM
    hi, mid, lo = _split3(gk)
    bcum = jnp.concatenate(
        [_dot(tril, hi[rs]) + _dot(tril, mid[rs]) + _dot(tril, lo[rs]) for rs in chunks], axis=0)
    b_last = [bcum[rs][c_sz - 1:c_sz, :] for rs in chunks]
    b_last_rows = jnp.concatenate([jnp.broadcast_to(bl, (c_sz, bl.shape[1])) for bl in b_last], axis=0)
    qd = (q_ref[...] * (dk ** -0.5) * jnp.exp(bcum)).astype(BF16)
    k = k_ref[...]
    kd = (k * jnp.exp(-bcum)).astype(BF16)
    kl = (k * jnp.exp(b_last_rows - bcum)).astype(BF16)
    v = v_ref[...].astype(BF16)

    o_intra = []
    for hh in heads:
        ks, vs = slice(hh * dk, (hh + 1) * dk), slice(hh * dv, (hh + 1) * dv)
        outs = []
        for c, rs in enumerate(chunks):
            att = jnp.where(causal, _dot_nt(qd[rs, ks], kd[rs, ks]), 0.0).astype(BF16)
            outs.append(_dot(att, v[rs, vs]))
            upd_ref[hh, c] = _dot_tn(v[rs, vs], kl[rs, ks])
        o_intra.append(outs)

    st = [st_ref[hh] for hh in heads]
    ng = ng_ref[...]
    for c, rs in enumerate(chunks):
        for hh in heads:
            ks, vs = slice(hh * dk, (hh + 1) * dk), slice(hh * dv, (hh + 1) * dv)
            o = o_intra[hh][c] + _dot_nt(qd[rs, ks], st[hh].astype(BF16))
            st[hh] = st[hh] * jnp.exp(b_last[c][:, ks]) + upd_ref[hh, c]
            o = o * lax.rsqrt(jnp.mean(o * o, axis=-1, keepdims=True) + LN_EPS) * ng
            o_ref[rs, vs] = (o * _silu(r_ref[rs, vs])).astype(BF16)
    for hh in heads:
        st_ref[hh] = st[hh]


def gla_scan(p, a, wa, ba, ng, *, bsz, seq, tt=512):
    m = p.shape[0]
    dk, dv = GLA_HEAD_K, GLA_HEAD_V
    hp = GLA_HEADS_PER_STEP
    nt = seq // tt
    c_k = GLA_KEY_DIM // (hp * dk)
    c_v = 2 * GLA_KEY_DIM // (hp * dv)
    c_r = (2 * GLA_KEY_DIM + GLA_VAL_DIM) // (hp * dv)
    row = lambda b, h, i: b * nt + i
    return pl.pallas_call(
        _gla_kernel,
        grid=(bsz, GLA_HEADS // hp, nt),
        in_specs=[
            pl.BlockSpec((tt, hp * dk), lambda b, h, i: (row(b, h, i), h)),
            pl.BlockSpec((tt, hp * dk), lambda b, h, i: (row(b, h, i), c_k + h)),
            pl.BlockSpec((tt, hp * dv), lambda b, h, i: (row(b, h, i), c_v + h)),
            pl.BlockSpec((tt, hp * dv), lambda b, h, i: (row(b, h, i), c_r + h)),
            pl.BlockSpec((tt, LANES), lambda b, h, i: (row(b, h, i), 0)),
            pl.BlockSpec((LANES, hp * dk), lambda b, h, i: (0, h)),
            pl.BlockSpec((1, hp * dk), lambda b, h, i: (0, h)),
            pl.BlockSpec((1, dv), lambda b, h, i: (0, 0)),
        ],
        out_specs=pl.BlockSpec((tt, hp * dv), lambda b, h, i: (row(b, h, i), h)),
        out_shape=jax.ShapeDtypeStruct((m, GLA_VAL_DIM), BF16),
        scratch_shapes=[pltpu.VMEM((hp, dv, dk), F32),
                        pltpu.VMEM((hp, tt // GLA_CHUNK, dv, dk), F32)],
        compiler_params=_params(("parallel", "parallel", "arbitrary")),
        name="gla_scan",
    )(p, p, p, p, a, wa, ba, ng)


def gla_mixer_ln(x, w_in, w_a2, b_a, norm_g, w_out, g, b, *, bsz, seq):
    n_main = 2 * GLA_KEY_DIM + 2 * GLA_VAL_DIM
    dummy = jnp.zeros((x.shape[0], LANES), F32)
    p = proj(x, w_in[:, :n_main].astype(BF16), dummy, dummy, tn=PROJ_WIDE)
    w_a = jnp.pad(w_in[:, n_main:], ((0, 0), (0, LANES - GLA_GATE_RANK))).astype(BF16)
    a = proj(x, w_a, dummy, dummy, tn=LANES)
    wa2 = jnp.pad(w_a2, ((0, LANES - GLA_GATE_RANK), (0, 0))).astype(BF16)
    o = gla_scan(p, a, wa2, b_a[None, :], norm_g[None, :], bsz=bsz, seq=seq)
    return out_ln(o, w_out.astype(BF16), x, g, b)


def kernel(x, positions, ln_g, ln_b, ffn_w_in, ffn_w_out, nsa_w_in, nsa_gate_b, nsa_cmp_pos,
           nsa_cmp_w1, nsa_cmp_w2, nsa_w_out, conv_w_in, conv_w, conv_w_out, gla_w_in, gla_w_a2,
           gla_b_a, gla_norm_g, gla_w_out):
    bsz, seq, d = x.shape
    m = bsz * seq
    h = x.reshape(m, d)
    cos, sin = rope_tables(positions.reshape(m, 1).astype(F32))
    ffn_in = ffn_w_in.astype(BF16)
    ffn_out = ffn_w_out.astype(BF16)
    for i in range(DEPTH):
        lg = lambda k: ln_g[i, k][None, :]
        lb = lambda k: ln_b[i, k][None, :]
        h = ffn_ln(h, ffn_in, ffn_out, lg(0), lb(0), i, 0)
        kind, j = i % N_MIXERS, i // N_MIXERS
        if kind == 0:
            h = nsa_mixer_ln(h, cos, sin, nsa_w_in[j], nsa_gate_b[j], nsa_cmp_pos[j], nsa_cmp_w1[j],
                             nsa_cmp_w2[j], nsa_w_out[j], lg(1), lb(1), bsz=bsz, seq=seq)
        elif kind == 1:
            h = conv_mixer_ln(h, conv_w_in[j].astype(BF16), conv_w[j], conv_w_out[j].astype(BF16),
                              lg(1), lb(1), seq=seq)
        else:
            h = gla_mixer_ln(h, gla_w_in[j], gla_w_a2[j], gla_b_a[j], gla_norm_g[j], gla_w_out[j],
                             lg(1), lb(1), bsz=bsz, seq=seq)
        h = ffn_ln(h, ffn_in, ffn_out, lg(2), lb(2), i, 1)
    return h.reshape(bsz, seq, d)
```

```python
import functools

import numpy as np
import jax
import jax.numpy as jnp
from jax import lax
from jax.experimental import pallas as pl
from jax.experimental.pallas import tpu as pltpu

F32 = jnp.float32
BF16 = jnp.bfloat16

D_MODEL = 2048
DEPTH = 4
N_MIXERS = 3
DEEPNORM_ALPHA = (2.0 * DEPTH) ** 0.25
LN_EPS = 1e-5
MACARON_WEIGHT = 0.5
D_FF = 5632

NSA_HEADS = 16
NSA_KV_GROUPS = 4
NSA_REP = NSA_HEADS // NSA_KV_GROUPS
NSA_HEAD_DIM = D_MODEL // NSA_HEADS
NSA_Q = NSA_HEADS * NSA_HEAD_DIM
NSA_KV = NSA_KV_GROUPS * NSA_HEAD_DIM
CMP_BLOCK = 32
CMP_STRIDE = 16
SEL_BLOCK = 64
SEL_TOPK = 16
WINDOW = 512
ROPE_THETA = 10000.0
NEG = -1e30
FORCE = 1e3
LOG2E = float(np.log2(np.e))

CONV_WIDTH = 3

GLA_HEADS = 4
GLA_KEY_DIM = D_MODEL // 2
GLA_VAL_DIM = D_MODEL
GLA_HEAD_K = GLA_KEY_DIM // GLA_HEADS
GLA_HEAD_V = GLA_VAL_DIM // GLA_HEADS
GLA_GATE_RANK = 16
GLA_GATE_NORM = 16.0
GLA_CHUNK = 64
GLA_HEADS_PER_STEP = 2

LANES = 128
PROJ_SLABS = 4
OUT_SLABS = 2
PROJ_WIDE = 1024
VMEM_LIMIT = 56 * 1024 * 1024


def _params(sem):
    return pltpu.CompilerParams(dimension_semantics=sem, vmem_limit_bytes=VMEM_LIMIT)


def _layer_norm(y, g, b):
    mu = jnp.mean(y, axis=-1, keepdims=True)
    d = y - mu
    var = jnp.mean(d * d, axis=-1, keepdims=True)
    return d * lax.rsqrt(var + LN_EPS) * g + b


def _silu(h):
    return h * jax.nn.sigmoid(h)


def _dot(a, b):
    return jnp.dot(a, b, preferred_element_type=F32)


def _dot_nt(a, b):
    return lax.dot_general(a, b, (((1,), (1,)), ((), ())), preferred_element_type=F32)


def _dot_tn(a, b):
    return lax.dot_general(a, b, (((0,), (0,)), ((), ())), preferred_element_type=F32)


def _split3(x):
    hi = x.astype(BF16)
    r1 = x - hi.astype(F32)
    mid = r1.astype(BF16)
    lo = (r1 - mid.astype(F32)).astype(BF16)
    return hi, mid, lo


def _ffn_kernel(x_ref, wga_ref, wua_ref, woa_ref, wgb_ref, wub_ref, wob_ref, g_ref, b_ref, o_ref,
                xb_ref, acc_ref, *, n_ff_tiles):
    j = pl.program_id(1)

    def ff_tile(xb, wg_ref, wu_ref, wo_ref):
        h = _dot(xb, wg_ref[...])
        u = _dot(xb, wu_ref[...])
        return _dot((_silu(h) * u).astype(BF16), wo_ref[...])

    def ff_pair(xb):
        return ff_tile(xb, wga_ref, wua_ref, woa_ref) + ff_tile(xb, wgb_ref, wub_ref, wob_ref)

    first = j == 0
    paired = 2 * j + 1 < n_ff_tiles

    @pl.when(first)
    def _():
        xb = x_ref[...].astype(BF16)
        xb_ref[...] = xb
        acc_ref[...] = ff_pair(xb)

    @pl.when(jnp.logical_not(first) & paired)
    def _():
        acc_ref[...] += ff_pair(xb_ref[...])

    @pl.when(jnp.logical_not(first) & jnp.logical_not(paired))
    def _():
        acc_ref[...] += ff_tile(xb_ref[...], wga_ref, wua_ref, woa_ref)

    @pl.when(j == pl.num_programs(1) - 1)
    def _():
        y = DEEPNORM_ALPHA * x_ref[...] + MACARON_WEIGHT * acc_ref[...]
        o_ref[...] = _layer_norm(y, g_ref[...], b_ref[...])


def ffn_ln(x, w_in, w_out, g, b, layer, half, *, tm=512, tf=512):
    m, d = x.shape
    ff = w_out.shape[2]
    nt = ff // tf
    assert nt >= 2
    nj = (nt + 1) // 2

    def tile_a(j):
        return 2 * j

    def tile_b(j):
        return jnp.minimum(2 * j + 1, nt - 1)

    def w_specs(tile):
        return [
            pl.BlockSpec((None, None, d, tf), lambda i, j: (layer, half, 0, tile(j))),
            pl.BlockSpec((None, None, d, tf), lambda i, j: (layer, half, 0, tile(j) + nt)),
            pl.BlockSpec((None, None, tf, d), lambda i, j: (layer, half, tile(j), 0)),
        ]

    return pl.pallas_call(
        functools.partial(_ffn_kernel, n_ff_tiles=nt),
        grid=(m // tm, nj),
        in_specs=[pl.BlockSpec((tm, d), lambda i, j: (i, 0))] + w_specs(tile_a) + w_specs(tile_b) + [
            pl.BlockSpec((1, d), lambda i, j: (0, 0)),
            pl.BlockSpec((1, d), lambda i, j: (0, 0)),
        ],
        out_specs=pl.BlockSpec((tm, d), lambda i, j: (i, 0)),
        out_shape=jax.ShapeDtypeStruct((m, d), F32),
        scratch_shapes=[pltpu.VMEM((tm, d), BF16), pltpu.VMEM((tm, d), F32)],
        compiler_params=_params(("parallel", "arbitrary")),
        name="ffn_ln",
    )(x, w_in, w_in, w_out, w_in, w_in, w_out, g, b)


def _proj_kernel(x_ref, w_ref, cos_ref, sin_ref, o_ref, xb_ref, *, n_rope, n_scale, scale):
    j = pl.program_id(1)

    @pl.when(j == 0)
    def _():
        xb_ref[...] = x_ref[...].astype(BF16)

    tm, tn = o_ref.shape
    slabs = [slice(r * tm // PROJ_SLABS, (r + 1) * tm // PROJ_SLABS) for r in range(PROJ_SLABS)]
    ys = [_dot(xb_ref[rs, :], w_ref[...]) for rs in slabs]

    if n_rope > 0:
        rotary = j < n_rope
        sc = jnp.where(j < n_scale, scale, 1.0).astype(F32)
        for rs, y in zip(slabs, ys):
            cos = jnp.where(rotary, cos_ref[rs, :], 1.0)
            sin = jnp.where(rotary, sin_ref[rs, :], 0.0)
            for hh in range(tn // LANES):
                t = y[:, hh * LANES:(hh + 1) * LANES]
                rot = pltpu.roll(t, LANES // 2, axis=1)
                o_ref[rs, hh * LANES:(hh + 1) * LANES] = ((t * cos + rot * sin) * sc).astype(o_ref.dtype)
    else:
        for rs, y in zip(slabs, ys):
            o_ref[rs, :] = y.astype(o_ref.dtype)


def proj(x, w, cos, sin, *, n_rope=0, n_scale=0, scale=1.0, out_dtype=F32, tm=1024, tn=512):
    m, d = x.shape
    n = w.shape[1]
    kern = functools.partial(_proj_kernel, n_rope=n_rope, n_scale=n_scale, scale=scale)
    return pl.pallas_call(
        kern,
        grid=(m // tm, n // tn),
        in_specs=[
            pl.BlockSpec((tm, d), lambda i, j: (i, 0)),
            pl.BlockSpec((d, tn), lambda i, j: (0, j)),
            pl.BlockSpec((tm, LANES), lambda i, j: (i, 0)),
            pl.BlockSpec((tm, LANES), lambda i, j: (i, 0)),
        ],
        out_specs=pl.BlockSpec((tm, tn), lambda i, j: (i, j)),
        out_shape=jax.ShapeDtypeStruct((m, n), out_dtype),
        scratch_shapes=[pltpu.VMEM((tm, d), BF16)],
        compiler_params=_params(("parallel", "arbitrary")),
        name="proj",
    )(x, w, cos, sin)


def _rope_table_kernel(pos_ref, inv_ref, sign_ref, cos_ref, sin_ref):
    ang = pos_ref[...] * inv_ref[...]
    cos_ref[...] = jnp.cos(ang)
    sin_ref[...] = jnp.sin(ang) * sign_ref[...]


def rope_tables(pos_f32, *, tm=2048):
    m = pos_f32.shape[0]
    hd = NSA_HEAD_DIM
    inv = ROPE_THETA ** (-jnp.arange(0, hd, 2, dtype=F32) / hd)
    inv_full = jnp.concatenate([inv, inv])[None, :]
    sign = jnp.concatenate([-jnp.ones((hd // 2,), F32), jnp.ones((hd // 2,), F32)])[None, :]
    return pl.pallas_call(
        _rope_table_kernel,
        grid=(m // tm,),
        in_specs=[
            pl.BlockSpec((tm, 1), lambda i: (i, 0)),
            pl.BlockSpec((1, hd), lambda i: (0, 0)),
            pl.BlockSpec((1, hd), lambda i: (0, 0)),
        ],
        out_specs=[pl.BlockSpec((tm, hd), lambda i: (i, 0))] * 2,
        out_shape=[jax.ShapeDtypeStruct((m, hd), F32)] * 2,
        compiler_params=_params(("parallel",)),
        name="rope_tables",
    )(pos_f32, inv_full, sign)


def _gate_kernel(x_ref, w_ref, b_ref, o_ref):
    z = _dot(x_ref[...].astype(BF16), w_ref[...]) + b_ref[...]
    o_ref[...] = jax.nn.sigmoid(z)


def gate_proj(x, w, b, *, tm=1024):
    m, d = x.shape
    n = w.shape[1]
    return pl.pallas_call(
        _gate_kernel,
        grid=(m // tm,),
        in_specs=[
            pl.BlockSpec((tm, d), lambda i: (i, 0)),
            pl.BlockSpec((d, n), lambda i: (0, 0)),
            pl.BlockSpec((1, n), lambda i: (0, 0)),
        ],
        out_specs=pl.BlockSpec((tm, n), lambda i: (i, 0)),
        out_shape=jax.ShapeDtypeStruct((m, n), F32),
        compiler_params=_params(("parallel",)),
        name="gate_proj",
    )(x, w, b)


def _out_ln_kernel(a_ref, w_ref, x_ref, g_ref, b_ref, o_ref):
    tm = a_ref.shape[0]
    slabs = [slice(r * tm // OUT_SLABS, (r + 1) * tm // OUT_SLABS) for r in range(OUT_SLABS)]
    ys = [_dot(a_ref[rs, :], w_ref[...]) for rs in slabs]
    for rs, y in zip(slabs, ys):
        o_ref[rs, :] = _layer_norm(DEEPNORM_ALPHA * x_ref[rs, :] + y, g_ref[...], b_ref[...])


def out_ln(a, w, x, g, b, *, tm=512):
    m, d = x.shape
    k = a.shape[1]
    return pl.pallas_call(
        _out_ln_kernel,
        grid=(m // tm,),
        in_specs=[
            pl.BlockSpec((tm, k), lambda i: (i, 0)),
            pl.BlockSpec((k, d), lambda i: (0, 0)),
            pl.BlockSpec((tm, d), lambda i: (i, 0)),
            pl.BlockSpec((1, d), lambda i: (0, 0)),
            pl.BlockSpec((1, d), lambda i: (0, 0)),
        ],
        out_specs=pl.BlockSpec((tm, d), lambda i: (i, 0)),
        out_shape=jax.ShapeDtypeStruct((m, d), F32),
        compiler_params=_params(("parallel",)),
        name="out_ln",
    )(a, w, x, g, b)


def _gelu_tanh(x):
    c = float(np.sqrt(2.0 / np.pi))
    return 0.5 * x * (1.0 + jnp.tanh(c * (x + 0.044715 * (x * x * x))))


def _cmp_kernel(kc_ref, vc_ref, pos_ref, w1_ref, w2_ref, ko_ref, vo_ref):
    seq, hd = kc_ref.shape
    nrow = seq // CMP_STRIDE
    half = CMP_STRIDE * hd
    for idx, (src, dst) in enumerate(((kc_ref, ko_ref), (vc_ref, vo_ref))):
        first, second = [], []
        for l in range(CMP_STRIDE):
            xl = src[pl.ds(l, nrow, stride=CMP_STRIDE), :]
            first.append(xl + pos_ref[idx, l:l + 1, :])
            second.append(xl + pos_ref[idx, CMP_STRIDE + l:CMP_STRIDE + l + 1, :])
        ya = _dot(jnp.concatenate(first, axis=1).astype(BF16), w1_ref[idx, 0:half, :])
        yb = _dot(jnp.concatenate(second, axis=1).astype(BF16), w1_ref[idx, half:2 * half, :])
        h = _gelu_tanh(ya + pltpu.roll(yb, nrow - 1, axis=0))
        out = _dot(h.astype(BF16), w2_ref[idx])
        if idx == 0:
            dst[0, 0] = out.astype(BF16)
        else:
            dst[0, 0] = out.astype(BF16).astype(F32).T.astype(BF16)


def nsa_compress(pb, pos, w1, w2, *, bsz, seq):
    g = NSA_KV_GROUPS
    hd = NSA_HEAD_DIM
    nrow = seq // CMP_STRIDE
    oblk = pl.BlockSpec((1, 1, nrow, hd), lambda b, gg: (b, gg, 0, 0))
    return pl.pallas_call(
        _cmp_kernel,
        grid=(bsz, g),
        in_specs=[
            pl.BlockSpec((seq, hd), lambda b, gg: (b, gg)),
            pl.BlockSpec((seq, hd), lambda b, gg: (b, g + gg)),
            pl.BlockSpec(pos.shape, lambda b, gg: (0, 0, 0)),
            pl.BlockSpec(w1.shape, lambda b, gg: (0, 0, 0)),
            pl.BlockSpec(w2.shape, lambda b, gg: (0, 0, 0)),
        ],
        out_specs=[oblk, oblk],
        out_shape=[jax.ShapeDtypeStruct((bsz, g, nrow, hd), BF16)] * 2,
        compiler_params=_params(("parallel", "parallel")),
        name="nsa_compress",
    )(pb, pb, pos, w1, w2)


def _nsa_attn_kernel(q_ref, ks_ref, kw_ref, vs_ref, vw_ref, kc_ref, vct_ref, gate_ref,
                     ovt_ref, hot_ref, o_ref, ksa, vst, kwp, vwt, pbuf, *, tq, tk, seq):
    i = pl.program_id(2)
    hd = NSA_HEAD_DIM
    rep = NSA_REP
    cols = rep * tq
    nblk = seq // tq
    wblk = WINDOW // tq

    @pl.when(i == 0)
    def _():
        ksa[:, 0:hd] = ks_ref[...]
        ksa[:, hd:2 * hd] = hot_ref[...]
        kwp[0:WINDOW, :] = jnp.zeros((WINDOW, hd), BF16)
        kwp[WINDOW:WINDOW + seq, :] = kw_ref[...]
        for blk in range(wblk):
            vwt[blk] = jnp.zeros((hd, tq), BF16)
        for blk in range(nblk):
            rs = slice(blk * tq, (blk + 1) * tq)
            vst[blk] = vs_ref[rs, :].astype(F32).T.astype(BF16)
            vwt[wblk + blk] = vw_ref[rs, :].astype(F32).T.astype(BF16)

    t0 = i * tq
    blk0 = i
    q_t = jnp.concatenate([q_ref[:, r * hd:(r + 1) * hd].astype(F32).T for r in range(rep)],
                          axis=1).astype(BF16)

    def lane_t(shape):
        return t0 + (lax.broadcasted_iota(jnp.int32, shape, 1) & (tq - 1))

    def softmax_cols(s):
        mx = jnp.max(s, axis=0, keepdims=True)
        p = jnp.exp2(s - mx)
        return mx, p, jnp.sum(p, axis=0, keepdims=True)

    ku = lax.broadcasted_iota(jnp.int32, (tq, cols), 0)
    qu = lax.broadcasted_iota(jnp.int32, (tq, cols), 1) & (tq - 1)
    bias_diag = jnp.where(ku <= qu, 0.0, NEG)
    bias_first = jnp.where(ku > qu, 0.0, NEG)

    ncmp = kc_ref.shape[2]
    band = WINDOW + tq
    dstart = pl.multiple_of(t0, tq)
    s = _dot(kc_ref[0, 0], q_t)
    diag_forced = tq <= 2 * SEL_BLOCK
    if diag_forced:
        sd = _dot(ksa[pl.ds(dstart, tq), 0:hd], q_t) + bias_diag
    sw = _dot(kwp[pl.ds(dstart, band), :], q_t)

    n_idx = lax.broadcasted_iota(jnp.int32, (ncmp, cols), 0)
    valid = n_idx * CMP_STRIDE + (CMP_BLOCK - 1) <= lane_t((ncmp, cols))
    sm = jnp.where(valid, s, NEG)
    mx = jnp.max(sm, axis=0, keepdims=True)
    p = jnp.where(valid, jnp.exp2(sm - mx), 0.0)
    den = jnp.sum(p, axis=0, keepdims=True)
    p = p * (1.0 / jnp.where(den > 0.0, den, 1.0))
    o_cmp = _dot(vct_ref[0, 0], p.astype(BF16))

    pieces = []
    for jb in range(wblk + 1):
        piece = sw[jb * tq:(jb + 1) * tq]
        if jb == wblk:
            piece = piece + bias_diag
        else:
            piece = piece + jnp.where(t0 - WINDOW + jb * tq < 0, NEG, 0.0).astype(F32)
            if jb == 0:
                piece = piece + bias_first
        pieces.append(piece)
    _, pw, l_w = softmax_cols(jnp.concatenate(pieces, axis=0))
    vw_t = jnp.concatenate([vwt[blk0 + k] for k in range(wblk + 1)], axis=1)
    o_win = _dot(vw_t, pw.astype(BF16)) * (1.0 / l_w)

    psum = p[:, 0:tq]
    for r in range(1, rep):
        psum = psum + p[:, r * tq:(r + 1) * tq]
    hi, mid, lo = _split3(psum)
    ovt = ovt_ref[...]
    imp = _dot(ovt, hi) + _dot(ovt, mid) + _dot(ovt, lo)
    n_sel = seq // SEL_BLOCK
    b_idx = lax.broadcasted_iota(jnp.int32, (n_sel, tq), 0)
    cur = (t0 + lax.broadcasted_iota(jnp.int32, (n_sel, tq), 1)) // SEL_BLOCK
    forced = (b_idx == 0) | (b_idx == cur) | (b_idx == cur - 1)
    score = jnp.where(b_idx <= cur, imp[0:n_sel] + jnp.where(forced, FORCE, 0.0), NEG)
    rank = jnp.zeros((n_sel, tq), F32)
    for mp in range(n_sel):
        row = score[mp:mp + 1, :]
        beats = (row > score) | ((row == score) & (b_idx > mp))
        rank = rank + jnp.where(beats, 1.0, 0.0)
    selected = rank < float(SEL_TOPK)
    before_tile = b_idx < t0 // SEL_BLOCK

    def augment(keep):
        bias_blk = jnp.where(keep, 0.0, NEG)
        bias_blk = jnp.concatenate([bias_blk, jnp.zeros((hd - n_sel, tq), F32)], axis=0).astype(BF16)
        return jnp.concatenate([q_t, jnp.concatenate([bias_blk] * rep, axis=1)], axis=0)

    q_aug = augment(selected & before_tile)
    if not diag_forced:
        sd = _dot(ksa[pl.ds(dstart, tq), :], augment(selected & jnp.logical_not(before_tile))) + bias_diag

    m_i, pd, l_i = softmax_cols(sd)
    acc = _dot(vst[blk0], pd.astype(BF16))
    per = tk // tq
    pbuf[...] = jnp.zeros_like(pbuf)

    def v_chunk(c):
        return jnp.concatenate([vst[c * per + k] for k in range(per)], axis=1)

    n_chunks = (t0 + tk - 1) // tk
    last = jnp.maximum(n_chunks - 1, 0)

    def sel_body(c, carry):
        m_i, l_i, acc = carry
        off = pl.multiple_of(c * tk, tk)
        sc = _dot(ksa[pl.ds(off, tk), :], q_aug)
        pv = _dot(v_chunk(jnp.maximum(c - 1, 0)), pbuf[...])
        m_new = jnp.maximum(m_i, jnp.max(sc, axis=0, keepdims=True))
        pp = jnp.exp2(sc - m_new)
        pbuf[...] = pp.astype(BF16)
        alpha = jnp.exp2(m_i - m_new)
        l_new = alpha * l_i + jnp.sum(pp, axis=0, keepdims=True)
        return m_new, l_new, alpha * (acc + pv)

    _, l_s, acc_s = lax.fori_loop(0, n_chunks, sel_body, (m_i, l_i, acc))
    acc_s = acc_s + _dot(v_chunk(last), pbuf[...])
    o_sel = acc_s * (1.0 / l_s)

    g_t = gate_ref[...].T
    for r in range(rep):
        cs = slice(r * tq, (r + 1) * tq)
        o_r = (g_t[r:r + 1, :] * o_cmp[:, cs]
               + g_t[rep + r:rep + r + 1, :] * o_sel[:, cs]
               + g_t[2 * rep + r:2 * rep + r + 1, :] * o_win[:, cs])
        o_ref[:, r * hd:(r + 1) * hd] = o_r.T.astype(BF16)


def nsa_attention(pa, kcmp, vcmp_t, gates, overlap_t, onehot, *, bsz, seq, tq=512, tk=512):
    m = pa.shape[0]
    hd = NSA_HEAD_DIM
    g = NSA_KV_GROUPS
    nq = seq // tq
    gw = NSA_REP * hd
    c_ks = NSA_Q // hd
    c_kw = (NSA_Q + NSA_KV) // hd
    c_vs = (NSA_Q + 2 * NSA_KV) // hd
    c_vw = (NSA_Q + 3 * NSA_KV) // hd
    ncmp = kcmp.shape[2]

    def kv_spec(c0):
        return pl.BlockSpec((seq, hd), lambda b, gg, i: (b, c0 + gg))

    kern = functools.partial(_nsa_attn_kernel, tq=tq, tk=tk, seq=seq)
    return pl.pallas_call(
        kern,
        grid=(bsz, g, nq),
        in_specs=[
            pl.BlockSpec((tq, gw), lambda b, gg, i: (b * nq + i, gg)),
            kv_spec(c_ks), kv_spec(c_kw), kv_spec(c_vs), kv_spec(c_vw),
            pl.BlockSpec((1, 1, ncmp, hd), lambda b, gg, i: (b, gg, 0, 0)),
            pl.BlockSpec((1, 1, hd, ncmp), lambda b, gg, i: (b, gg, 0, 0)),
            pl.BlockSpec((tq, LANES), lambda b, gg, i: (b * nq + i, gg)),
            pl.BlockSpec(overlap_t.shape, lambda b, gg, i: (0, 0)),
            pl.BlockSpec(onehot.shape, lambda b, gg, i: (0, 0)),
        ],
        out_specs=pl.BlockSpec((tq, gw), lambda b, gg, i: (b * nq + i, gg)),
        out_shape=jax.ShapeDtypeStruct((m, NSA_Q), BF16),
        scratch_shapes=[
            pltpu.VMEM((seq, 2 * hd), BF16),
            pltpu.VMEM((seq // tq, hd, tq), BF16),
            pltpu.VMEM((seq + WINDOW, hd), BF16),
            pltpu.VMEM((seq // tq + WINDOW // tq, hd, tq), BF16),
            pltpu.VMEM((tk, NSA_REP * tq), BF16),
        ],
        compiler_params=_params(("parallel", "parallel", "arbitrary")),
        name="nsa_attention",
    )(pa, pa, pa, pa, pa, kcmp, vcmp_t, gates, overlap_t, onehot)


def _nsa_constants(seq):
    mm = np.arange(LANES)[:, None]
    n = np.arange(LANES)[None, :]
    c_start = n * CMP_STRIDE
    s_start = mm * SEL_BLOCK
    n_cmp = (seq - CMP_BLOCK) // CMP_STRIDE + 1
    overlap_t = ((c_start < s_start + SEL_BLOCK) & (c_start + CMP_BLOCK > s_start)
                 & (n < n_cmp) & (mm < seq // SEL_BLOCK))
    onehot = (np.arange(seq)[:, None] // SEL_BLOCK) == np.arange(LANES)[None, :]
    return (jnp.asarray(overlap_t.astype(np.float32), dtype=BF16),
            jnp.asarray(onehot.astype(np.float32), dtype=BF16))


def nsa_mixer_ln(x, cos, sin, w_in, gate_b, cmp_pos, cmp_w1, cmp_w2, w_out, g, b, *, bsz, seq):
    hd = NSA_HEAD_DIM
    grp = NSA_KV_GROUPS
    rep = NSA_REP
    kv = NSA_KV

    def wcols(order):
        return jnp.concatenate([w_in[:, NSA_Q + i * kv:NSA_Q + (i + 1) * kv] for i in order], axis=1)

    w_a = jnp.concatenate([w_in[:, :NSA_Q], wcols((2, 4, 3, 5))], axis=1).astype(BF16)
    pa = proj(x, w_a, cos, sin, n_rope=(NSA_Q + 2 * kv) // PROJ_WIDE, n_scale=NSA_Q // PROJ_WIDE,
              scale=hd ** -0.5 * LOG2E, out_dtype=BF16, tn=PROJ_WIDE)
    pb = proj(x, wcols((0, 1)).astype(BF16), cos, sin, n_rope=1, tn=kv)

    w_gl = w_in[:, NSA_Q + 6 * kv:].reshape(-1, 3, grp, rep).transpose(0, 2, 1, 3).reshape(-1, grp, 3 * rep)
    w_gl = jnp.pad(w_gl, ((0, 0), (0, 0), (0, LANES - 3 * rep))).reshape(-1, grp * LANES).astype(BF16)
    b_gl = gate_b.reshape(3, grp, rep).transpose(1, 0, 2).reshape(grp, 3 * rep)
    b_gl = jnp.pad(b_gl, ((0, 0), (0, LANES - 3 * rep))).reshape(1, grp * LANES)
    gates = gate_proj(x, w_gl, b_gl)

    kcmp, vcmp_t = nsa_compress(pb, cmp_pos, cmp_w1.astype(BF16), cmp_w2.astype(BF16), bsz=bsz, seq=seq)
    overlap_t, onehot = _nsa_constants(seq)
    o = nsa_attention(pa, kcmp, vcmp_t, gates, overlap_t, onehot, bsz=bsz, seq=seq)
    return out_ln(o, w_out.astype(BF16), x, g, b)


def _conv_kernel(x_ref, wba_ref, wca_ref, wha_ref, cwa_ref, woa_ref, wbb_ref, wcb_ref, whb_ref, cwb_ref,
                 wob_ref, g_ref, b_ref, o_ref, xb_ref, acc_ref, tail_ref, *, tiles_per_seq):
    i = pl.program_id(0)
    j = pl.program_id(1)
    seq_start = (i % tiles_per_seq) == 0

    def channel_tile(xb, t, wb_ref, wc_ref, wh_ref, cw_ref, wo_ref):
        bg = _dot(xb, wb_ref[...])
        u = _dot(xb, wc_ref[...]) * _dot(xb, wh_ref[...])
        tm = u.shape[0]
        prev = jnp.where(seq_start, 0.0, tail_ref[t])
        tail_ref[t] = u[tm - 8:tm, :]
        rid = lax.broadcasted_iota(jnp.int32, u.shape, 0)
        u1 = jnp.where(rid == 0, prev[7:8, :], pltpu.roll(u, 1, axis=0))
        u2 = pltpu.roll(u, 2, axis=0)
        u2 = jnp.where(rid == 0, prev[6:7, :], jnp.where(rid == 1, prev[7:8, :], u2))
        cw = cw_ref[...]
        y = cw[0:1, :] * u2 + cw[1:2, :] * u1 + cw[2:3, :] * u
        return _dot((bg * y).astype(BF16), wo_ref[...])

    def pair(xb):
        return (channel_tile(xb, 2 * j, wba_ref, wca_ref, wha_ref, cwa_ref, woa_ref)
                + channel_tile(xb, 2 * j + 1, wbb_ref, wcb_ref, whb_ref, cwb_ref, wob_ref))

    @pl.when(j == 0)
    def _():
        xb = x_ref[...].astype(BF16)
        xb_ref[...] = xb
        acc_ref[...] = pair(xb)

    @pl.when(j > 0)
    def _():
        acc_ref[...] += pair(xb_ref[...])

    @pl.when(j == pl.num_programs(1) - 1)
    def _():
        yy = DEEPNORM_ALPHA * x_ref[...] + acc_ref[...]
        o_ref[...] = _layer_norm(yy, g_ref[...], b_ref[...])


def conv_mixer_ln(x, w_in, conv_w, w_out, g, b, *, seq, tm=512, tn=256):
    m, d = x.shape
    nt = d // tn
    assert nt % 2 == 0
    kern = functools.partial(_conv_kernel, tiles_per_seq=seq // tm)

    def tile_specs(off):
        return [
            pl.BlockSpec((d, tn), lambda i, j: (0, 2 * j + off)),
            pl.BlockSpec((d, tn), lambda i, j: (0, 2 * j + off + nt)),
            pl.BlockSpec((d, tn), lambda i, j: (0, 2 * j + off + 2 * nt)),
            pl.BlockSpec((CONV_WIDTH, tn), lambda i, j: (0, 2 * j + off)),
            pl.BlockSpec((tn, d), lambda i, j: (2 * j + off, 0)),
        ]

    return pl.pallas_call(
        kern,
        grid=(m // tm, nt // 2),
        in_specs=[pl.BlockSpec((tm, d), lambda i, j: (i, 0))] + tile_specs(0) + tile_specs(1) + [
            pl.BlockSpec((1, d), lambda i, j: (0, 0)),
            pl.BlockSpec((1, d), lambda i, j: (0, 0)),
        ],
        out_specs=pl.BlockSpec((tm, d), lambda i, j: (i, 0)),
        out_shape=jax.ShapeDtypeStruct((m, d), F32),
        scratch_shapes=[pltpu.VMEM((tm, d), BF16), pltpu.VMEM((tm, d), F32),
                        pltpu.VMEM((nt, 8, tn), F32)],
        compiler_params=_params(("arbitrary", "arbitrary")),
        name="conv_mixer_ln",
    )(x, w_in, w_in, w_in, conv_w, w_out, w_in, w_in, w_in, conv_w, w_out, g, b)


def _log_sigmoid(z):
    return jnp.minimum(z, 0.0) - jnp.log1p(jnp.exp(-jnp.abs(z)))


def _gla_kernel(q_ref, k_ref, v_ref, r_ref, a_ref, wa_ref, ba_ref, ng_ref, o_ref, st_ref, upd_ref):
    i = pl.program_id(2)
    c_sz = GLA_CHUNK
    dk, dv = GLA_HEAD_K, GLA_HEAD_V
    n_c = q_ref.shape[0] // c_sz
    heads = range(GLA_HEADS_PER_STEP)

    @pl.when(i == 0)
    def _():
        st_ref[...] = jnp.zeros_like(st_ref)

    rr = lax.broadcasted_iota(jnp.int32, (c_sz, c_sz), 0)
    cc = lax.broadcasted_iota(jnp.int32, (c_sz, c_sz), 1)
    causal = cc <= rr
    tril = jnp.where(causal, 1.0, 0.0).astype(BF16)
    chunks = [slice(c * c_sz, (c + 1) * c_sz) for c in range(n_c)]

    z = _dot(a_ref[...].astype(BF16), wa_ref[...]) + ba_ref[...]
    gk = _log_sigmoid(z) / GLA_GATE_NORM
    hi, mid, lo = _split3(gk)
    bcum = jnp.concatenate(
        [_dot(tril, hi[rs]) + _dot(tril, mid[rs]) + _dot(tril, lo[rs]) for rs in chunks], axis=0)
    b_last = [bcum[rs][c_sz - 1:c_sz, :] for rs in chunks]
    b_last_rows = jnp.concatenate([jnp.broadcast_to(bl, (c_sz, bl.shape[1])) for bl in b_last], axis=0)
    qd = (q_ref[...] * (dk ** -0.5) * jnp.exp(bcum)).astype(BF16)
    k = k_ref[...]
    kd = (k * jnp.exp(-bcum)).astype(BF16)
    kl = (k * jnp.exp(b_last_rows - bcum)).astype(BF16)
    v = v_ref[...].astype(BF16)

    o_intra = []
    for hh in heads:
        ks, vs = slice(hh * dk, (hh + 1) * dk), slice(hh * dv, (hh + 1) * dv)
        outs = []
        for c, rs in enumerate(chunks):
            att = jnp.where(causal, _dot_nt(qd[rs, ks], kd[rs, ks]), 0.0).astype(BF16)
            outs.append(_dot(att, v[rs, vs]))
            upd_ref[hh, c] = _dot_tn(v[rs, vs], kl[rs, ks])
        o_intra.append(outs)

    st = [st_ref[hh] for hh in heads]
    ng = ng_ref[...]
    for c, rs in enumerate(chunks):
        for hh in heads:
            ks, vs = slice(hh * dk, (hh + 1) * dk), slice(hh * dv, (hh + 1) * dv)
            o = o_intra[hh][c] + _dot_nt(qd[rs, ks], st[hh].astype(BF16))
            st[hh] = st[hh] * jnp.exp(b_last[c][:, ks]) + upd_ref[hh, c]
            o = o * lax.rsqrt(jnp.mean(o * o, axis=-1, keepdims=True) + LN_EPS) * ng
            o_ref[rs, vs] = (o * _silu(r_ref[rs, vs])).astype(BF16)
    for hh in heads:
        st_ref[hh] = st[hh]


def gla_scan(p, a, wa, ba, ng, *, bsz, seq, tt=512):
    m = p.shape[0]
    dk, dv = GLA_HEAD_K, GLA_HEAD_V
    hp = GLA_HEADS_PER_STEP
    nt = seq // tt
    c_k = GLA_KEY_DIM // (hp * dk)
    c_v = 2 * GLA_KEY_DIM // (hp * dv)
    c_r = (2 * GLA_KEY_DIM + GLA_VAL_DIM) // (hp * dv)
    row = lambda b, h, i: b * nt + i
    return pl.pallas_call(
        _gla_kernel,
        grid=(bsz, GLA_HEADS // hp, nt),
        in_specs=[
            pl.BlockSpec((tt, hp * dk), lambda b, h, i: (row(b, h, i), h)),
            pl.BlockSpec((tt, hp * dk), lambda b, h, i: (row(b, h, i), c_k + h)),
            pl.BlockSpec((tt, hp * dv), lambda b, h, i: (row(b, h, i), c_v + h)),
            pl.BlockSpec((tt, hp * dv), lambda b, h, i: (row(b, h, i), c_r + h)),
            pl.BlockSpec((tt, LANES), lambda b, h, i: (row(b, h, i), 0)),
            pl.BlockSpec((LANES, hp * dk), lambda b, h, i: (0, h)),
            pl.BlockSpec((1, hp * dk), lambda b, h, i: (0, h)),
            pl.BlockSpec((1, dv), lambda b, h, i: (0, 0)),
        ],
        out_specs=pl.BlockSpec((tt, hp * dv), lambda b, h, i: (row(b, h, i), h)),
        out_shape=jax.ShapeDtypeStruct((m, GLA_VAL_DIM), BF16),
        scratch_shapes=[pltpu.VMEM((hp, dv, dk), F32),
                        pltpu.VMEM((hp, tt // GLA_CHUNK, dv, dk), F32)],
        compiler_params=_params(("parallel", "parallel", "arbitrary")),
        name="gla_scan",
    )(p, p, p, p, a, wa, ba, ng)


def gla_mixer_ln(x, w_in, w_a2, b_a, norm_g, w_out, g, b, *, bsz, seq):
    n_main = 2 * GLA_KEY_DIM + 2 * GLA_VAL_DIM
    dummy = jnp.zeros((x.shape[0], LANES), F32)
    p = proj(x, w_in[:, :n_main].astype(BF16), dummy, dummy, tn=PROJ_WIDE)
    w_a = jnp.pad(w_in[:, n_main:], ((0, 0), (0, LANES - GLA_GATE_RANK))).astype(BF16)
    a = proj(x, w_a, dummy, dummy, tn=LANES)
    wa2 = jnp.pad(w_a2, ((0, LANES - GLA_GATE_RANK), (0, 0))).astype(BF16)
    o = gla_scan(p, a, wa2, b_a[None, :], norm_g[None, :], bsz=bsz, seq=seq)
    return out_ln(o, w_out.astype(BF16), x, g, b)


def kernel(x, positions, ln_g, ln_b, ffn_w_in, ffn_w_out, nsa_w_in, nsa_gate_b, nsa_cmp_pos,
           nsa_cmp_w1, nsa_cmp_w2, nsa_w_out, conv_w_in, conv_w, conv_w_out, gla_w_in, gla_w_a2,
           gla_b_a, gla_norm_g, gla_w_out):
    bsz, seq, d = x.shape
    m = bsz * seq
    h = x.reshape(m, d)
    cos, sin = rope_tables(positions.reshape(m, 1).astype(F32))
    ffn_in = ffn_w_in.astype(BF16)
    ffn_out = ffn_w_out.astype(BF16)
    for i in range(DEPTH):
        lg = lambda k: ln_g[i, k][None, :]
        lb = lambda k: ln_b[i, k][None, :]
        h = ffn_ln(h, ffn_in, ffn_out, lg(0), lb(0), i, 0)
        kind, j = i % N_MIXERS, i // N_MIXERS
        if kind == 0:
            h = nsa_mixer_ln(h, cos, sin, nsa_w_in[j], nsa_gate_b[j], nsa_cmp_pos[j], nsa_cmp_w1[j],
                             nsa_cmp_w2[j], nsa_w_out[j], lg(1), lb(1), bsz=bsz, seq=seq)
        elif kind == 1:
            h = conv_mixer_ln(h, conv_w_in[j].astype(BF16), conv_w[j], conv_w_out[j].astype(BF16),
                              lg(1), lb(1), seq=seq)
        else:
            h = gla_mixer_ln(h, gla_w_in[j], gla_w_a2[j], gla_b_a[j], gla_norm_g[j], gla_w_out[j],
                             lg(1), lb(1), bsz=bsz, seq=seq)
        h = ffn_ln(h, ffn_in, ffn_out, lg(2), lb(2), i, 1)
    return h.reshape(bsz, seq, d)
```

```python
import functools

import numpy as np
import jax
import jax.numpy as jnp
from jax import lax
from jax.experimental import pallas as pl
from jax.experimental.pallas import tpu as pltpu

F32 = jnp.float32
BF16 = jnp.bfloat16

D_MODEL = 2048
DEPTH = 4
N_MIXERS = 3
DEEPNORM_ALPHA = (2.0 * DEPTH) ** 0.25
LN_EPS = 1e-5
MACARON_WEIGHT = 0.5
D_FF = 5632

NSA_HEADS = 16
NSA_KV_GROUPS = 4
NSA_REP = NSA_HEADS // NSA_KV_GROUPS
NSA_HEAD_DIM = D_MODEL // NSA_HEADS
NSA_Q = NSA_HEADS * NSA_HEAD_DIM
NSA_KV = NSA_KV_GROUPS * NSA_HEAD_DIM
CMP_BLOCK = 32
CMP_STRIDE = 16
SEL_BLOCK = 64
SEL_TOPK = 16
WINDOW = 512
ROPE_THETA = 10000.0
NEG = -1e30
FORCE = 1e3
LOG2E = float(np.log2(np.e))

CONV_WIDTH = 3

GLA_HEADS = 4
GLA_KEY_DIM = D_MODEL // 2
GLA_VAL_DIM = D_MODEL
GLA_HEAD_K = GLA_KEY_DIM // GLA_HEADS
GLA_HEAD_V = GLA_VAL_DIM // GLA_HEADS
GLA_GATE_RANK = 16
GLA_GATE_NORM = 16.0
GLA_CHUNK = 64
GLA_HEADS_PER_STEP = 2

LANES = 128
PROJ_SLABS = 4
OUT_SLABS = 2
PROJ_WIDE = 1024
VMEM_LIMIT = 56 * 1024 * 1024


def _params(sem):
    return pltpu.CompilerParams(dimension_semantics=sem, vmem_limit_bytes=VMEM_LIMIT)


def _layer_norm(y, g, b):
    mu = jnp.mean(y, axis=-1, keepdims=True)
    d = y - mu
    var = jnp.mean(d * d, axis=-1, keepdims=True)
    return d * lax.rsqrt(var + LN_EPS) * g + b


def _silu(h):
    return h * jax.nn.sigmoid(h)


def _dot(a, b):
    return jnp.dot(a, b, preferred_element_type=F32)


def _dot_nt(a, b):
    return lax.dot_general(a, b, (((1,), (1,)), ((), ())), preferred_element_type=F32)


def _dot_tn(a, b):
    return lax.dot_general(a, b, (((0,), (0,)), ((), ())), preferred_element_type=F32)


def _split3(x):
    hi = x.astype(BF16)
    r1 = x - hi.astype(F32)
    mid = r1.astype(BF16)
    lo = (r1 - mid.astype(F32)).astype(BF16)
    return hi, mid, lo


def _ffn_kernel(x_ref, wga_ref, wua_ref, woa_ref, wgb_ref, wub_ref, wob_ref, g_ref, b_ref, o_ref,
                xb_ref, acc_ref, *, n_ff_tiles):
    j = pl.program_id(1)

    def ff_tile(xb, wg_ref, wu_ref, wo_ref):
        h = _dot(xb, wg_ref[...])
        u = _dot(xb, wu_ref[...])
        return _dot((_silu(h) * u).astype(BF16), wo_ref[...])

    def ff_pair(xb):
        return ff_tile(xb, wga_ref, wua_ref, woa_ref) + ff_tile(xb, wgb_ref, wub_ref, wob_ref)

    first = j == 0
    paired = 2 * j + 1 < n_ff_tiles

    @pl.when(first)
    def _():
        xb = x_ref[...].astype(BF16)
        xb_ref[...] = xb
        acc_ref[...] = ff_pair(xb)

    @pl.when(jnp.logical_not(first) & paired)
    def _():
        acc_ref[...] += ff_pair(xb_ref[...])

    @pl.when(jnp.logical_not(first) & jnp.logical_not(paired))
    def _():
        acc_ref[...] += ff_tile(xb_ref[...], wga_ref, wua_ref, woa_ref)

    @pl.when(j == pl.num_programs(1) - 1)
    def _():
        y = DEEPNORM_ALPHA * x_ref[...] + MACARON_WEIGHT * acc_ref[...]
        o_ref[...] = _layer_norm(y, g_ref[...], b_ref[...])


def ffn_ln(x, w_in, w_out, g, b, layer, half, *, tm=512, tf=512):
    m, d = x.shape
    ff = w_out.shape[2]
    nt = ff // tf
    assert nt >= 2
    nj = (nt + 1) // 2

    def tile_a(j):
        return 2 * j

    def tile_b(j):
        return jnp.minimum(2 * j + 1, nt - 1)

    def w_specs(tile):
        return [
            pl.BlockSpec((None, None, d, tf), lambda i, j: (layer, half, 0, tile(j))),
            pl.BlockSpec((None, None, d, tf), lambda i, j: (layer, half, 0, tile(j) + nt)),
            pl.BlockSpec((None, None, tf, d), lambda i, j: (layer, half, tile(j), 0)),
        ]

    return pl.pallas_call(
        functools.partial(_ffn_kernel, n_ff_tiles=nt),
        grid=(m // tm, nj),
        in_specs=[pl.BlockSpec((tm, d), lambda i, j: (i, 0))] + w_specs(tile_a) + w_specs(tile_b) + [
            pl.BlockSpec((1, d), lambda i, j: (0, 0)),
            pl.BlockSpec((1, d), lambda i, j: (0, 0)),
        ],
        out_specs=pl.BlockSpec((tm, d), lambda i, j: (i, 0)),
        out_shape=jax.ShapeDtypeStruct((m, d), F32),
        scratch_shapes=[pltpu.VMEM((tm, d), BF16), pltpu.VMEM((tm, d), F32)],
        compiler_params=_params(("parallel", "arbitrary")),
        name="ffn_ln",
    )(x, w_in, w_in, w_out, w_in, w_in, w_out, g, b)


def _proj_kernel(x_ref, w_ref, cos_ref, sin_ref, *rest, n_rope, n_scale, scale, gate_from):
    bias_ref = rest[0] if gate_from is not None else None
    o_ref, xb_ref = rest[-2:]
    j = pl.program_id(1)

    @pl.when(j == 0)
    def _():
        xb_ref[...] = x_ref[...].astype(BF16)

    tm, tn = o_ref.shape
    slabs = [slice(r * tm // PROJ_SLABS, (r + 1) * tm // PROJ_SLABS) for r in range(PROJ_SLABS)]
    ys = [_dot(xb_ref[rs, :], w_ref[...]) for rs in slabs]

    if n_rope > 0:
        rotary = j < n_rope
        sc = jnp.where(j < n_scale, scale, 1.0).astype(F32)
        for rs, y in zip(slabs, ys):
            cos = jnp.where(rotary, cos_ref[rs, :], 1.0)
            sin = jnp.where(rotary, sin_ref[rs, :], 0.0)
            for hh in range(tn // LANES):
                cs = slice(hh * LANES, (hh + 1) * LANES)
                t = y[:, cs]
                rot = pltpu.roll(t, LANES // 2, axis=1)
                val = (t * cos + rot * sin) * sc
                if gate_from is not None:
                    val = jnp.where(j >= gate_from, jax.nn.sigmoid(val + bias_ref[:, cs]), val)
                o_ref[rs, cs] = val.astype(o_ref.dtype)
    else:
        for rs, y in zip(slabs, ys):
            o_ref[rs, :] = y.astype(o_ref.dtype)


def proj(x, w, cos, sin, *, n_rope=0, n_scale=0, scale=1.0, out_dtype=F32, tm=1024, tn=512,
         gate_bias=None, gate_from=None):
    m, d = x.shape
    n = w.shape[1]
    assert (gate_bias is None) == (gate_from is None) and (gate_from is None or n_rope > 0)
    kern = functools.partial(_proj_kernel, n_rope=n_rope, n_scale=n_scale, scale=scale, gate_from=gate_from)
    in_specs = [
        pl.BlockSpec((tm, d), lambda i, j: (i, 0)),
        pl.BlockSpec((d, tn), lambda i, j: (0, j)),
        pl.BlockSpec((tm, LANES), lambda i, j: (i, 0)),
        pl.BlockSpec((tm, LANES), lambda i, j: (i, 0)),
    ]
    args = [x, w, cos, sin]
    if gate_bias is not None:
        in_specs.append(pl.BlockSpec((1, tn), lambda i, j: (0, j)))
        args.append(gate_bias)
    return pl.pallas_call(
        kern,
        grid=(m // tm, n // tn),
        in_specs=in_specs,
        out_specs=pl.BlockSpec((tm, tn), lambda i, j: (i, j)),
        out_shape=jax.ShapeDtypeStruct((m, n), out_dtype),
        scratch_shapes=[pltpu.VMEM((tm, d), BF16)],
        compiler_params=_params(("parallel", "arbitrary")),
        name="proj",
    )(*args)


def _rope_table_kernel(pos_ref, inv_ref, sign_ref, cos_ref, sin_ref):
    ang = pos_ref[...] * inv_ref[...]
    cos_ref[...] = jnp.cos(ang)
    sin_ref[...] = jnp.sin(ang) * sign_ref[...]


def rope_tables(pos_f32, *, tm=2048):
    m = pos_f32.shape[0]
    hd = NSA_HEAD_DIM
    inv = ROPE_THETA ** (-jnp.arange(0, hd, 2, dtype=F32) / hd)
    inv_full = jnp.concatenate([inv, inv])[None, :]
    sign = jnp.concatenate([-jnp.ones((hd // 2,), F32), jnp.ones((hd // 2,), F32)])[None, :]
    return pl.pallas_call(
        _rope_table_kernel,
        grid=(m // tm,),
        in_specs=[
            pl.BlockSpec((tm, 1), lambda i: (i, 0)),
            pl.BlockSpec((1, hd), lambda i: (0, 0)),
            pl.BlockSpec((1, hd), lambda i: (0, 0)),
        ],
        out_specs=[pl.BlockSpec((tm, hd), lambda i: (i, 0))] * 2,
        out_shape=[jax.ShapeDtypeStruct((m, hd), F32)] * 2,
        compiler_params=_params(("parallel",)),
        name="rope_tables",
    )(pos_f32, inv_full, sign)


def _out_ln_kernel(a_ref, w_ref, x_ref, g_ref, b_ref, o_ref):
    tm = a_ref.shape[0]
    slabs = [slice(r * tm // OUT_SLABS, (r + 1) * tm // OUT_SLABS) for r in range(OUT_SLABS)]
    ys = [_dot(a_ref[rs, :], w_ref[...]) for rs in slabs]
    for rs, y in zip(slabs, ys):
        o_ref[rs, :] = _layer_norm(DEEPNORM_ALPHA * x_ref[rs, :] + y, g_ref[...], b_ref[...])


def out_ln(a, w, x, g, b, *, tm=512):
    m, d = x.shape
    k = a.shape[1]
    return pl.pallas_call(
        _out_ln_kernel,
        grid=(m // tm,),
        in_specs=[
            pl.BlockSpec((tm, k), lambda i: (i, 0)),
            pl.BlockSpec((k, d), lambda i: (0, 0)),
            pl.BlockSpec((tm, d), lambda i: (i, 0)),
            pl.BlockSpec((1, d), lambda i: (0, 0)),
            pl.BlockSpec((1, d), lambda i: (0, 0)),
        ],
        out_specs=pl.BlockSpec((tm, d), lambda i: (i, 0)),
        out_shape=jax.ShapeDtypeStruct((m, d), F32),
        compiler_params=_params(("parallel",)),
        name="out_ln",
    )(a, w, x, g, b)


def _gelu_tanh(x):
    c = float(np.sqrt(2.0 / np.pi))
    return 0.5 * x * (1.0 + jnp.tanh(c * (x + 0.044715 * (x * x * x))))


def _cmp_kernel(kc_ref, vc_ref, pos_ref, w1_ref, w2_ref, ko_ref, vo_ref):
    seq, hd = kc_ref.shape
    nrow = seq // CMP_STRIDE
    half = CMP_STRIDE * hd
    for idx, (src, dst) in enumerate(((kc_ref, ko_ref), (vc_ref, vo_ref))):
        first, second = [], []
        for l in range(CMP_STRIDE):
            xl = src[pl.ds(l, nrow, stride=CMP_STRIDE), :]
            first.append(xl + pos_ref[idx, l:l + 1, :])
            second.append(xl + pos_ref[idx, CMP_STRIDE + l:CMP_STRIDE + l + 1, :])
        ya = _dot(jnp.concatenate(first, axis=1).astype(BF16), w1_ref[idx, 0:half, :])
        yb = _dot(jnp.concatenate(second, axis=1).astype(BF16), w1_ref[idx, half:2 * half, :])
        h = _gelu_tanh(ya + pltpu.roll(yb, nrow - 1, axis=0))
        out = _dot(h.astype(BF16), w2_ref[idx])
        if idx == 0:
            dst[0, 0] = out.astype(BF16)
        else:
            dst[0, 0] = out.astype(BF16).astype(F32).T.astype(BF16)


def nsa_compress(pb, pos, w1, w2, *, bsz, seq):
    g = NSA_KV_GROUPS
    hd = NSA_HEAD_DIM
    nrow = seq // CMP_STRIDE
    oblk = pl.BlockSpec((1, 1, nrow, hd), lambda b, gg: (b, gg, 0, 0))
    return pl.pallas_call(
        _cmp_kernel,
        grid=(bsz, g),
        in_specs=[
            pl.BlockSpec((seq, hd), lambda b, gg: (b, gg)),
            pl.BlockSpec((seq, hd), lambda b, gg: (b, g + gg)),
            pl.BlockSpec(pos.shape, lambda b, gg: (0, 0, 0)),
            pl.BlockSpec(w1.shape, lambda b, gg: (0, 0, 0)),
            pl.BlockSpec(w2.shape, lambda b, gg: (0, 0, 0)),
        ],
        out_specs=[oblk, oblk],
        out_shape=[jax.ShapeDtypeStruct((bsz, g, nrow, hd), BF16)] * 2,
        compiler_params=_params(("parallel", "parallel")),
        name="nsa_compress",
    )(pb, pb, pos, w1, w2)


def _nsa_attn_kernel(q_ref, ks_ref, kw_ref, vs_ref, vw_ref, kc_ref, vct_ref, gate_ref,
                     ovt_ref, hot_ref, o_ref, ksa, vst, kwp, vwt, pbuf, *, tq, tk, seq):
    i = pl.program_id(2)
    hd = NSA_HEAD_DIM
    rep = NSA_REP
    cols = rep * tq
    nblk = seq // tq
    wblk = WINDOW // tq

    @pl.when(i == 0)
    def _():
        ksa[:, 0:hd] = ks_ref[...]
        ksa[:, hd:2 * hd] = hot_ref[...]
        kwp[0:WINDOW, :] = jnp.zeros((WINDOW, hd), BF16)
        kwp[WINDOW:WINDOW + seq, :] = kw_ref[...]
        for blk in range(wblk):
            vwt[blk] = jnp.zeros((hd, tq), BF16)
        for blk in range(nblk):
            rs = slice(blk * tq, (blk + 1) * tq)
            vst[blk] = vs_ref[rs, :].astype(F32).T.astype(BF16)
            vwt[wblk + blk] = vw_ref[rs, :].astype(F32).T.astype(BF16)

    t0 = i * tq
    blk0 = i
    q_t = jnp.concatenate([q_ref[:, r * hd:(r + 1) * hd].astype(F32).T for r in range(rep)],
                          axis=1).astype(BF16)

    def lane_t(shape):
        return t0 + (lax.broadcasted_iota(jnp.int32, shape, 1) & (tq - 1))

    def softmax_cols(s):
        mx = jnp.max(s, axis=0, keepdims=True)
        p = jnp.exp2(s - mx)
        return mx, p, jnp.sum(p, axis=0, keepdims=True)

    ku = lax.broadcasted_iota(jnp.int32, (tq, cols), 0)
    qu = lax.broadcasted_iota(jnp.int32, (tq, cols), 1) & (tq - 1)
    bias_diag = jnp.where(ku <= qu, 0.0, NEG)
    bias_first = jnp.where(ku > qu, 0.0, NEG)

    ncmp = kc_ref.shape[2]
    band = WINDOW + tq
    dstart = pl.multiple_of(t0, tq)
    s = _dot(kc_ref[0, 0], q_t)
    diag_forced = tq <= 2 * SEL_BLOCK
    if diag_forced:
        sd = _dot(ksa[pl.ds(dstart, tq), 0:hd], q_t) + bias_diag
    sw = _dot(kwp[pl.ds(dstart, band), :], q_t)

    n_idx = lax.broadcasted_iota(jnp.int32, (ncmp, cols), 0)
    valid = n_idx * CMP_STRIDE + (CMP_BLOCK - 1) <= lane_t((ncmp, cols))
    sm = jnp.where(valid, s, NEG)
    mx = jnp.max(sm, axis=0, keepdims=True)
    p = jnp.where(valid, jnp.exp2(sm - mx), 0.0)
    den = jnp.sum(p, axis=0, keepdims=True)
    p = p * (1.0 / jnp.where(den > 0.0, den, 1.0))
    o_cmp = _dot(vct_ref[0, 0], p.astype(BF16))

    pieces = []
    for jb in range(wblk + 1):
        piece = sw[jb * tq:(jb + 1) * tq]
        if jb == wblk:
            piece = piece + bias_diag
        else:
            piece = piece + jnp.where(t0 - WINDOW + jb * tq < 0, NEG, 0.0).astype(F32)
            if jb == 0:
                piece = piece + bias_first
        pieces.append(piece)
    _, pw, l_w = softmax_cols(jnp.concatenate(pieces, axis=0))
    vw_t = jnp.concatenate([vwt[blk0 + k] for k in range(wblk + 1)], axis=1)
    o_win = _dot(vw_t, pw.astype(BF16)) * (1.0 / l_w)

    psum = p[:, 0:tq]
    for r in range(1, rep):
        psum = psum + p[:, r * tq:(r + 1) * tq]
    hi, mid, lo = _split3(psum)
    ovt = ovt_ref[...]
    imp = _dot(ovt, hi) + _dot(ovt, mid) + _dot(ovt, lo)
    n_sel = seq // SEL_BLOCK
    b_idx = lax.broadcasted_iota(jnp.int32, (n_sel, tq), 0)
    cur = (t0 + lax.broadcasted_iota(jnp.int32, (n_sel, tq), 1)) // SEL_BLOCK
    forced = (b_idx == 0) | (b_idx == cur) | (b_idx == cur - 1)
    score = jnp.where(b_idx <= cur, imp[0:n_sel] + jnp.where(forced, FORCE, 0.0), NEG)
    rank = jnp.zeros((n_sel, tq), F32)
    for mp in range(n_sel):
        row = score[mp:mp + 1, :]
        beats = (row > score) | ((row == score) & (b_idx > mp))
        rank = rank + jnp.where(beats, 1.0, 0.0)
    selected = rank < float(SEL_TOPK)
    before_tile = b_idx < t0 // SEL_BLOCK

    def augment(keep):
        bias_blk = jnp.where(keep, 0.0, NEG)
        bias_blk = jnp.concatenate([bias_blk, jnp.zeros((hd - n_sel, tq), F32)], axis=0).astype(BF16)
        return jnp.concatenate([q_t, jnp.concatenate([bias_blk] * rep, axis=1)], axis=0)

    q_aug = augment(selected & before_tile)
    if not diag_forced:
        sd = _dot(ksa[pl.ds(dstart, tq), :], augment(selected & jnp.logical_not(before_tile))) + bias_diag

    m_i, pd, l_i = softmax_cols(sd)
    acc = _dot(vst[blk0], pd.astype(BF16))
    per = tk // tq
    pbuf[...] = jnp.zeros_like(pbuf)

    def v_chunk(c):
        return jnp.concatenate([vst[c * per + k] for k in range(per)], axis=1)

    n_chunks = (t0 + tk - 1) // tk
    last = jnp.maximum(n_chunks - 1, 0)

    def sel_body(c, carry):
        m_i, l_i, acc = carry
        off = pl.multiple_of(c * tk, tk)
        sc = _dot(ksa[pl.ds(off, tk), :], q_aug)
        pv = _dot(v_chunk(jnp.maximum(c - 1, 0)), pbuf[...])
        m_new = jnp.maximum(m_i, jnp.max(sc, axis=0, keepdims=True))
        pp = jnp.exp2(sc - m_new)
        pbuf[...] = pp.astype(BF16)
        alpha = jnp.exp2(m_i - m_new)
        l_new = alpha * l_i + jnp.sum(pp, axis=0, keepdims=True)
        return m_new, l_new, alpha * (acc + pv)

    _, l_s, acc_s = lax.fori_loop(0, n_chunks, sel_body, (m_i, l_i, acc))
    acc_s = acc_s + _dot(v_chunk(last), pbuf[...])
    o_sel = acc_s * (1.0 / l_s)

    g_t = gate_ref[...].T
    for r in range(rep):
        cs = slice(r * tq, (r + 1) * tq)
        o_r = (g_t[r:r + 1, :] * o_cmp[:, cs]
               + g_t[rep + r:rep + r + 1, :] * o_sel[:, cs]
               + g_t[2 * rep + r:2 * rep + r + 1, :] * o_win[:, cs])
        o_ref[:, r * hd:(r + 1) * hd] = o_r.T.astype(BF16)


def nsa_attention(pa, kcmp, vcmp_t, gates, overlap_t, onehot, *, bsz, seq, gate_col0, tq=512, tk=512):
    m = pa.shape[0]
    hd = NSA_HEAD_DIM
    g = NSA_KV_GROUPS
    nq = seq // tq
    gw = NSA_REP * hd
    c_ks = NSA_Q // hd
    c_kw = (NSA_Q + NSA_KV) // hd
    c_vs = (NSA_Q + 2 * NSA_KV) // hd
    c_vw = (NSA_Q + 3 * NSA_KV) // hd
    ncmp = kcmp.shape[2]

    def kv_spec(c0):
        return pl.BlockSpec((seq, hd), lambda b, gg, i: (b, c0 + gg))

    kern = functools.partial(_nsa_attn_kernel, tq=tq, tk=tk, seq=seq)
    return pl.pallas_call(
        kern,
        grid=(bsz, g, nq),
        in_specs=[
            pl.BlockSpec((tq, gw), lambda b, gg, i: (b * nq + i, gg)),
            kv_spec(c_ks), kv_spec(c_kw), kv_spec(c_vs), kv_spec(c_vw),
            pl.BlockSpec((1, 1, ncmp, hd), lambda b, gg, i: (b, gg, 0, 0)),
            pl.BlockSpec((1, 1, hd, ncmp), lambda b, gg, i: (b, gg, 0, 0)),
            pl.BlockSpec((tq, LANES), lambda b, gg, i: (b * nq + i, gate_col0 + gg)),
            pl.BlockSpec(overlap_t.shape, lambda b, gg, i: (0, 0)),
            pl.BlockSpec(onehot.shape, lambda b, gg, i: (0, 0)),
        ],
        out_specs=pl.BlockSpec((tq, gw), lambda b, gg, i: (b * nq + i, gg)),
        out_shape=jax.ShapeDtypeStruct((m, NSA_Q), BF16),
        scratch_shapes=[
            pltpu.VMEM((seq, 2 * hd), BF16),
            pltpu.VMEM((seq // tq, hd, tq), BF16),
            pltpu.VMEM((seq + WINDOW, hd), BF16),
            pltpu.VMEM((seq // tq + WINDOW // tq, hd, tq), BF16),
            pltpu.VMEM((tk, NSA_REP * tq), BF16),
        ],
        compiler_params=_params(("parallel", "parallel", "arbitrary")),
        name="nsa_attention",
    )(pa, pa, pa, pa, pa, kcmp, vcmp_t, gates, overlap_t, onehot)


def _nsa_constants(seq):
    mm = np.arange(LANES)[:, None]
    n = np.arange(LANES)[None, :]
    c_start = n * CMP_STRIDE
    s_start = mm * SEL_BLOCK
    n_cmp = (seq - CMP_BLOCK) // CMP_STRIDE + 1
    overlap_t = ((c_start < s_start + SEL_BLOCK) & (c_start + CMP_BLOCK > s_start)
                 & (n < n_cmp) & (mm < seq // SEL_BLOCK))
    onehot = (np.arange(seq)[:, None] // SEL_BLOCK) == np.arange(LANES)[None, :]
    return (jnp.asarray(overlap_t.astype(np.float32), dtype=BF16),
            jnp.asarray(onehot.astype(np.float32), dtype=BF16))


def nsa_mixer_ln(x, cos, sin, w_in, gate_b, cmp_pos, cmp_w1, cmp_w2, w_out, g, b, *, bsz, seq):
    hd = NSA_HEAD_DIM
    grp = NSA_KV_GROUPS
    rep = NSA_REP
    kv = NSA_KV

    def wcols(order):
        return jnp.concatenate([w_in[:, NSA_Q + i * kv:NSA_Q + (i + 1) * kv] for i in order], axis=1)

    w_a = jnp.concatenate([w_in[:, :NSA_Q], wcols((2, 4, 3, 5))], axis=1).astype(BF16)
    pa = proj(x, w_a, cos, sin, n_rope=(NSA_Q + 2 * kv) // PROJ_WIDE, n_scale=NSA_Q // PROJ_WIDE,
              scale=hd ** -0.5 * LOG2E, out_dtype=BF16, tn=PROJ_WIDE)
    w_gl = w_in[:, NSA_Q + 6 * kv:].reshape(-1, 3, grp, rep).transpose(0, 2, 1, 3).reshape(-1, grp, 3 * rep)
    w_gl = jnp.pad(w_gl, ((0, 0), (0, 0), (0, LANES - 3 * rep))).reshape(-1, grp * LANES)
    b_gl = gate_b.reshape(3, grp, rep).transpose(1, 0, 2).reshape(grp, 3 * rep)
    b_gl = jnp.pad(b_gl, ((0, 0), (0, LANES - 3 * rep))).reshape(1, grp * LANES)
    w_b = jnp.concatenate([wcols((0, 1)), w_gl], axis=1).astype(BF16)
    bias_b = jnp.concatenate([jnp.zeros((1, 2 * kv), F32), b_gl], axis=1)
    pb = proj(x, w_b, cos, sin, n_rope=1, tn=kv, gate_bias=bias_b, gate_from=2)

    kcmp, vcmp_t = nsa_compress(pb, cmp_pos, cmp_w1.astype(BF16), cmp_w2.astype(BF16), bsz=bsz, seq=seq)
    overlap_t, onehot = _nsa_constants(seq)
    o = nsa_attention(pa, kcmp, vcmp_t, pb, overlap_t, onehot, bsz=bsz, seq=seq,
                      gate_col0=2 * kv // LANES)
    return out_ln(o, w_out.astype(BF16), x, g, b)


def _conv_kernel(x_ref, wba_ref, wca_ref, wha_ref, cwa_ref, woa_ref, wbb_ref, wcb_ref, whb_ref, cwb_ref,
                 wob_ref, g_ref, b_ref, o_ref, xb_ref, acc_ref, tail_ref, *, tiles_per_seq):
    i = pl.program_id(0)
    j = pl.program_id(1)
    seq_start = (i % tiles_per_seq) == 0

    def channel_tile(xb, t, wb_ref, wc_ref, wh_ref, cw_ref, wo_ref):
        bg = _dot(xb, wb_ref[...])
        u = _dot(xb, wc_ref[...]) * _dot(xb, wh_ref[...])
        tm = u.shape[0]
        prev = jnp.where(seq_start, 0.0, tail_ref[t])
        tail_ref[t] = u[tm - 8:tm, :]
        rid = lax.broadcasted_iota(jnp.int32, u.shape, 0)
        u1 = jnp.where(rid == 0, prev[7:8, :], pltpu.roll(u, 1, axis=0))
        u2 = pltpu.roll(u, 2, axis=0)
        u2 = jnp.where(rid == 0, prev[6:7, :], jnp.where(rid == 1, prev[7:8, :], u2))
        cw = cw_ref[...]
        y = cw[0:1, :] * u2 + cw[1:2, :] * u1 + cw[2:3, :] * u
        return _dot((bg * y).astype(BF16), wo_ref[...])

    def pair(xb):
        return (channel_tile(xb, 2 * j, wba_ref, wca_ref, wha_ref, cwa_ref, woa_ref)
                + channel_tile(xb, 2 * j + 1, wbb_ref, wcb_ref, whb_ref, cwb_ref, wob_ref))

    @pl.when(j == 0)
    def _():
        xb = x_ref[...].astype(BF16)
        xb_ref[...] = xb
        acc_ref[...] = pair(xb)

    @pl.when(j > 0)
    def _():
        acc_ref[...] += pair(xb_ref[...])

    @pl.when(j == pl.num_programs(1) - 1)
    def _():
        yy = DEEPNORM_ALPHA * x_ref[...] + acc_ref[...]
        o_ref[...] = _layer_norm(yy, g_ref[...], b_ref[...])


def conv_mixer_ln(x, w_in, conv_w, w_out, g, b, *, seq, tm=512, tn=256):
    m, d = x.shape
    nt = d // tn
    assert nt % 2 == 0
    kern = functools.partial(_conv_kernel, tiles_per_seq=seq // tm)

    def tile_specs(off):
        return [
            pl.BlockSpec((d, tn), lambda i, j: (0, 2 * j + off)),
            pl.BlockSpec((d, tn), lambda i, j: (0, 2 * j + off + nt)),
            pl.BlockSpec((d, tn), lambda i, j: (0, 2 * j + off + 2 * nt)),
            pl.BlockSpec((CONV_WIDTH, tn), lambda i, j: (0, 2 * j + off)),
            pl.BlockSpec((tn, d), lambda i, j: (2 * j + off, 0)),
        ]

    return pl.pallas_call(
        kern,
        grid=(m // tm, nt // 2),
        in_specs=[pl.BlockSpec((tm, d), lambda i, j: (i, 0))] + tile_specs(0) + tile_specs(1) + [
            pl.BlockSpec((1, d), lambda i, j: (0, 0)),
            pl.BlockSpec((1, d), lambda i, j: (0, 0)),
        ],
        out_specs=pl.BlockSpec((tm, d), lambda i, j: (i, 0)),
        out_shape=jax.ShapeDtypeStruct((m, d), F32),
        scratch_shapes=[pltpu.VMEM((tm, d), BF16), pltpu.VMEM((tm, d), F32),
                        pltpu.VMEM((nt, 8, tn), F32)],
        compiler_params=_params(("arbitrary", "arbitrary")),
        name="conv_mixer_ln",
    )(x, w_in, w_in, w_in, conv_w, w_out, w_in, w_in, w_in, conv_w, w_out, g, b)


def _log_sigmoid(z):
    return jnp.minimum(z, 0.0) - jnp.log1p(jnp.exp(-jnp.abs(z)))


def _gla_kernel(q_ref, k_ref, v_ref, r_ref, a_ref, wa_ref, ba_ref, ng_ref, o_ref, st_ref, upd_ref):
    i = pl.program_id(2)
    c_sz = GLA_CHUNK
    dk, dv = GLA_HEAD_K, GLA_HEAD_V
    n_c = q_ref.shape[0] // c_sz
    heads = range(GLA_HEADS_PER_STEP)

    @pl.when(i == 0)
    def _():
        st_ref[...] = jnp.zeros_like(st_ref)

    rr = lax.broadcasted_iota(jnp.int32, (c_sz, c_sz), 0)
    cc = lax.broadcasted_iota(jnp.int32, (c_sz, c_sz), 1)
    causal = cc <= rr
    tril = jnp.where(causal, 1.0, 0.0).astype(BF16)
    chunks = [slice(c * c_sz, (c + 1) * c_sz) for c in range(n_c)]

    z = _dot(a_ref[...].astype(BF16), wa_ref[...]) + ba_ref[...]
    gk = _log_sigmoid(z) / GLA_GATE_NORM
    hi, mid, lo = _split3(gk)
    bcum = jnp.concatenate(
        [_dot(tril, hi[rs]) + _dot(tril, mid[rs]) + _dot(tril, lo[rs]) for rs in chunks], axis=0)
    b_last = [bcum[rs][c_sz - 1:c_sz, :] for rs in chunks]
    b_last_rows = jnp.concatenate([jnp.broadcast_to(bl, (c_sz, bl.shape[1])) for bl in b_last], axis=0)
    qd = (q_ref[...] * (dk ** -0.5) * jnp.exp(bcum)).astype(BF16)
    k = k_ref[...]
    kd = (k * jnp.exp(-bcum)).astype(BF16)
    kl = (k * jnp.exp(b_last_rows - bcum)).astype(BF16)
    v = v_ref[...].astype(BF16)

    o_intra = []
    for hh in heads:
        ks, vs = slice(hh * dk, (hh + 1) * dk), slice(hh * dv, (hh + 1) * dv)
        outs = []
        for c, rs in enumerate(chunks):
            att = jnp.where(causal, _dot_nt(qd[rs, ks], kd[rs, ks]), 0.0).astype(BF16)
            outs.append(_dot(att, v[rs, vs]))
            upd_ref[hh, c] = _dot_tn(v[rs, vs], kl[rs, ks])
        o_intra.append(outs)

    st = [st_ref[hh] for hh in heads]
    ng = ng_ref[...]
    for c, rs in enumerate(chunks):
        for hh in heads:
            ks, vs = slice(hh * dk, (hh + 1) * dk), slice(hh * dv, (hh + 1) * dv)
            o = o_intra[hh][c] + _dot_nt(qd[rs, ks], st[hh].astype(BF16))
            st[hh] = st[hh] * jnp.exp(b_last[c][:, ks]) + upd_ref[hh, c]
            o = o * lax.rsqrt(jnp.mean(o * o, axis=-1, keepdims=True) + LN_EPS) * ng
            o_ref[rs, vs] = (o * _silu(r_ref[rs, vs])).astype(BF16)
    for hh in heads:
        st_ref[hh] = st[hh]


def gla_scan(p, a, wa, ba, ng, *, bsz, seq, tt=512):
    m = p.shape[0]
    dk, dv = GLA_HEAD_K, GLA_HEAD_V
    hp = GLA_HEADS_PER_STEP
    nt = seq // tt
    c_k = GLA_KEY_DIM // (hp * dk)
    c_v = 2 * GLA_KEY_DIM // (hp * dv)
    c_r = (2 * GLA_KEY_DIM + GLA_VAL_DIM) // (hp * dv)
    row = lambda b, h, i: b * nt + i
    return pl.pallas_call(
        _gla_kernel,
        grid=(bsz, GLA_HEADS // hp, nt),
        in_specs=[
            pl.BlockSpec((tt, hp * dk), lambda b, h, i: (row(b, h, i), h)),
            pl.BlockSpec((tt, hp * dk), lambda b, h, i: (row(b, h, i), c_k + h)),
            pl.BlockSpec((tt, hp * dv), lambda b, h, i: (row(b, h, i), c_v + h)),
            pl.BlockSpec((tt, hp * dv), lambda b, h, i: (row(b, h, i), c_r + h)),
            pl.BlockSpec((tt, LANES), lambda b, h, i: (row(b, h, i), 0)),
            pl.BlockSpec((LANES, hp * dk), lambda b, h, i: (0, h)),
            pl.BlockSpec((1, hp * dk), lambda b, h, i: (0, h)),
            pl.BlockSpec((1, dv), lambda b, h, i: (0, 0)),
        ],
        out_specs=pl.BlockSpec((tt, hp * dv), lambda b, h, i: (row(b, h, i), h)),
        out_shape=jax.ShapeDtypeStruct((m, GLA_VAL_DIM), BF16),
        scratch_shapes=[pltpu.VMEM((hp, dv, dk), F32),
                        pltpu.VMEM((hp, tt // GLA_CHUNK, dv, dk), F32)],
        compiler_params=_params(("parallel", "parallel", "arbitrary")),
        name="gla_scan",
    )(p, p, p, p, a, wa, ba, ng)


def gla_mixer_ln(x, w_in, w_a2, b_a, norm_g, w_out, g, b, *, bsz, seq):
    n_main = 2 * GLA_KEY_DIM + 2 * GLA_VAL_DIM
    dummy = jnp.zeros((x.shape[0], LANES), F32)
    p = proj(x, w_in[:, :n_main].astype(BF16), dummy, dummy, tn=PROJ_WIDE)
    w_a = jnp.pad(w_in[:, n_main:], ((0, 0), (0, LANES - GLA_GATE_RANK))).astype(BF16)
    a = proj(x, w_a, dummy, dummy, tn=LANES)
    wa2 = jnp.pad(w_a2, ((0, LANES - GLA_GATE_RANK), (0, 0))).astype(BF16)
    o = gla_scan(p, a, wa2, b_a[None, :], norm_g[None, :], bsz=bsz, seq=seq)
    return out_ln(o, w_out.astype(BF16), x, g, b)


def kernel(x, positions, ln_g, ln_b, ffn_w_in, ffn_w_out, nsa_w_in, nsa_gate_b, nsa_cmp_pos,
           nsa_cmp_w1, nsa_cmp_w2, nsa_w_out, conv_w_in, conv_w, conv_w_out, gla_w_in, gla_w_a2,
           gla_b_a, gla_norm_g, gla_w_out):
    bsz, seq, d = x.shape
    m = bsz * seq
    h = x.reshape(m, d)
    cos, sin = rope_tables(positions.reshape(m, 1).astype(F32))
    ffn_in = ffn_w_in.astype(BF16)
    ffn_out = ffn_w_out.astype(BF16)
    for i in range(DEPTH):
        lg = lambda k: ln_g[i, k][None, :]
        lb = lambda k: ln_b[i, k][None, :]
        h = ffn_ln(h, ffn_in, ffn_out, lg(0), lb(0), i, 0)
        kind, j = i % N_MIXERS, i // N_MIXERS
        if kind == 0:
            h = nsa_mixer_ln(h, cos, sin, nsa_w_in[j], nsa_gate_b[j], nsa_cmp_pos[j], nsa_cmp_w1[j],
                             nsa_cmp_w2[j], nsa_w_out[j], lg(1), lb(1), bsz=bsz, seq=seq)
        elif kind == 1:
            h = conv_mixer_ln(h, conv_w_in[j].astype(BF16), conv_w[j], conv_w_out[j].astype(BF16),
                              lg(1), lb(1), seq=seq)
        else:
            h = gla_mixer_ln(h, gla_w_in[j], gla_w_a2[j], gla_b_a[j], gla_norm_g[j], gla_w_out[j],
                             lg(1), lb(1), bsz=bsz, seq=seq)
        h = ffn_ln(h, ffn_in, ffn_out, lg(2), lb(2), i, 1)
    return h.reshape(bsz, seq, d)
```

```python
import functools

import numpy as np
import jax
import jax.numpy as jnp
from jax import lax
from jax.experimental import pallas as pl
from jax.experimental.pallas import tpu as pltpu

F32 = jnp.float32
BF16 = jnp.bfloat16

D_MODEL = 2048
DEPTH = 4
N_MIXERS = 3
DEEPNORM_ALPHA = (2.0 * DEPTH) ** 0.25
LN_EPS = 1e-5
MACARON_WEIGHT = 0.5
D_FF = 5632

NSA_HEADS = 16
NSA_KV_GROUPS = 4
NSA_REP = NSA_HEADS // NSA_KV_GROUPS
NSA_HEAD_DIM = D_MODEL // NSA_HEADS
NSA_Q = NSA_HEADS * NSA_HEAD_DIM
NSA_KV = NSA_KV_GROUPS * NSA_HEAD_DIM
CMP_BLOCK = 32
CMP_STRIDE = 16
SEL_BLOCK = 64
SEL_TOPK = 16
WINDOW = 512
ROPE_THETA = 10000.0
NEG = -1e30
FORCE = 1e3
LOG2E = float(np.log2(np.e))

CONV_WIDTH = 3

GLA_HEADS = 4
GLA_KEY_DIM = D_MODEL // 2
GLA_VAL_DIM = D_MODEL
GLA_HEAD_K = GLA_KEY_DIM // GLA_HEADS
GLA_HEAD_V = GLA_VAL_DIM // GLA_HEADS
GLA_GATE_RANK = 16
GLA_GATE_NORM = 16.0
GLA_CHUNK = 64
GLA_HEADS_PER_STEP = 2

LANES = 128
PROJ_SLABS = 4
OUT_SLABS = 2
PROJ_WIDE = 1024
VMEM_LIMIT = 56 * 1024 * 1024


def _params(sem):
    return pltpu.CompilerParams(dimension_semantics=sem, vmem_limit_bytes=VMEM_LIMIT)


def _layer_norm(y, g, b):
    mu = jnp.mean(y, axis=-1, keepdims=True)
    d = y - mu
    var = jnp.mean(d * d, axis=-1, keepdims=True)
    return d * lax.rsqrt(var + LN_EPS) * g + b


def _silu(h):
    return h * jax.nn.sigmoid(h)


def _dot(a, b):
    return jnp.dot(a, b, preferred_element_type=F32)


def _dot_nt(a, b):
    return lax.dot_general(a, b, (((1,), (1,)), ((), ())), preferred_element_type=F32)


def _dot_tn(a, b):
    return lax.dot_general(a, b, (((0,), (0,)), ((), ())), preferred_element_type=F32)


def _split3(x):
    hi = x.astype(BF16)
    r1 = x - hi.astype(F32)
    mid = r1.astype(BF16)
    lo = (r1 - mid.astype(F32)).astype(BF16)
    return hi, mid, lo


def _ffn_kernel(x_ref, wga_ref, wua_ref, woa_ref, wgb_ref, wub_ref, wob_ref, g_ref, b_ref, o_ref,
                xb_ref, acc_ref, *, n_ff_tiles):
    j = pl.program_id(1)

    def ff_tile(xb, wg_ref, wu_ref, wo_ref):
        h = _dot(xb, wg_ref[...])
        u = _dot(xb, wu_ref[...])
        return _dot((_silu(h) * u).astype(BF16), wo_ref[...])

    def ff_pair(xb):
        return ff_tile(xb, wga_ref, wua_ref, woa_ref) + ff_tile(xb, wgb_ref, wub_ref, wob_ref)

    first = j == 0
    paired = 2 * j + 1 < n_ff_tiles

    @pl.when(first)
    def _():
        xb = x_ref[...].astype(BF16)
        xb_ref[...] = xb
        acc_ref[...] = ff_pair(xb) + DEEPNORM_ALPHA * x_ref[...]

    @pl.when(jnp.logical_not(first) & paired)
    def _():
        acc_ref[...] += ff_pair(xb_ref[...])

    @pl.when(jnp.logical_not(first) & jnp.logical_not(paired))
    def _():
        acc_ref[...] += ff_tile(xb_ref[...], wga_ref, wua_ref, woa_ref)

    @pl.when(j == pl.num_programs(1) - 1)
    def _():
        o_ref[...] = _layer_norm(acc_ref[...], g_ref[...], b_ref[...])


def ffn_ln(x, w_in, w_out, g, b, layer, half, *, tm=512, tf=512):
    m, d = x.shape
    ff = w_out.shape[2]
    nt = ff // tf
    assert nt >= 2
    nj = (nt + 1) // 2

    def tile_a(j):
        return 2 * j

    def tile_b(j):
        return jnp.minimum(2 * j + 1, nt - 1)

    def w_specs(tile):
        return [
            pl.BlockSpec((None, None, d, tf), lambda i, j: (layer, half, 0, tile(j))),
            pl.BlockSpec((None, None, d, tf), lambda i, j: (layer, half, 0, tile(j) + nt)),
            pl.BlockSpec((None, None, tf, d), lambda i, j: (layer, half, tile(j), 0)),
        ]

    return pl.pallas_call(
        functools.partial(_ffn_kernel, n_ff_tiles=nt),
        grid=(m // tm, nj),
        in_specs=[pl.BlockSpec((tm, d), lambda i, j: (i, 0))] + w_specs(tile_a) + w_specs(tile_b) + [
            pl.BlockSpec((1, d), lambda i, j: (0, 0)),
            pl.BlockSpec((1, d), lambda i, j: (0, 0)),
        ],
        out_specs=pl.BlockSpec((tm, d), lambda i, j: (i, 0)),
        out_shape=jax.ShapeDtypeStruct((m, d), F32),
        scratch_shapes=[pltpu.VMEM((tm, d), BF16), pltpu.VMEM((tm, d), F32)],
        compiler_params=_params(("parallel", "arbitrary")),
        name="ffn_ln",
    )(x, w_in, w_in, w_out, w_in, w_in, w_out, g, b)


def _proj_kernel(x_ref, w_ref, cos_ref, sin_ref, *rest, n_rope, n_scale, scale, gate_from):
    bias_ref = rest[0] if gate_from is not None else None
    o_ref, xb_ref = rest[-2:]
    j = pl.program_id(1)

    @pl.when(j == 0)
    def _():
        xb_ref[...] = x_ref[...].astype(BF16)

    tm, tn = o_ref.shape
    slabs = [slice(r * tm // PROJ_SLABS, (r + 1) * tm // PROJ_SLABS) for r in range(PROJ_SLABS)]
    ys = [_dot(xb_ref[rs, :], w_ref[...]) for rs in slabs]

    if n_rope > 0:
        rotary = j < n_rope
        sc = jnp.where(j < n_scale, scale, 1.0).astype(F32)
        for rs, y in zip(slabs, ys):
            cos = jnp.where(rotary, cos_ref[rs, :], 1.0)
            sin = jnp.where(rotary, sin_ref[rs, :], 0.0)
            for hh in range(tn // LANES):
                cs = slice(hh * LANES, (hh + 1) * LANES)
                t = y[:, cs]
                rot = pltpu.roll(t, LANES // 2, axis=1)
                val = (t * cos + rot * sin) * sc
                if gate_from is not None:
                    val = jnp.where(j >= gate_from, jax.nn.sigmoid(val + bias_ref[:, cs]), val)
                o_ref[rs, cs] = val.astype(o_ref.dtype)
    else:
        for rs, y in zip(slabs, ys):
            o_ref[rs, :] = y.astype(o_ref.dtype)


def proj(x, w, cos, sin, *, n_rope=0, n_scale=0, scale=1.0, out_dtype=F32, tm=1024, tn=512,
         gate_bias=None, gate_from=None):
    m, d = x.shape
    n = w.shape[1]
    assert (gate_bias is None) == (gate_from is None) and (gate_from is None or n_rope > 0)
    kern = functools.partial(_proj_kernel, n_rope=n_rope, n_scale=n_scale, scale=scale, gate_from=gate_from)
    in_specs = [
        pl.BlockSpec((tm, d), lambda i, j: (i, 0)),
        pl.BlockSpec((d, tn), lambda i, j: (0, j)),
        pl.BlockSpec((tm, LANES), lambda i, j: (i, 0)),
        pl.BlockSpec((tm, LANES), lambda i, j: (i, 0)),
    ]
    args = [x, w, cos, sin]
    if gate_bias is not None:
        in_specs.append(pl.BlockSpec((1, tn), lambda i, j: (0, j)))
        args.append(gate_bias)
    return pl.pallas_call(
        kern,
        grid=(m // tm, n // tn),
        in_specs=in_specs,
        out_specs=pl.BlockSpec((tm, tn), lambda i, j: (i, j)),
        out_shape=jax.ShapeDtypeStruct((m, n), out_dtype),
        scratch_shapes=[pltpu.VMEM((tm, d), BF16)],
        compiler_params=_params(("parallel", "arbitrary")),
        name="proj",
    )(*args)


def _rope_table_kernel(pos_ref, inv_ref, sign_ref, cos_ref, sin_ref):
    ang = pos_ref[...] * inv_ref[...]
    cos_ref[...] = jnp.cos(ang)
    sin_ref[...] = jnp.sin(ang) * sign_ref[...]


def rope_tables(pos_f32, *, tm=2048):
    m = pos_f32.shape[0]
    hd = NSA_HEAD_DIM
    inv = ROPE_THETA ** (-jnp.arange(0, hd, 2, dtype=F32) / hd)
    inv_full = jnp.concatenate([inv, inv])[None, :]
    sign = jnp.concatenate([-jnp.ones((hd // 2,), F32), jnp.ones((hd // 2,), F32)])[None, :]
    return pl.pallas_call(
        _rope_table_kernel,
        grid=(m // tm,),
        in_specs=[
            pl.BlockSpec((tm, 1), lambda i: (i, 0)),
            pl.BlockSpec((1, hd), lambda i: (0, 0)),
            pl.BlockSpec((1, hd), lambda i: (0, 0)),
        ],
        out_specs=[pl.BlockSpec((tm, hd), lambda i: (i, 0))] * 2,
        out_shape=[jax.ShapeDtypeStruct((m, hd), F32)] * 2,
        compiler_params=_params(("parallel",)),
        name="rope_tables",
    )(pos_f32, inv_full, sign)


def _out_ln_kernel(a_ref, w_ref, x_ref, g_ref, b_ref, o_ref):
    tm = a_ref.shape[0]
    slabs = [slice(r * tm // OUT_SLABS, (r + 1) * tm // OUT_SLABS) for r in range(OUT_SLABS)]
    ys = [_dot(a_ref[rs, :], w_ref[...]) for rs in slabs]
    for rs, y in zip(slabs, ys):
        o_ref[rs, :] = _layer_norm(DEEPNORM_ALPHA * x_ref[rs, :] + y, g_ref[...], b_ref[...])


def out_ln(a, w, x, g, b, *, tm=512):
    m, d = x.shape
    k = a.shape[1]
    return pl.pallas_call(
        _out_ln_kernel,
        grid=(m // tm,),
        in_specs=[
            pl.BlockSpec((tm, k), lambda i: (i, 0)),
            pl.BlockSpec((k, d), lambda i: (0, 0)),
            pl.BlockSpec((tm, d), lambda i: (i, 0)),
            pl.BlockSpec((1, d), lambda i: (0, 0)),
            pl.BlockSpec((1, d), lambda i: (0, 0)),
        ],
        out_specs=pl.BlockSpec((tm, d), lambda i: (i, 0)),
        out_shape=jax.ShapeDtypeStruct((m, d), F32),
        compiler_params=_params(("parallel",)),
        name="out_ln",
    )(a, w, x, g, b)


def _gelu_tanh(x):
    c = float(np.sqrt(2.0 / np.pi))
    return 0.5 * x * (1.0 + jnp.tanh(c * (x + 0.044715 * (x * x * x))))


def _cmp_kernel(kc_ref, vc_ref, pos_ref, w1_ref, w2_ref, ko_ref, vo_ref):
    seq, hd = kc_ref.shape
    nrow = seq // CMP_STRIDE
    half = CMP_STRIDE * hd
    for idx, (src, dst) in enumerate(((kc_ref, ko_ref), (vc_ref, vo_ref))):
        first, second = [], []
        for l in range(CMP_STRIDE):
            xl = src[pl.ds(l, nrow, stride=CMP_STRIDE), :]
            first.append(xl + pos_ref[idx, l:l + 1, :])
            second.append(xl + pos_ref[idx, CMP_STRIDE + l:CMP_STRIDE + l + 1, :])
        ya = _dot(jnp.concatenate(first, axis=1).astype(BF16), w1_ref[idx, 0:half, :])
        yb = _dot(jnp.concatenate(second, axis=1).astype(BF16), w1_ref[idx, half:2 * half, :])
        h = _gelu_tanh(ya + pltpu.roll(yb, nrow - 1, axis=0))
        out = _dot(h.astype(BF16), w2_ref[idx])
        if idx == 0:
            dst[0, 0] = out.astype(BF16)
        else:
            dst[0, 0] = out.astype(BF16).astype(F32).T.astype(BF16)


def nsa_compress(pb, pos, w1, w2, *, bsz, seq):
    g = NSA_KV_GROUPS
    hd = NSA_HEAD_DIM
    nrow = seq // CMP_STRIDE
    oblk = pl.BlockSpec((1, 1, nrow, hd), lambda b, gg: (b, gg, 0, 0))
    return pl.pallas_call(
        _cmp_kernel,
        grid=(bsz, g),
        in_specs=[
            pl.BlockSpec((seq, hd), lambda b, gg: (b, gg)),
            pl.BlockSpec((seq, hd), lambda b, gg: (b, g + gg)),
            pl.BlockSpec(pos.shape, lambda b, gg: (0, 0, 0)),
            pl.BlockSpec(w1.shape, lambda b, gg: (0, 0, 0)),
            pl.BlockSpec(w2.shape, lambda b, gg: (0, 0, 0)),
        ],
        out_specs=[oblk, oblk],
        out_shape=[jax.ShapeDtypeStruct((bsz, g, nrow, hd), BF16)] * 2,
        compiler_params=_params(("parallel", "parallel")),
        name="nsa_compress",
    )(pb, pb, pos, w1, w2)


def _nsa_attn_kernel(q_ref, ks_ref, kw_ref, vs_ref, vw_ref, kc_ref, vct_ref, gate_ref,
                     ovt_ref, hot_ref, o_ref, ksa, vst, kwp, vwt, pbuf, *, tq, tk, seq):
    i = pl.program_id(2)
    hd = NSA_HEAD_DIM
    rep = NSA_REP
    cols = rep * tq
    nblk = seq // tq
    wblk = WINDOW // tq

    @pl.when(i == 0)
    def _():
        ksa[:, 0:hd] = ks_ref[...]
        ksa[:, hd:2 * hd] = hot_ref[...]
        kwp[0:WINDOW, :] = jnp.zeros((WINDOW, hd), BF16)
        kwp[WINDOW:WINDOW + seq, :] = kw_ref[...]
        for blk in range(wblk):
            vwt[blk] = jnp.zeros((hd, tq), BF16)
        for blk in range(nblk):
            rs = slice(blk * tq, (blk + 1) * tq)
            vst[blk] = vs_ref[rs, :].astype(F32).T.astype(BF16)
            vwt[wblk + blk] = vw_ref[rs, :].astype(F32).T.astype(BF16)

    t0 = i * tq
    blk0 = i
    q_t = jnp.concatenate([q_ref[:, r * hd:(r + 1) * hd].astype(F32).T for r in range(rep)],
                          axis=1).astype(BF16)

    def lane_t(shape):
        return t0 + (lax.broadcasted_iota(jnp.int32, shape, 1) & (tq - 1))

    def softmax_cols(s):
        mx = jnp.max(s, axis=0, keepdims=True)
        p = jnp.exp2(s - mx)
        return mx, p, jnp.sum(p, axis=0, keepdims=True)

    ku = lax.broadcasted_iota(jnp.int32, (tq, cols), 0)
    qu = lax.broadcasted_iota(jnp.int32, (tq, cols), 1) & (tq - 1)
    bias_diag = jnp.where(ku <= qu, 0.0, NEG)
    bias_first = jnp.where(ku > qu, 0.0, NEG)

    ncmp = kc_ref.shape[2]
    band = WINDOW + tq
    dstart = pl.multiple_of(t0, tq)
    s = _dot(kc_ref[0, 0], q_t)
    diag_forced = tq <= 2 * SEL_BLOCK
    if diag_forced:
        sd = _dot(ksa[pl.ds(dstart, tq), 0:hd], q_t) + bias_diag
    sw = _dot(kwp[pl.ds(dstart, band), :], q_t)

    n_idx = lax.broadcasted_iota(jnp.int32, (ncmp, cols), 0)
    valid = n_idx * CMP_STRIDE + (CMP_BLOCK - 1) <= lane_t((ncmp, cols))
    sm = jnp.where(valid, s, NEG)
    mx = jnp.max(sm, axis=0, keepdims=True)
    p = jnp.where(valid, jnp.exp2(sm - mx), 0.0)
    den = jnp.sum(p, axis=0, keepdims=True)
    p = p * (1.0 / jnp.where(den > 0.0, den, 1.0))
    o_cmp = _dot(vct_ref[0, 0], p.astype(BF16))

    pieces = []
    for jb in range(wblk + 1):
        piece = sw[jb * tq:(jb + 1) * tq]
        if jb == wblk:
            piece = piece + bias_diag
        else:
            piece = piece + jnp.where(t0 - WINDOW + jb * tq < 0, NEG, 0.0).astype(F32)
            if jb == 0:
                piece = piece + bias_first
        pieces.append(piece)
    _, pw, l_w = softmax_cols(jnp.concatenate(pieces, axis=0))
    vw_t = jnp.concatenate([vwt[blk0 + k] for k in range(wblk + 1)], axis=1)
    o_win = _dot(vw_t, pw.astype(BF16)) * (1.0 / l_w)

    psum = p[:, 0:tq]
    for r in range(1, rep):
        psum = psum + p[:, r * tq:(r + 1) * tq]
    hi, mid, lo = _split3(psum)
    ovt = ovt_ref[...]
    imp = _dot(ovt, hi) + _dot(ovt, mid) + _dot(ovt, lo)
    n_sel = seq // SEL_BLOCK
    b_idx = lax.broadcasted_iota(jnp.int32, (n_sel, tq), 0)
    cur = (t0 + lax.broadcasted_iota(jnp.int32, (n_sel, tq), 1)) // SEL_BLOCK
    forced = (b_idx == 0) | (b_idx == cur) | (b_idx == cur - 1)
    score = jnp.where(b_idx <= cur, imp[0:n_sel] + jnp.where(forced, FORCE, 0.0), NEG)
    rank = jnp.zeros((n_sel, tq), F32)
    for mp in range(n_sel):
        row = score[mp:mp + 1, :]
        beats = (row > score) | ((row == score) & (b_idx > mp))
        rank = rank + jnp.where(beats, 1.0, 0.0)
    selected = rank < float(SEL_TOPK)
    before_tile = b_idx < t0 // SEL_BLOCK

    def augment(keep):
        bias_blk = jnp.where(keep, 0.0, NEG)
        bias_blk = jnp.concatenate([bias_blk, jnp.zeros((hd - n_sel, tq), F32)], axis=0).astype(BF16)
        return jnp.concatenate([q_t, jnp.concatenate([bias_blk] * rep, axis=1)], axis=0)

    q_aug = augment(selected & before_tile)
    if not diag_forced:
        sd = _dot(ksa[pl.ds(dstart, tq), :], augment(selected & jnp.logical_not(before_tile))) + bias_diag

    m_i, pd, l_i = softmax_cols(sd)
    acc = _dot(vst[blk0], pd.astype(BF16))
    per = tk // tq
    pbuf[...] = jnp.zeros_like(pbuf)

    def v_chunk(c):
        return jnp.concatenate([vst[c * per + k] for k in range(per)], axis=1)

    n_chunks = (t0 + tk - 1) // tk
    last = jnp.maximum(n_chunks - 1, 0)

    def sel_body(c, carry):
        m_i, l_i, acc = carry
        off = pl.multiple_of(c * tk, tk)
        sc = _dot(ksa[pl.ds(off, tk), :], q_aug)
        pv = _dot(v_chunk(jnp.maximum(c - 1, 0)), pbuf[...])
        m_new = jnp.maximum(m_i, jnp.max(sc, axis=0, keepdims=True))
        pp = jnp.exp2(sc - m_new)
        pbuf[...] = pp.astype(BF16)
        alpha = jnp.exp2(m_i - m_new)
        l_new = alpha * l_i + jnp.sum(pp, axis=0, keepdims=True)
        return m_new, l_new, alpha * (acc + pv)

    _, l_s, acc_s = lax.fori_loop(0, n_chunks, sel_body, (m_i, l_i, acc))
    acc_s = acc_s + _dot(v_chunk(last), pbuf[...])
    o_sel = acc_s * (1.0 / l_s)

    g_t = gate_ref[...].T
    for r in range(rep):
        cs = slice(r * tq, (r + 1) * tq)
        o_r = (g_t[r:r + 1, :] * o_cmp[:, cs]
               + g_t[rep + r:rep + r + 1, :] * o_sel[:, cs]
               + g_t[2 * rep + r:2 * rep + r + 1, :] * o_win[:, cs])
        o_ref[:, r * hd:(r + 1) * hd] = o_r.T.astype(BF16)


def nsa_attention(pa, kcmp, vcmp_t, gates, overlap_t, onehot, *, bsz, seq, gate_col0, tq=512, tk=512):
    m = pa.shape[0]
    hd = NSA_HEAD_DIM
    g = NSA_KV_GROUPS
    nq = seq // tq
    gw = NSA_REP * hd
    c_ks = NSA_Q // hd
    c_kw = (NSA_Q + NSA_KV) // hd
    c_vs = (NSA_Q + 2 * NSA_KV) // hd
    c_vw = (NSA_Q + 3 * NSA_KV) // hd
    ncmp = kcmp.shape[2]

    def kv_spec(c0):
        return pl.BlockSpec((seq, hd), lambda b, gg, i: (b, c0 + gg))

    kern = functools.partial(_nsa_attn_kernel, tq=tq, tk=tk, seq=seq)
    return pl.pallas_call(
        kern,
        grid=(bsz, g, nq),
        in_specs=[
            pl.BlockSpec((tq, gw), lambda b, gg, i: (b * nq + i, gg)),
            kv_spec(c_ks), kv_spec(c_kw), kv_spec(c_vs), kv_spec(c_vw),
            pl.BlockSpec((1, 1, ncmp, hd), lambda b, gg, i: (b, gg, 0, 0)),
            pl.BlockSpec((1, 1, hd, ncmp), lambda b, gg, i: (b, gg, 0, 0)),
            pl.BlockSpec((tq, LANES), lambda b, gg, i: (b * nq + i, gate_col0 + gg)),
            pl.BlockSpec(overlap_t.shape, lambda b, gg, i: (0, 0)),
            pl.BlockSpec(onehot.shape, lambda b, gg, i: (0, 0)),
        ],
        out_specs=pl.BlockSpec((tq, gw), lambda b, gg, i: (b * nq + i, gg)),
        out_shape=jax.ShapeDtypeStruct((m, NSA_Q), BF16),
        scratch_shapes=[
            pltpu.VMEM((seq, 2 * hd), BF16),
            pltpu.VMEM((seq // tq, hd, tq), BF16),
            pltpu.VMEM((seq + WINDOW, hd), BF16),
            pltpu.VMEM((seq // tq + WINDOW // tq, hd, tq), BF16),
            pltpu.VMEM((tk, NSA_REP * tq), BF16),
        ],
        compiler_params=_params(("parallel", "parallel", "arbitrary")),
        name="nsa_attention",
    )(pa, pa, pa, pa, pa, kcmp, vcmp_t, gates, overlap_t, onehot)


def _nsa_constants(seq):
    mm = np.arange(LANES)[:, None]
    n = np.arange(LANES)[None, :]
    c_start = n * CMP_STRIDE
    s_start = mm * SEL_BLOCK
    n_cmp = (seq - CMP_BLOCK) // CMP_STRIDE + 1
    overlap_t = ((c_start < s_start + SEL_BLOCK) & (c_start + CMP_BLOCK > s_start)
                 & (n < n_cmp) & (mm < seq // SEL_BLOCK))
    onehot = (np.arange(seq)[:, None] // SEL_BLOCK) == np.arange(LANES)[None, :]
    return (jnp.asarray(overlap_t.astype(np.float32), dtype=BF16),
            jnp.asarray(onehot.astype(np.float32), dtype=BF16))


def nsa_mixer_ln(x, cos, sin, w_in, gate_b, cmp_pos, cmp_w1, cmp_w2, w_out, g, b, *, bsz, seq):
    hd = NSA_HEAD_DIM
    grp = NSA_KV_GROUPS
    rep = NSA_REP
    kv = NSA_KV

    def wcols(order):
        return jnp.concatenate([w_in[:, NSA_Q + i * kv:NSA_Q + (i + 1) * kv] for i in order], axis=1)

    w_a = jnp.concatenate([w_in[:, :NSA_Q], wcols((2, 4, 3, 5))], axis=1).astype(BF16)
    pa = proj(x, w_a, cos, sin, n_rope=(NSA_Q + 2 * kv) // PROJ_WIDE, n_scale=NSA_Q // PROJ_WIDE,
              scale=hd ** -0.5 * LOG2E, out_dtype=BF16, tn=PROJ_WIDE)
    w_gl = w_in[:, NSA_Q + 6 * kv:].reshape(-1, 3, grp, rep).transpose(0, 2, 1, 3).reshape(-1, grp, 3 * rep)
    w_gl = jnp.pad(w_gl, ((0, 0), (0, 0), (0, LANES - 3 * rep))).reshape(-1, grp * LANES)
    b_gl = gate_b.reshape(3, grp, rep).transpose(1, 0, 2).reshape(grp, 3 * rep)
    b_gl = jnp.pad(b_gl, ((0, 0), (0, LANES - 3 * rep))).reshape(1, grp * LANES)
    w_b = jnp.concatenate([wcols((0, 1)), w_gl], axis=1).astype(BF16)
    bias_b = jnp.concatenate([jnp.zeros((1, 2 * kv), F32), b_gl], axis=1)
    pb = proj(x, w_b, cos, sin, n_rope=1, tn=kv, gate_bias=bias_b, gate_from=2)

    kcmp, vcmp_t = nsa_compress(pb, cmp_pos, cmp_w1.astype(BF16), cmp_w2.astype(BF16), bsz=bsz, seq=seq)
    overlap_t, onehot = _nsa_constants(seq)
    o = nsa_attention(pa, kcmp, vcmp_t, pb, overlap_t, onehot, bsz=bsz, seq=seq,
                      gate_col0=2 * kv // LANES)
    return out_ln(o, w_out.astype(BF16), x, g, b)


def _conv_kernel(x_ref, wba_ref, wca_ref, wha_ref, cwa_ref, woa_ref, wbb_ref, wcb_ref, whb_ref, cwb_ref,
                 wob_ref, g_ref, b_ref, o_ref, xb_ref, acc_ref, tail_ref, *, tiles_per_seq):
    i = pl.program_id(0)
    j = pl.program_id(1)
    seq_start = (i % tiles_per_seq) == 0

    def channel_tile(xb, t, wb_ref, wc_ref, wh_ref, cw_ref, wo_ref):
        bg = _dot(xb, wb_ref[...])
        u = _dot(xb, wc_ref[...]) * _dot(xb, wh_ref[...])
        tm = u.shape[0]
        prev = jnp.where(seq_start, 0.0, tail_ref[t])
        tail_ref[t] = u[tm - 8:tm, :]
        rid = lax.broadcasted_iota(jnp.int32, u.shape, 0)
        u1 = jnp.where(rid == 0, prev[7:8, :], pltpu.roll(u, 1, axis=0))
        u2 = pltpu.roll(u, 2, axis=0)
        u2 = jnp.where(rid == 0, prev[6:7, :], jnp.where(rid == 1, prev[7:8, :], u2))
        cw = cw_ref[...]
        y = cw[0:1, :] * u2 + cw[1:2, :] * u1 + cw[2:3, :] * u
        return _dot((bg * y).astype(BF16), wo_ref[...])

    def pair(xb):
        return (channel_tile(xb, 2 * j, wba_ref, wca_ref, wha_ref, cwa_ref, woa_ref)
                + channel_tile(xb, 2 * j + 1, wbb_ref, wcb_ref, whb_ref, cwb_ref, wob_ref))

    @pl.when(j == 0)
    def _():
        xb = x_ref[...].astype(BF16)
        xb_ref[...] = xb
        acc_ref[...] = pair(xb) + DEEPNORM_ALPHA * x_ref[...]

    @pl.when(j > 0)
    def _():
        acc_ref[...] += pair(xb_ref[...])

    @pl.when(j == pl.num_programs(1) - 1)
    def _():
        o_ref[...] = _layer_norm(acc_ref[...], g_ref[...], b_ref[...])


def conv_mixer_ln(x, w_in, conv_w, w_out, g, b, *, seq, tm=512, tn=256):
    m, d = x.shape
    nt = d // tn
    assert nt % 2 == 0
    kern = functools.partial(_conv_kernel, tiles_per_seq=seq // tm)

    def tile_specs(off):
        return [
            pl.BlockSpec((d, tn), lambda i, j: (0, 2 * j + off)),
            pl.BlockSpec((d, tn), lambda i, j: (0, 2 * j + off + nt)),
            pl.BlockSpec((d, tn), lambda i, j: (0, 2 * j + off + 2 * nt)),
            pl.BlockSpec((CONV_WIDTH, tn), lambda i, j: (0, 2 * j + off)),
            pl.BlockSpec((tn, d), lambda i, j: (2 * j + off, 0)),
        ]

    return pl.pallas_call(
        kern,
        grid=(m // tm, nt // 2),
        in_specs=[pl.BlockSpec((tm, d), lambda i, j: (i, 0))] + tile_specs(0) + tile_specs(1) + [
            pl.BlockSpec((1, d), lambda i, j: (0, 0)),
            pl.BlockSpec((1, d), lambda i, j: (0, 0)),
        ],
        out_specs=pl.BlockSpec((tm, d), lambda i, j: (i, 0)),
        out_shape=jax.ShapeDtypeStruct((m, d), F32),
        scratch_shapes=[pltpu.VMEM((tm, d), BF16), pltpu.VMEM((tm, d), F32),
                        pltpu.VMEM((nt, 8, tn), F32)],
        compiler_params=_params(("arbitrary", "arbitrary")),
        name="conv_mixer_ln",
    )(x, w_in, w_in, w_in, conv_w, w_out, w_in, w_in, w_in, conv_w, w_out, g, b)


def _log_sigmoid(z):
    return jnp.minimum(z, 0.0) - jnp.log1p(jnp.exp(-jnp.abs(z)))


def _gla_kernel(q_ref, k_ref, v_ref, r_ref, a_ref, wa_ref, ba_ref, ng_ref, o_ref, st_ref, upd_ref):
    i = pl.program_id(2)
    c_sz = GLA_CHUNK
    dk, dv = GLA_HEAD_K, GLA_HEAD_V
    n_c = q_ref.shape[0] // c_sz
    heads = range(GLA_HEADS_PER_STEP)

    @pl.when(i == 0)
    def _():
        st_ref[...] = jnp.zeros_like(st_ref)

    rr = lax.broadcasted_iota(jnp.int32, (c_sz, c_sz), 0)
    cc = lax.broadcasted_iota(jnp.int32, (c_sz, c_sz), 1)
    causal = cc <= rr
    tril = jnp.where(causal, 1.0, 0.0).astype(BF16)
    chunks = [slice(c * c_sz, (c + 1) * c_sz) for c in range(n_c)]

    z = _dot(a_ref[...].astype(BF16), wa_ref[...]) + ba_ref[...]
    gk = _log_sigmoid(z) / GLA_GATE_NORM
    hi, mid, lo = _split3(gk)
    bcum = jnp.concatenate(
        [_dot(tril, hi[rs]) + _dot(tril, mid[rs]) + _dot(tril, lo[rs]) for rs in chunks], axis=0)
    b_last = [bcum[rs][c_sz - 1:c_sz, :] for rs in chunks]
    b_last_rows = jnp.concatenate([jnp.broadcast_to(bl, (c_sz, bl.shape[1])) for bl in b_last], axis=0)
    qd = (q_ref[...] * (dk ** -0.5) * jnp.exp(bcum)).astype(BF16)
    k = k_ref[...]
    kd = (k * jnp.exp(-bcum)).astype(BF16)
    kl = (k * jnp.exp(b_last_rows - bcum)).astype(BF16)
    v = v_ref[...].astype(BF16)

    o_intra = []
    for hh in heads:
        ks, vs = slice(hh * dk, (hh + 1) * dk), slice(hh * dv, (hh + 1) * dv)
        outs = []
        for c, rs in enumerate(chunks):
            att = jnp.where(causal, _dot_nt(qd[rs, ks], kd[rs, ks]), 0.0).astype(BF16)
            outs.append(_dot(att, v[rs, vs]))
            upd_ref[hh, c] = _dot_tn(v[rs, vs], kl[rs, ks])
        o_intra.append(outs)

    st = [st_ref[hh] for hh in heads]
    ng = ng_ref[...]
    for c, rs in enumerate(chunks):
        for hh in heads:
            ks, vs = slice(hh * dk, (hh + 1) * dk), slice(hh * dv, (hh + 1) * dv)
            o = o_intra[hh][c] + _dot_nt(qd[rs, ks], st[hh].astype(BF16))
            st[hh] = st[hh] * jnp.exp(b_last[c][:, ks]) + upd_ref[hh, c]
            o = o * lax.rsqrt(jnp.mean(o * o, axis=-1, keepdims=True) + LN_EPS) * ng
            o_ref[rs, vs] = (o * _silu(r_ref[rs, vs])).astype(BF16)
    for hh in heads:
        st_ref[hh] = st[hh]


def gla_scan(p, a, wa, ba, ng, *, bsz, seq, tt=512):
    m = p.shape[0]
    dk, dv = GLA_HEAD_K, GLA_HEAD_V
    hp = GLA_HEADS_PER_STEP
    nt = seq // tt
    c_k = GLA_KEY_DIM // (hp * dk)
    c_v = 2 * GLA_KEY_DIM // (hp * dv)
    c_r = (2 * GLA_KEY_DIM + GLA_VAL_DIM) // (hp * dv)
    row = lambda b, h, i: b * nt + i
    return pl.pallas_call(
        _gla_kernel,
        grid=(bsz, GLA_HEADS // hp, nt),
        in_specs=[
            pl.BlockSpec((tt, hp * dk), lambda b, h, i: (row(b, h, i), h)),
            pl.BlockSpec((tt, hp * dk), lambda b, h, i: (row(b, h, i), c_k + h)),
            pl.BlockSpec((tt, hp * dv), lambda b, h, i: (row(b, h, i), c_v + h)),
            pl.BlockSpec((tt, hp * dv), lambda b, h, i: (row(b, h, i), c_r + h)),
            pl.BlockSpec((tt, LANES), lambda b, h, i: (row(b, h, i), 0)),
            pl.BlockSpec((LANES, hp * dk), lambda b, h, i: (0, h)),
            pl.BlockSpec((1, hp * dk), lambda b, h, i: (0, h)),
            pl.BlockSpec((1, dv), lambda b, h, i: (0, 0)),
        ],
        out_specs=pl.BlockSpec((tt, hp * dv), lambda b, h, i: (row(b, h, i), h)),
        out_shape=jax.ShapeDtypeStruct((m, GLA_VAL_DIM), BF16),
        scratch_shapes=[pltpu.VMEM((hp, dv, dk), F32),
                        pltpu.VMEM((hp, tt // GLA_CHUNK, dv, dk), F32)],
        compiler_params=_params(("parallel", "parallel", "arbitrary")),
        name="gla_scan",
    )(p, p, p, p, a, wa, ba, ng)


def gla_mixer_ln(x, w_in, w_a2, b_a, norm_g, w_out, g, b, *, bsz, seq):
    n_main = 2 * GLA_KEY_DIM + 2 * GLA_VAL_DIM
    dummy = jnp.zeros((x.shape[0], LANES), F32)
    p = proj(x, w_in[:, :n_main].astype(BF16), dummy, dummy, tn=PROJ_WIDE)
    w_a = jnp.pad(w_in[:, n_main:], ((0, 0), (0, LANES - GLA_GATE_RANK))).astype(BF16)
    a = proj(x, w_a, dummy, dummy, tn=LANES)
    wa2 = jnp.pad(w_a2, ((0, LANES - GLA_GATE_RANK), (0, 0))).astype(BF16)
    o = gla_scan(p, a, wa2, b_a[None, :], norm_g[None, :], bsz=bsz, seq=seq)
    return out_ln(o, w_out.astype(BF16), x, g, b)


def kernel(x, positions, ln_g, ln_b, ffn_w_in, ffn_w_out, nsa_w_in, nsa_gate_b, nsa_cmp_pos,
           nsa_cmp_w1, nsa_cmp_w2, nsa_w_out, conv_w_in, conv_w, conv_w_out, gla_w_in, gla_w_a2,
           gla_b_a, gla_norm_g, gla_w_out):
    bsz, seq, d = x.shape
    m = bsz * seq
    h = x.reshape(m, d)
    cos, sin = rope_tables(positions.reshape(m, 1).astype(F32))
    ffn_in = ffn_w_in.astype(BF16)
    ffn_out = (ffn_w_out * MACARON_WEIGHT).astype(BF16)
    for i in range(DEPTH):
        lg = lambda k: ln_g[i, k][None, :]
        lb = lambda k: ln_b[i, k][None, :]
        h = ffn_ln(h, ffn_in, ffn_out, lg(0), lb(0), i, 0)
        kind, j = i % N_MIXERS, i // N_MIXERS
        if kind == 0:
            h = nsa_mixer_ln(h, cos, sin, nsa_w_in[j], nsa_gate_b[j], nsa_cmp_pos[j], nsa_cmp_w1[j],
                             nsa_cmp_w2[j], nsa_w_out[j], lg(1), lb(1), bsz=bsz, seq=seq)
        elif kind == 1:
            h = conv_mixer_ln(h, conv_w_in[j].astype(BF16), conv_w[j], conv_w_out[j].astype(BF16),
                              lg(1), lb(1), seq=seq)
        else:
            h = gla_mixer_ln(h, gla_w_in[j], gla_w_a2[j], gla_b_a[j], gla_norm_g[j], gla_w_out[j],
                             lg(1), lb(1), bsz=bsz, seq=seq)
        h = ffn_ln(h, ffn_in, ffn_out, lg(2), lb(2), i, 1)
    return h.reshape(bsz, seq, d)
```

```python
import functools

import numpy as np
import jax
import jax.numpy as jnp
from jax import lax
from jax.experimental import pallas as pl
from jax.experimental.pallas import tpu as pltpu

F32 = jnp.float32
BF16 = jnp.bfloat16

D_MODEL = 2048
DEPTH = 4
N_MIXERS = 3
DEEPNORM_ALPHA = (2.0 * DEPTH) ** 0.25
LN_EPS = 1e-5
MACARON_WEIGHT = 0.5
D_FF = 5632

NSA_HEADS = 16
NSA_KV_GROUPS = 4
NSA_REP = NSA_HEADS // NSA_KV_GROUPS
NSA_HEAD_DIM = D_MODEL // NSA_HEADS
NSA_Q = NSA_HEADS * NSA_HEAD_DIM
NSA_KV = NSA_KV_GROUPS * NSA_HEAD_DIM
CMP_BLOCK = 32
CMP_STRIDE = 16
SEL_BLOCK = 64
SEL_TOPK = 16
WINDOW = 512
ROPE_THETA = 10000.0
NEG = -1e30
FORCE = 1e3
LOG2E = float(np.log2(np.e))

CONV_WIDTH = 3

GLA_HEADS = 4
GLA_KEY_DIM = D_MODEL // 2
GLA_VAL_DIM = D_MODEL
GLA_HEAD_K = GLA_KEY_DIM // GLA_HEADS
GLA_HEAD_V = GLA_VAL_DIM // GLA_HEADS
GLA_GATE_RANK = 16
GLA_GATE_NORM = 16.0
GLA_CHUNK = 64
GLA_HEADS_PER_STEP = 2

LANES = 128
PROJ_SLABS = 4
OUT_SLABS = 2
PROJ_WIDE = 1024
VMEM_LIMIT = 56 * 1024 * 1024


def _params(sem):
    return pltpu.CompilerParams(dimension_semantics=sem, vmem_limit_bytes=VMEM_LIMIT)


def _layer_norm(y, g, b):
    mu = jnp.mean(y, axis=-1, keepdims=True)
    d = y - mu
    var = jnp.mean(d * d, axis=-1, keepdims=True)
    return d * lax.rsqrt(var + LN_EPS) * g + b


def _silu(h):
    return h * jax.nn.sigmoid(h)


def _dot(a, b):
    return jnp.dot(a, b, preferred_element_type=F32)


def _dot_nt(a, b):
    return lax.dot_general(a, b, (((1,), (1,)), ((), ())), preferred_element_type=F32)


def _dot_tn(a, b):
    return lax.dot_general(a, b, (((0,), (0,)), ((), ())), preferred_element_type=F32)


def _split3(x):
    hi = x.astype(BF16)
    r1 = x - hi.astype(F32)
    mid = r1.astype(BF16)
    lo = (r1 - mid.astype(F32)).astype(BF16)
    return hi, mid, lo


def _ffn_kernel(x_ref, wga_ref, wua_ref, woa_ref, wgb_ref, wub_ref, wob_ref, g_ref, b_ref, o_ref,
                xb_ref, acc_ref, *, n_ff_tiles):
    j = pl.program_id(1)

    def ff_tile(xb, wg_ref, wu_ref, wo_ref):
        h = _dot(xb, wg_ref[...])
        u = _dot(xb, wu_ref[...])
        return _dot((_silu(h) * u).astype(BF16), wo_ref[...])

    def ff_pair(xb):
        return ff_tile(xb, wga_ref, wua_ref, woa_ref) + ff_tile(xb, wgb_ref, wub_ref, wob_ref)

    first = j == 0
    paired = 2 * j + 1 < n_ff_tiles

    @pl.when(first)
    def _():
        xb = x_ref[...].astype(BF16)
        xb_ref[...] = xb
        acc_ref[...] = ff_pair(xb)

    @pl.when(jnp.logical_not(first) & paired)
    def _():
        acc_ref[...] += ff_pair(xb_ref[...])

    @pl.when(jnp.logical_not(first) & jnp.logical_not(paired))
    def _():
        acc_ref[...] += ff_tile(xb_ref[...], wga_ref, wua_ref, woa_ref)

    @pl.when(j == pl.num_programs(1) - 1)
    def _():
        y = DEEPNORM_ALPHA * x_ref[...] + MACARON_WEIGHT * acc_ref[...]
        o_ref[...] = _layer_norm(y, g_ref[...], b_ref[...])


def ffn_ln(x, w_in, w_out, g, b, layer, half, *, tm=512, tf=512):
    m, d = x.shape
    ff = w_out.shape[2]
    nt = ff // tf
    assert nt >= 2
    nj = (nt + 1) // 2

    def tile_a(j):
        return 2 * j

    def tile_b(j):
        return jnp.minimum(2 * j + 1, nt - 1)

    def w_specs(tile):
        return [
            pl.BlockSpec((None, None, d, tf), lambda i, j: (layer, half, 0, tile(j))),
            pl.BlockSpec((None, None, d, tf), lambda i, j: (layer, half, 0, tile(j) + nt)),
            pl.BlockSpec((None, None, tf, d), lambda i, j: (layer, half, tile(j), 0)),
        ]

    return pl.pallas_call(
        functools.partial(_ffn_kernel, n_ff_tiles=nt),
        grid=(m // tm, nj),
        in_specs=[pl.BlockSpec((tm, d), lambda i, j: (i, 0))] + w_specs(tile_a) + w_specs(tile_b) + [
            pl.BlockSpec((1, d), lambda i, j: (0, 0)),
            pl.BlockSpec((1, d), lambda i, j: (0, 0)),
        ],
        out_specs=pl.BlockSpec((tm, d), lambda i, j: (i, 0)),
        out_shape=jax.ShapeDtypeStruct((m, d), F32),
        scratch_shapes=[pltpu.VMEM((tm, d), BF16), pltpu.VMEM((tm, d), F32)],
        compiler_params=_params(("parallel", "arbitrary")),
        name="ffn_ln",
    )(x, w_in, w_in, w_out, w_in, w_in, w_out, g, b)


def _proj_kernel(x_ref, w_ref, cos_ref, sin_ref, o_ref, xb_ref, *, n_rope, n_scale, scale):
    j = pl.program_id(1)

    @pl.when(j == 0)
    def _():
        xb_ref[...] = x_ref[...].astype(BF16)

    tm, tn = o_ref.shape
    slabs = [slice(r * tm // PROJ_SLABS, (r + 1) * tm // PROJ_SLABS) for r in range(PROJ_SLABS)]
    ys = [_dot(xb_ref[rs, :], w_ref[...]) for rs in slabs]

    if n_rope > 0:
        rotary = j < n_rope
        sc = jnp.where(j < n_scale, scale, 1.0).astype(F32)
        for rs, y in zip(slabs, ys):
            cos = jnp.where(rotary, cos_ref[rs, :], 1.0)
            sin = jnp.where(rotary, sin_ref[rs, :], 0.0)
            for hh in range(tn // LANES):
                t = y[:, hh * LANES:(hh + 1) * LANES]
                rot = pltpu.roll(t, LANES // 2, axis=1)
                o_ref[rs, hh * LANES:(hh + 1) * LANES] = ((t * cos + rot * sin) * sc).astype(o_ref.dtype)
    else:
        for rs, y in zip(slabs, ys):
            o_ref[rs, :] = y.astype(o_ref.dtype)


def proj(x, w, cos, sin, *, n_rope=0, n_scale=0, scale=1.0, out_dtype=F32, tm=1024, tn=512):
    m, d = x.shape
    n = w.shape[1]
    kern = functools.partial(_proj_kernel, n_rope=n_rope, n_scale=n_scale, scale=scale)
    return pl.pallas_call(
        kern,
        grid=(m // tm, n // tn),
        in_specs=[
            pl.BlockSpec((tm, d), lambda i, j: (i, 0)),
            pl.BlockSpec((d, tn), lambda i, j: (0, j)),
            pl.BlockSpec((tm, LANES), lambda i, j: (i, 0)),
            pl.BlockSpec((tm, LANES), lambda i, j: (i, 0)),
        ],
        out_specs=pl.BlockSpec((tm, tn), lambda i, j: (i, j)),
        out_shape=jax.ShapeDtypeStruct((m, n), out_dtype),
        scratch_shapes=[pltpu.VMEM((tm, d), BF16)],
        compiler_params=_params(("parallel", "arbitrary")),
        name="proj",
    )(x, w, cos, sin)


def _rope_table_kernel(pos_ref, inv_ref, sign_ref, cos_ref, sin_ref):
    ang = pos_ref[...] * inv_ref[...]
    cos_ref[...] = jnp.cos(ang)
    sin_ref[...] = jnp.sin(ang) * sign_ref[...]


def rope_tables(pos_f32, *, tm=2048):
    m = pos_f32.shape[0]
    hd = NSA_HEAD_DIM
    inv = ROPE_THETA ** (-jnp.arange(0, hd, 2, dtype=F32) / hd)
    inv_full = jnp.concatenate([inv, inv])[None, :]
    sign = jnp.concatenate([-jnp.ones((hd // 2,), F32), jnp.ones((hd // 2,), F32)])[None, :]
    return pl.pallas_call(
        _rope_table_kernel,
        grid=(m // tm,),
        in_specs=[
            pl.BlockSpec((tm, 1), lambda i: (i, 0)),
            pl.BlockSpec((1, hd), lambda i: (0, 0)),
            pl.BlockSpec((1, hd), lambda i: (0, 0)),
        ],
        out_specs=[pl.BlockSpec((tm, hd), lambda i: (i, 0))] * 2,
        out_shape=[jax.ShapeDtypeStruct((m, hd), F32)] * 2,
        compiler_params=_params(("parallel",)),
        name="rope_tables",
    )(pos_f32, inv_full, sign)


def _gate_kernel(x_ref, w_ref, b_ref, o_ref):
    z = _dot(x_ref[...].astype(BF16), w_ref[...]) + b_ref[...]
    o_ref[...] = jax.nn.sigmoid(z)


def gate_proj(x, w, b, *, tm=1024):
    m, d = x.shape
    n = w.shape[1]
    return pl.pallas_call(
        _gate_kernel,
        grid=(m // tm,),
        in_specs=[
            pl.BlockSpec((tm, d), lambda i: (i, 0)),
            pl.BlockSpec((d, n), lambda i: (0, 0)),
            pl.BlockSpec((1, n), lambda i: (0, 0)),
        ],
        out_specs=pl.BlockSpec((tm, n), lambda i: (i, 0)),
        out_shape=jax.ShapeDtypeStruct((m, n), F32),
        compiler_params=_params(("parallel",)),
        name="gate_proj",
    )(x, w, b)


def _out_ln_kernel(a_ref, w_ref, x_ref, g_ref, b_ref, o_ref):
    tm = a_ref.shape[0]
    slabs = [slice(r * tm // OUT_SLABS, (r + 1) * tm // OUT_SLABS) for r in range(OUT_SLABS)]
    ys = [_dot(a_ref[rs, :], w_ref[...]) for rs in slabs]
    for rs, y in zip(slabs, ys):
        o_ref[rs, :] = _layer_norm(DEEPNORM_ALPHA * x_ref[rs, :] + y, g_ref[...], b_ref[...])


def out_ln(a, w, x, g, b, *, tm=512):
    m, d = x.shape
    k = a.shape[1]
    return pl.pallas_call(
        _out_ln_kernel,
        grid=(m // tm,),
        in_specs=[
            pl.BlockSpec((tm, k), lambda i: (i, 0)),
            pl.BlockSpec((k, d), lambda i: (0, 0)),
            pl.BlockSpec((tm, d), lambda i: (i, 0)),
            pl.BlockSpec((1, d), lambda i: (0, 0)),
            pl.BlockSpec((1, d), lambda i: (0, 0)),
        ],
        out_specs=pl.BlockSpec((tm, d), lambda i: (i, 0)),
        out_shape=jax.ShapeDtypeStruct((m, d), F32),
        compiler_params=_params(("parallel",)),
        name="out_ln",
    )(a, w, x, g, b)


def _gelu_tanh(x):
    c = float(np.sqrt(2.0 / np.pi))
    return 0.5 * x * (1.0 + jnp.tanh(c * (x + 0.044715 * (x * x * x))))


def _cmp_kernel(kc_ref, vc_ref, pos_ref, w1_ref, w2_ref, ko_ref, vo_ref):
    seq, hd = kc_ref.shape
    nrow = seq // CMP_STRIDE
    half = CMP_STRIDE * hd
    for idx, (src, dst) in enumerate(((kc_ref, ko_ref), (vc_ref, vo_ref))):
        first, second = [], []
        for l in range(CMP_STRIDE):
            xl = src[pl.ds(l, nrow, stride=CMP_STRIDE), :]
            first.append(xl + pos_ref[idx, l:l + 1, :])
            second.append(xl + pos_ref[idx, CMP_STRIDE + l:CMP_STRIDE + l + 1, :])
        ya = _dot(jnp.concatenate(first, axis=1).astype(BF16), w1_ref[idx, 0:half, :])
        yb = _dot(jnp.concatenate(second, axis=1).astype(BF16), w1_ref[idx, half:2 * half, :])
        h = _gelu_tanh(ya + pltpu.roll(yb, nrow - 1, axis=0))
        out = _dot(h.astype(BF16), w2_ref[idx])
        if idx == 0:
            dst[0, 0] = out.astype(BF16)
        else:
            dst[0, 0] = out.astype(BF16).astype(F32).T.astype(BF16)


def nsa_compress(pb, pos, w1, w2, *, bsz, seq):
    g = NSA_KV_GROUPS
    hd = NSA_HEAD_DIM
    nrow = seq // CMP_STRIDE
    oblk = pl.BlockSpec((1, 1, nrow, hd), lambda b, gg: (b, gg, 0, 0))
    return pl.pallas_call(
        _cmp_kernel,
        grid=(bsz, g),
        in_specs=[
            pl.BlockSpec((seq, hd), lambda b, gg: (b, gg)),
            pl.BlockSpec((seq, hd), lambda b, gg: (b, g + gg)),
            pl.BlockSpec(pos.shape, lambda b, gg: (0, 0, 0)),
            pl.BlockSpec(w1.shape, lambda b, gg: (0, 0, 0)),
            pl.BlockSpec(w2.shape, lambda b, gg: (0, 0, 0)),
        ],
        out_specs=[oblk, oblk],
        out_shape=[jax.ShapeDtypeStruct((bsz, g, nrow, hd), BF16)] * 2,
        compiler_params=_params(("parallel", "parallel")),
        name="nsa_compress",
    )(pb, pb, pos, w1, w2)


def _nsa_attn_kernel(*refs, tq, tk, seq):
    i = pl.program_id(2)
    for tile in range(seq // tq):
        @pl.when(i == tile)
        def _(tile=tile):
            _nsa_attn_tile(tile, *refs, tq=tq, tk=tk, seq=seq)


def _nsa_attn_tile(i, q_ref, ks_ref, kw_ref, vs_ref, vw_ref, kc_ref, vct_ref, gate_ref,
                   ovt_ref, hot_ref, o_ref, ksa, vst, kwp, vwt, pbuf, *, tq, tk, seq):
    hd = NSA_HEAD_DIM
    rep = NSA_REP
    cols = rep * tq
    nblk = seq // tq
    wblk = WINDOW // tq

    @pl.when(i == 0)
    def _():
        ksa[:, 0:hd] = ks_ref[...]
        ksa[:, hd:2 * hd] = hot_ref[...]
        kwp[0:WINDOW, :] = jnp.zeros((WINDOW, hd), BF16)
        kwp[WINDOW:WINDOW + seq, :] = kw_ref[...]
        for blk in range(wblk):
            vwt[blk] = jnp.zeros((hd, tq), BF16)
        for blk in range(nblk):
            rs = slice(blk * tq, (blk + 1) * tq)
            vst[blk] = vs_ref[rs, :].astype(F32).T.astype(BF16)
            vwt[wblk + blk] = vw_ref[rs, :].astype(F32).T.astype(BF16)

    t0 = i * tq
    blk0 = i
    q_t = jnp.concatenate([q_ref[:, r * hd:(r + 1) * hd].astype(F32).T for r in range(rep)],
                          axis=1).astype(BF16)

    def lane_t(shape):
        return t0 + (lax.broadcasted_iota(jnp.int32, shape, 1) & (tq - 1))

    def softmax_cols(s):
        mx = jnp.max(s, axis=0, keepdims=True)
        p = jnp.exp2(s - mx)
        return mx, p, jnp.sum(p, axis=0, keepdims=True)

    ku = lax.broadcasted_iota(jnp.int32, (tq, cols), 0)
    qu = lax.broadcasted_iota(jnp.int32, (tq, cols), 1) & (tq - 1)
    bias_diag = jnp.where(ku <= qu, 0.0, NEG)
    bias_first = jnp.where(ku > qu, 0.0, NEG)

    ncmp = kc_ref.shape[2]
    band = WINDOW + tq
    dstart = t0
    s = _dot(kc_ref[0, 0], q_t)
    diag_forced = tq <= 2 * SEL_BLOCK
    if diag_forced:
        sd = _dot(ksa[pl.ds(dstart, tq), 0:hd], q_t) + bias_diag
    sw = _dot(kwp[pl.ds(dstart, band), :], q_t)

    n_idx = lax.broadcasted_iota(jnp.int32, (ncmp, cols), 0)
    valid = n_idx * CMP_STRIDE + (CMP_BLOCK - 1) <= lane_t((ncmp, cols))
    sm = jnp.where(valid, s, NEG)
    mx = jnp.max(sm, axis=0, keepdims=True)
    p = jnp.where(valid, jnp.exp2(sm - mx), 0.0)
    den = jnp.sum(p, axis=0, keepdims=True)
    p = p * (1.0 / jnp.where(den > 0.0, den, 1.0))
    o_cmp = _dot(vct_ref[0, 0], p.astype(BF16))

    pieces = []
    for jb in range(wblk + 1):
        piece = sw[jb * tq:(jb + 1) * tq]
        if jb == wblk:
            piece = piece + bias_diag
        else:
            piece = piece + jnp.where(t0 - WINDOW + jb * tq < 0, NEG, 0.0).astype(F32)
            if jb == 0:
                piece = piece + bias_first
        pieces.append(piece)
    _, pw, l_w = softmax_cols(jnp.concatenate(pieces, axis=0))
    vw_t = jnp.concatenate([vwt[blk0 + k] for k in range(wblk + 1)], axis=1)
    o_win = _dot(vw_t, pw.astype(BF16)) * (1.0 / l_w)

    psum = p[:, 0:tq]
    for r in range(1, rep):
        psum = psum + p[:, r * tq:(r + 1) * tq]
    hi, mid, lo = _split3(psum)
    ovt = ovt_ref[...]
    imp = _dot(ovt, hi) + _dot(ovt, mid) + _dot(ovt, lo)
    n_sel = seq // SEL_BLOCK
    b_idx = lax.broadcasted_iota(jnp.int32, (n_sel, tq), 0)
    cur = (t0 + lax.broadcasted_iota(jnp.int32, (n_sel, tq), 1)) // SEL_BLOCK
    forced = (b_idx == 0) | (b_idx == cur) | (b_idx == cur - 1)
    score = jnp.where(b_idx <= cur, imp[0:n_sel] + jnp.where(forced, FORCE, 0.0), NEG)
    rank = jnp.zeros((n_sel, tq), F32)
    for mp in range(n_sel):
        row = score[mp:mp + 1, :]
        beats = (row > score) | ((row == score) & (b_idx > mp))
        rank = rank + jnp.where(beats, 1.0, 0.0)
    selected = rank < float(SEL_TOPK)
    before_tile = b_idx < t0 // SEL_BLOCK

    def augment(keep):
        bias_blk = jnp.where(keep, 0.0, NEG)
        bias_blk = jnp.concatenate([bias_blk, jnp.zeros((hd - n_sel, tq), F32)], axis=0).astype(BF16)
        return jnp.concatenate([q_t, jnp.concatenate([bias_blk] * rep, axis=1)], axis=0)

    q_aug = augment(selected & before_tile)
    if not diag_forced:
        sd = _dot(ksa[pl.ds(dstart, tq), :], augment(selected & jnp.logical_not(before_tile))) + bias_diag

    m_i, pd, l_i = softmax_cols(sd)
    acc = _dot(vst[blk0], pd.astype(BF16))
    per = tk // tq
    pbuf[...] = jnp.zeros_like(pbuf)

    def v_chunk(c):
        return jnp.concatenate([vst[c * per + k] for k in range(per)], axis=1)

    n_chunks = (t0 + tk - 1) // tk
    last = max(n_chunks - 1, 0)

    def sel_body(c, carry):
        m_i, l_i, acc = carry
        off = c * tk
        sc = _dot(ksa[pl.ds(off, tk), :], q_aug)
        pv = _dot(v_chunk(max(c - 1, 0)), pbuf[...])
        m_new = jnp.maximum(m_i, jnp.max(sc, axis=0, keepdims=True))
        pp = jnp.exp2(sc - m_new)
        pbuf[...] = pp.astype(BF16)
        alpha = jnp.exp2(m_i - m_new)
        l_new = alpha * l_i + jnp.sum(pp, axis=0, keepdims=True)
        return m_new, l_new, alpha * (acc + pv)

    carry = (m_i, l_i, acc)
    for c in range(n_chunks):
        carry = sel_body(c, carry)
    _, l_s, acc_s = carry
    acc_s = acc_s + _dot(v_chunk(last), pbuf[...])
    o_sel = acc_s * (1.0 / l_s)

    g_t = gate_ref[...].T
    for r in range(rep):
        cs = slice(r * tq, (r + 1) * tq)
        o_r = (g_t[r:r + 1, :] * o_cmp[:, cs]
               + g_t[rep + r:rep + r + 1, :] * o_sel[:, cs]
               + g_t[2 * rep + r:2 * rep + r + 1, :] * o_win[:, cs])
        o_ref[:, r * hd:(r + 1) * hd] = o_r.T.astype(BF16)


def nsa_attention(pa, kcmp, vcmp_t, gates, overlap_t, onehot, *, bsz, seq, tq=512, tk=512):
    m = pa.shape[0]
    hd = NSA_HEAD_DIM
    g = NSA_KV_GROUPS
    nq = seq // tq
    gw = NSA_REP * hd
    c_ks = NSA_Q // hd
    c_kw = (NSA_Q + NSA_KV) // hd
    c_vs = (NSA_Q + 2 * NSA_KV) // hd
    c_vw = (NSA_Q + 3 * NSA_KV) // hd
    ncmp = kcmp.shape[2]

    def kv_spec(c0):
        return pl.BlockSpec((seq, hd), lambda b, gg, i: (b, c0 + gg))

    kern = functools.partial(_nsa_attn_kernel, tq=tq, tk=tk, seq=seq)
    return pl.pallas_call(
        kern,
        grid=(bsz, g, nq),
        in_specs=[
            pl.BlockSpec((tq, gw), lambda b, gg, i: (b * nq + i, gg)),
            kv_spec(c_ks), kv_spec(c_kw), kv_spec(c_vs), kv_spec(c_vw),
            pl.BlockSpec((1, 1, ncmp, hd), lambda b, gg, i: (b, gg, 0, 0)),
            pl.BlockSpec((1, 1, hd, ncmp), lambda b, gg, i: (b, gg, 0, 0)),
            pl.BlockSpec((tq, LANES), lambda b, gg, i: (b * nq + i, gg)),
            pl.BlockSpec(overlap_t.shape, lambda b, gg, i: (0, 0)),
            pl.BlockSpec(onehot.shape, lambda b, gg, i: (0, 0)),
        ],
        out_specs=pl.BlockSpec((tq, gw), lambda b, gg, i: (b * nq + i, gg)),
        out_shape=jax.ShapeDtypeStruct((m, NSA_Q), BF16),
        scratch_shapes=[
            pltpu.VMEM((seq, 2 * hd), BF16),
            pltpu.VMEM((seq // tq, hd, tq), BF16),
            pltpu.VMEM((seq + WINDOW, hd), BF16),
            pltpu.VMEM((seq // tq + WINDOW // tq, hd, tq), BF16),
            pltpu.VMEM((tk, NSA_REP * tq), BF16),
        ],
        compiler_params=_params(("parallel", "parallel", "arbitrary")),
        name="nsa_attention",
    )(pa, pa, pa, pa, pa, kcmp, vcmp_t, gates, overlap_t, onehot)


def _nsa_constants(seq):
    mm = np.arange(LANES)[:, None]
    n = np.arange(LANES)[None, :]
    c_start = n * CMP_STRIDE
    s_start = mm * SEL_BLOCK
    n_cmp = (seq - CMP_BLOCK) // CMP_STRIDE + 1
    overlap_t = ((c_start < s_start + SEL_BLOCK) & (c_start + CMP_BLOCK > s_start)
                 & (n < n_cmp) & (mm < seq // SEL_BLOCK))
    onehot = (np.arange(seq)[:, None] // SEL_BLOCK) == np.arange(LANES)[None, :]
    return (jnp.asarray(overlap_t.astype(np.float32), dtype=BF16),
            jnp.asarray(onehot.astype(np.float32), dtype=BF16))


def nsa_mixer_ln(x, cos, sin, w_in, gate_b, cmp_pos, cmp_w1, cmp_w2, w_out, g, b, *, bsz, seq):
    hd = NSA_HEAD_DIM
    grp = NSA_KV_GROUPS
    rep = NSA_REP
    kv = NSA_KV

    def wcols(order):
        return jnp.concatenate([w_in[:, NSA_Q + i * kv:NSA_Q + (i + 1) * kv] for i in order], axis=1)

    w_a = jnp.concatenate([w_in[:, :NSA_Q], wcols((2, 4, 3, 5))], axis=1).astype(BF16)
    pa = proj(x, w_a, cos, sin, n_rope=(NSA_Q + 2 * kv) // PROJ_WIDE, n_scale=NSA_Q // PROJ_WIDE,
              scale=hd ** -0.5 * LOG2E, out_dtype=BF16, tn=PROJ_WIDE)
    pb = proj(x, wcols((0, 1)).astype(BF16), cos, sin, n_rope=1, tn=kv)

    w_gl = w_in[:, NSA_Q + 6 * kv:].reshape(-1, 3, grp, rep).transpose(0, 2, 1, 3).reshape(-1, grp, 3 * rep)
    w_gl = jnp.pad(w_gl, ((0, 0), (0, 0), (0, LANES - 3 * rep))).reshape(-1, grp * LANES).astype(BF16)
    b_gl = gate_b.reshape(3, grp, rep).transpose(1, 0, 2).reshape(grp, 3 * rep)
    b_gl = jnp.pad(b_gl, ((0, 0), (0, LANES - 3 * rep))).reshape(1, grp * LANES)
    gates = gate_proj(x, w_gl, b_gl)

    kcmp, vcmp_t = nsa_compress(pb, cmp_pos, cmp_w1.astype(BF16), cmp_w2.astype(BF16), bsz=bsz, seq=seq)
    overlap_t, onehot = _nsa_constants(seq)
    o = nsa_attention(pa, kcmp, vcmp_t, gates, overlap_t, onehot, bsz=bsz, seq=seq)
    return out_ln(o, w_out.astype(BF16), x, g, b)


def _conv_kernel(x_ref, wba_ref, wca_ref, wha_ref, cwa_ref, woa_ref, wbb_ref, wcb_ref, whb_ref, cwb_ref,
                 wob_ref, g_ref, b_ref, o_ref, xb_ref, acc_ref, tail_ref, *, tiles_per_seq):
    i = pl.program_id(0)
    j = pl.program_id(1)
    seq_start = (i % tiles_per_seq) == 0

    def channel_tile(xb, t, wb_ref, wc_ref, wh_ref, cw_ref, wo_ref):
        bg = _dot(xb, wb_ref[...])
        u = _dot(xb, wc_ref[...]) * _dot(xb, wh_ref[...])
        tm = u.shape[0]
        prev = jnp.where(seq_start, 0.0, tail_ref[t])
        tail_ref[t] = u[tm - 8:tm, :]
        rid = lax.broadcasted_iota(jnp.int32, u.shape, 0)
        u1 = jnp.where(rid == 0, prev[7:8, :], pltpu.roll(u, 1, axis=0))
        u2 = pltpu.roll(u, 2, axis=0)
        u2 = jnp.where(rid == 0, prev[6:7, :], jnp.where(rid == 1, prev[7:8, :], u2))
        cw = cw_ref[...]
        y = cw[0:1, :] * u2 + cw[1:2, :] * u1 + cw[2:3, :] * u
        return _dot((bg * y).astype(BF16), wo_ref[...])

    def pair(xb):
        return (channel_tile(xb, 2 * j, wba_ref, wca_ref, wha_ref, cwa_ref, woa_ref)
                + channel_tile(xb, 2 * j + 1, wbb_ref, wcb_ref, whb_ref, cwb_ref, wob_ref))

    @pl.when(j == 0)
    def _():
        xb = x_ref[...].astype(BF16)
        xb_ref[...] = xb
        acc_ref[...] = pair(xb)

    @pl.when(j > 0)
    def _():
        acc_ref[...] += pair(xb_ref[...])

    @pl.when(j == pl.num_programs(1) - 1)
    def _():
        yy = DEEPNORM_ALPHA * x_ref[...] + acc_ref[...]
        o_ref[...] = _layer_norm(yy, g_ref[...], b_ref[...])


def conv_mixer_ln(x, w_in, conv_w, w_out, g, b, *, seq, tm=512, tn=256):
    m, d = x.shape
    nt = d // tn
    assert nt % 2 == 0
    kern = functools.partial(_conv_kernel, tiles_per_seq=seq // tm)

    def tile_specs(off):
        return [
            pl.BlockSpec((d, tn), lambda i, j: (0, 2 * j + off)),
            pl.BlockSpec((d, tn), lambda i, j: (0, 2 * j + off + nt)),
            pl.BlockSpec((d, tn), lambda i, j: (0, 2 * j + off + 2 * nt)),
            pl.BlockSpec((CONV_WIDTH, tn), lambda i, j: (0, 2 * j + off)),
            pl.BlockSpec((tn, d), lambda i, j: (2 * j + off, 0)),
        ]

    return pl.pallas_call(
        kern,
        grid=(m // tm, nt // 2),
        in_specs=[pl.BlockSpec((tm, d), lambda i, j: (i, 0))] + tile_specs(0) + tile_specs(1) + [
            pl.BlockSpec((1, d), lambda i, j: (0, 0)),
            pl.BlockSpec((1, d), lambda i, j: (0, 0)),
        ],
        out_specs=pl.BlockSpec((tm, d), lambda i, j: (i, 0)),
        out_shape=jax.ShapeDtypeStruct((m, d), F32),
        scratch_shapes=[pltpu.VMEM((tm, d), BF16), pltpu.VMEM((tm, d), F32),
                        pltpu.VMEM((nt, 8, tn), F32)],
        compiler_params=_params(("arbitrary", "arbitrary")),
        name="conv_mixer_ln",
    )(x, w_in, w_in, w_in, conv_w, w_out, w_in, w_in, w_in, conv_w, w_out, g, b)


def _log_sigmoid(z):
    return jnp.minimum(z, 0.0) - jnp.log1p(jnp.exp(-jnp.abs(z)))


def _gla_kernel(q_ref, k_ref, v_ref, r_ref, a_ref, wa_ref, ba_ref, ng_ref, o_ref, st_ref, upd_ref):
    i = pl.program_id(2)
    c_sz = GLA_CHUNK
    dk, dv = GLA_HEAD_K, GLA_HEAD_V
    n_c = q_ref.shape[0] // c_sz
    heads = range(GLA_HEADS_PER_STEP)

    @pl.when(i == 0)
    def _():
        st_ref[...] = jnp.zeros_like(st_ref)

    rr = lax.broadcasted_iota(jnp.int32, (c_sz, c_sz), 0)
    cc = lax.broadcasted_iota(jnp.int32, (c_sz, c_sz), 1)
    causal = cc <= rr
    tril = jnp.where(causal, 1.0, 0.0).astype(BF16)
    chunks = [slice(c * c_sz, (c + 1) * c_sz) for c in range(n_c)]

    z = _dot(a_ref[...].astype(BF16), wa_ref[...]) + ba_ref[...]
    gk = _log_sigmoid(z) / GLA_GATE_NORM
    hi, mid, lo = _split3(gk)
    bcum = jnp.concatenate(
        [_dot(tril, hi[rs]) + _dot(tril, mid[rs]) + _dot(tril, lo[rs]) for rs in chunks], axis=0)
    b_last = [bcum[rs][c_sz - 1:c_sz, :] for rs in chunks]
    b_last_rows = jnp.concatenate([jnp.broadcast_to(bl, (c_sz, bl.shape[1])) for bl in b_last], axis=0)
    qd = (q_ref[...] * (dk ** -0.5) * jnp.exp(bcum)).astype(BF16)
    k = k_ref[...]
    kd = (k * jnp.exp(-bcum)).astype(BF16)
    kl = (k * jnp.exp(b_last_rows - bcum)).astype(BF16)
    v = v_ref[...].astype(BF16)

    o_intra = []
    for hh in heads:
        ks, vs = slice(hh * dk, (hh + 1) * dk), slice(hh * dv, (hh + 1) * dv)
        outs = []
        for c, rs in enumerate(chunks):
            att = jnp.where(causal, _dot_nt(qd[rs, ks], kd[rs, ks]), 0.0).astype(BF16)
            outs.append(_dot(att, v[rs, vs]))
            upd_ref[hh, c] = _dot_tn(v[rs, vs], kl[rs, ks])
        o_intra.append(outs)

    st = [st_ref[hh] for hh in heads]
    ng = ng_ref[...]
    for c, rs in enumerate(chunks):
        for hh in heads:
            ks, vs = slice(hh * dk, (hh + 1) * dk), slice(hh * dv, (hh + 1) * dv)
            o = o_intra[hh][c] + _dot_nt(qd[rs, ks], st[hh].astype(BF16))
            st[hh] = st[hh] * jnp.exp(b_last[c][:, ks]) + upd_ref[hh, c]
            o = o * lax.rsqrt(jnp.mean(o * o, axis=-1, keepdims=True) + LN_EPS) * ng
            o_ref[rs, vs] = (o * _silu(r_ref[rs, vs])).astype(BF16)
    for hh in heads:
        st_ref[hh] = st[hh]


def gla_scan(p, a, wa, ba, ng, *, bsz, seq, tt=512):
    m = p.shape[0]
    dk, dv = GLA_HEAD_K, GLA_HEAD_V
    hp = GLA_HEADS_PER_STEP
    nt = seq // tt
    c_k = GLA_KEY_DIM // (hp * dk)
    c_v = 2 * GLA_KEY_DIM // (hp * dv)
    c_r = (2 * GLA_KEY_DIM + GLA_VAL_DIM) // (hp * dv)
    row = lambda b, h, i: b * nt + i
    return pl.pallas_call(
        _gla_kernel,
        grid=(bsz, GLA_HEADS // hp, nt),
        in_specs=[
            pl.BlockSpec((tt, hp * dk), lambda b, h, i: (row(b, h, i), h)),
            pl.BlockSpec((tt, hp * dk), lambda b, h, i: (row(b, h, i), c_k + h)),
            pl.BlockSpec((tt, hp * dv), lambda b, h, i: (row(b, h, i), c_v + h)),
            pl.BlockSpec((tt, hp * dv), lambda b, h, i: (row(b, h, i), c_r + h)),
            pl.BlockSpec((tt, LANES), lambda b, h, i: (row(b, h, i), 0)),
            pl.BlockSpec((LANES, hp * dk), lambda b, h, i: (0, h)),
            pl.BlockSpec((1, hp * dk), lambda b, h, i: (0, h)),
            pl.BlockSpec((1, dv), lambda b, h, i: (0, 0)),
        ],
        out_specs=pl.BlockSpec((tt, hp * dv), lambda b, h, i: (row(b, h, i), h)),
        out_shape=jax.ShapeDtypeStruct((m, GLA_VAL_DIM), BF16),
        scratch_shapes=[pltpu.VMEM((hp, dv, dk), F32),
                        pltpu.VMEM((hp, tt // GLA_CHUNK, dv, dk), F32)],
        compiler_params=_params(("parallel", "parallel", "arbitrary")),
        name="gla_scan",
    )(p, p, p, p, a, wa, ba, ng)


def gla_mixer_ln(x, w_in, w_a2, b_a, norm_g, w_out, g, b, *, bsz, seq):
    n_main = 2 * GLA_KEY_DIM + 2 * GLA_VAL_DIM
    dummy = jnp.zeros((x.shape[0], LANES), F32)
    p = proj(x, w_in[:, :n_main].astype(BF16), dummy, dummy, tn=PROJ_WIDE)
    w_a = jnp.pad(w_in[:, n_main:], ((0, 0), (0, LANES - GLA_GATE_RANK))).astype(BF16)
    a = proj(x, w_a, dummy, dummy, tn=LANES)
    wa2 = jnp.pad(w_a2, ((0, LANES - GLA_GATE_RANK), (0, 0))).astype(BF16)
    o = gla_scan(p, a, wa2, b_a[None, :], norm_g[None, :], bsz=bsz, seq=seq)
    return out_ln(o, w_out.astype(BF16), x, g, b)


def kernel(x, positions, ln_g, ln_b, ffn_w_in, ffn_w_out, nsa_w_in, nsa_gate_b, nsa_cmp_pos,
           nsa_cmp_w1, nsa_cmp_w2, nsa_w_out, conv_w_in, conv_w, conv_w_out, gla_w_in, gla_w_a2,
           gla_b_a, gla_norm_g, gla_w_out):
    bsz, seq, d = x.shape
    m = bsz * seq
    h = x.reshape(m, d)
    cos, sin = rope_tables(positions.reshape(m, 1).astype(F32))
    ffn_in = ffn_w_in.astype(BF16)
    ffn_out = ffn_w_out.astype(BF16)
    for i in range(DEPTH):
        lg = lambda k: ln_g[i, k][None, :]
        lb = lambda k: ln_b[i, k][None, :]
        h = ffn_ln(h, ffn_in, ffn_out, lg(0), lb(0), i, 0)
        kind, j = i % N_MIXERS, i // N_MIXERS
        if kind == 0:
            h = nsa_mixer_ln(h, cos, sin, nsa_w_in[j], nsa_gate_b[j], nsa_cmp_pos[j], nsa_cmp_w1[j],
                             nsa_cmp_w2[j], nsa_w_out[j], lg(1), lb(1), bsz=bsz, seq=seq)
        elif kind == 1:
            h = conv_mixer_ln(h, conv_w_in[j].astype(BF16), conv_w[j], conv_w_out[j].astype(BF16),
                              lg(1), lb(1), seq=seq)
        else:
            h = gla_mixer_ln(h, gla_w_in[j], gla_w_a2[j], gla_b_a[j], gla_norm_g[j], gla_w_out[j],
                             lg(1), lb(1), bsz=bsz, seq=seq)
        h = ffn_ln(h, ffn_in, ffn_out, lg(2), lb(2), i, 1)
    return h.reshape(bsz, seq, d)
```

```python
import functools

import numpy as np
import jax
import jax.numpy as jnp
from jax import lax
from jax.experimental import pallas as pl
from jax.experimental.pallas import tpu as pltpu

F32 = jnp.float32
BF16 = jnp.bfloat16

D_MODEL = 2048
DEPTH = 4
N_MIXERS = 3
DEEPNORM_ALPHA = (2.0 * DEPTH) ** 0.25
LN_EPS = 1e-5
MACARON_WEIGHT = 0.5
D_FF = 5632

NSA_HEADS = 16
NSA_KV_GROUPS = 4
NSA_REP = NSA_HEADS // NSA_KV_GROUPS
NSA_HEAD_DIM = D_MODEL // NSA_HEADS
NSA_Q = NSA_HEADS * NSA_HEAD_DIM
NSA_KV = NSA_KV_GROUPS * NSA_HEAD_DIM
CMP_BLOCK = 32
CMP_STRIDE = 16
SEL_BLOCK = 64
SEL_TOPK = 16
WINDOW = 512
ROPE_THETA = 10000.0
NEG = -1e30
FORCE = 1e3
LOG2E = float(np.log2(np.e))

CONV_WIDTH = 3

GLA_HEADS = 4
GLA_KEY_DIM = D_MODEL // 2
GLA_VAL_DIM = D_MODEL
GLA_HEAD_K = GLA_KEY_DIM // GLA_HEADS
GLA_HEAD_V = GLA_VAL_DIM // GLA_HEADS
GLA_GATE_RANK = 16
GLA_GATE_NORM = 16.0
GLA_CHUNK = 64
GLA_HEADS_PER_STEP = 2

LANES = 128
PROJ_SLABS = 4
OUT_SLABS = 2
PROJ_WIDE = 1024
VMEM_LIMIT = 56 * 1024 * 1024


def _params(sem):
    return pltpu.CompilerParams(dimension_semantics=sem, vmem_limit_bytes=VMEM_LIMIT)


def _layer_norm(y, g, b):
    mu = jnp.mean(y, axis=-1, keepdims=True)
    d = y - mu
    var = jnp.mean(d * d, axis=-1, keepdims=True)
    return d * lax.rsqrt(var + LN_EPS) * g + b


def _silu(h):
    return h * jax.nn.sigmoid(h)


def _dot(a, b):
    return jnp.dot(a, b, preferred_element_type=F32)


def _dot_nt(a, b):
    return lax.dot_general(a, b, (((1,), (1,)), ((), ())), preferred_element_type=F32)


def _dot_tn(a, b):
    return lax.dot_general(a, b, (((0,), (0,)), ((), ())), preferred_element_type=F32)


def _split3(x):
    hi = x.astype(BF16)
    r1 = x - hi.astype(F32)
    mid = r1.astype(BF16)
    lo = (r1 - mid.astype(F32)).astype(BF16)
    return hi, mid, lo


def _ffn_kernel(x_ref, wga_ref, wua_ref, woa_ref, wgb_ref, wub_ref, wob_ref, g_ref, b_ref, o_ref,
                xb_ref, acc_ref, *, n_ff_tiles):
    j = pl.program_id(1)

    def ff_tile(xb, wg_ref, wu_ref, wo_ref):
        h = _dot(xb, wg_ref[...])
        u = _dot(xb, wu_ref[...])
        return _dot((_silu(h) * u).astype(BF16), wo_ref[...])

    def ff_pair(xb):
        return ff_tile(xb, wga_ref, wua_ref, woa_ref) + ff_tile(xb, wgb_ref, wub_ref, wob_ref)

    first = j == 0
    paired = 2 * j + 1 < n_ff_tiles

    @pl.when(first)
    def _():
        xb = x_ref[...].astype(BF16)
        xb_ref[...] = xb
        acc_ref[...] = ff_pair(xb)

    @pl.when(jnp.logical_not(first) & paired)
    def _():
        acc_ref[...] += ff_pair(xb_ref[...])

    @pl.when(jnp.logical_not(first) & jnp.logical_not(paired))
    def _():
        acc_ref[...] += ff_tile(xb_ref[...], wga_ref, wua_ref, woa_ref)

    @pl.when(j == pl.num_programs(1) - 1)
    def _():
        y = DEEPNORM_ALPHA * x_ref[...] + MACARON_WEIGHT * acc_ref[...]
        o_ref[...] = _layer_norm(y, g_ref[...], b_ref[...])


def ffn_ln(x, w_in, w_out, g, b, layer, half, *, tm=512, tf=512):
    m, d = x.shape
    ff = w_out.shape[2]
    nt = ff // tf
    assert nt >= 2
    nj = (nt + 1) // 2

    def tile_a(j):
        return 2 * j

    def tile_b(j):
        return jnp.minimum(2 * j + 1, nt - 1)

    def w_specs(tile):
        return [
            pl.BlockSpec((None, None, d, tf), lambda i, j: (layer, half, 0, tile(j))),
            pl.BlockSpec((None, None, d, tf), lambda i, j: (layer, half, 0, tile(j) + nt)),
            pl.BlockSpec((None, None, tf, d), lambda i, j: (layer, half, tile(j), 0)),
        ]

    return pl.pallas_call(
        functools.partial(_ffn_kernel, n_ff_tiles=nt),
        grid=(m // tm, nj),
        in_specs=[pl.BlockSpec((tm, d), lambda i, j: (i, 0))] + w_specs(tile_a) + w_specs(tile_b) + [
            pl.BlockSpec((1, d), lambda i, j: (0, 0)),
            pl.BlockSpec((1, d), lambda i, j: (0, 0)),
        ],
        out_specs=pl.BlockSpec((tm, d), lambda i, j: (i, 0)),
        out_shape=jax.ShapeDtypeStruct((m, d), F32),
        scratch_shapes=[pltpu.VMEM((tm, d), BF16), pltpu.VMEM((tm, d), F32)],
        compiler_params=_params(("parallel", "arbitrary")),
        name="ffn_ln",
    )(x, w_in, w_in, w_out, w_in, w_in, w_out, g, b)


def _proj_kernel(x_ref, w_ref, cos_ref, sin_ref, o_ref, xb_ref, *, n_rope, n_scale, scale):
    j = pl.program_id(1)

    @pl.when(j == 0)
    def _():
        xb_ref[...] = x_ref[...].astype(BF16)

    tm, tn = o_ref.shape
    slabs = [slice(r * tm // PROJ_SLABS, (r + 1) * tm // PROJ_SLABS) for r in range(PROJ_SLABS)]
    ys = [_dot(xb_ref[rs, :], w_ref[...]) for rs in slabs]

    if n_rope > 0:
        rotary = j < n_rope
        sc = jnp.where(j < n_scale, scale, 1.0).astype(F32)
        for rs, y in zip(slabs, ys):
            cos = jnp.where(rotary, cos_ref[rs, :], 1.0)
            sin = jnp.where(rotary, sin_ref[rs, :], 0.0)
            for hh in range(tn // LANES):
                t = y[:, hh * LANES:(hh + 1) * LANES]
                rot = pltpu.roll(t, LANES // 2, axis=1)
                o_ref[rs, hh * LANES:(hh + 1) * LANES] = ((t * cos + rot * sin) * sc).astype(o_ref.dtype)
    else:
        for rs, y in zip(slabs, ys):
            o_ref[rs, :] = y.astype(o_ref.dtype)


def proj(x, w, cos, sin, *, n_rope=0, n_scale=0, scale=1.0, out_dtype=F32, tm=1024, tn=512):
    m, d = x.shape
    n = w.shape[1]
    kern = functools.partial(_proj_kernel, n_rope=n_rope, n_scale=n_scale, scale=scale)
    return pl.pallas_call(
        kern,
        grid=(m // tm, n // tn),
        in_specs=[
            pl.BlockSpec((tm, d), lambda i, j: (i, 0)),
            pl.BlockSpec((d, tn), lambda i, j: (0, j)),
            pl.BlockSpec((tm, LANES), lambda i, j: (i, 0)),
            pl.BlockSpec((tm, LANES), lambda i, j: (i, 0)),
        ],
        out_specs=pl.BlockSpec((tm, tn), lambda i, j: (i, j)),
        out_shape=jax.ShapeDtypeStruct((m, n), out_dtype),
        scratch_shapes=[pltpu.VMEM((tm, d), BF16)],
        compiler_params=_params(("parallel", "arbitrary")),
        name="proj",
    )(x, w, cos, sin)


def _rope_table_kernel(pos_ref, inv_ref, sign_ref, cos_ref, sin_ref):
    ang = pos_ref[...] * inv_ref[...]
    cos_ref[...] = jnp.cos(ang)
    sin_ref[...] = jnp.sin(ang) * sign_ref[...]


def rope_tables(pos_f32, *, tm=2048):
    m = pos_f32.shape[0]
    hd = NSA_HEAD_DIM
    inv = ROPE_THETA ** (-jnp.arange(0, hd, 2, dtype=F32) / hd)
    inv_full = jnp.concatenate([inv, inv])[None, :]
    sign = jnp.concatenate([-jnp.ones((hd // 2,), F32), jnp.ones((hd // 2,), F32)])[None, :]
    return pl.pallas_call(
        _rope_table_kernel,
        grid=(m // tm,),
        in_specs=[
            pl.BlockSpec((tm, 1), lambda i: (i, 0)),
            pl.BlockSpec((1, hd), lambda i: (0, 0)),
            pl.BlockSpec((1, hd), lambda i: (0, 0)),
        ],
        out_specs=[pl.BlockSpec((tm, hd), lambda i: (i, 0))] * 2,
        out_shape=[jax.ShapeDtypeStruct((m, hd), F32)] * 2,
        compiler_params=_params(("parallel",)),
        name="rope_tables",
    )(pos_f32, inv_full, sign)


def _gate_kernel(x_ref, w_ref, b_ref, o_ref):
    z = _dot(x_ref[...].astype(BF16), w_ref[...]) + b_ref[...]
    o_ref[...] = jax.nn.sigmoid(z)


def gate_proj(x, w, b, *, tm=1024):
    m, d = x.shape
    n = w.shape[1]
    return pl.pallas_call(
        _gate_kernel,
        grid=(m // tm,),
        in_specs=[
            pl.BlockSpec((tm, d), lambda i: (i, 0)),
            pl.BlockSpec((d, n), lambda i: (0, 0)),
            pl.BlockSpec((1, n), lambda i: (0, 0)),
        ],
        out_specs=pl.BlockSpec((tm, n), lambda i: (i, 0)),
        out_shape=jax.ShapeDtypeStruct((m, n), F32),
        compiler_params=_params(("parallel",)),
        name="gate_proj",
    )(x, w, b)


def _out_ln_kernel(a_ref, w_ref, x_ref, g_ref, b_ref, o_ref):
    tm = a_ref.shape[0]
    slabs = [slice(r * tm // OUT_SLABS, (r + 1) * tm // OUT_SLABS) for r in range(OUT_SLABS)]
    ys = [_dot(a_ref[rs, :], w_ref[...]) for rs in slabs]
    for rs, y in zip(slabs, ys):
        o_ref[rs, :] = _layer_norm(DEEPNORM_ALPHA * x_ref[rs, :] + y, g_ref[...], b_ref[...])


def out_ln(a, w, x, g, b, *, tm=512):
    m, d = x.shape
    k = a.shape[1]
    return pl.pallas_call(
        _out_ln_kernel,
        grid=(m // tm,),
        in_specs=[
            pl.BlockSpec((tm, k), lambda i: (i, 0)),
            pl.BlockSpec((k, d), lambda i: (0, 0)),
            pl.BlockSpec((tm, d), lambda i: (i, 0)),
            pl.BlockSpec((1, d), lambda i: (0, 0)),
            pl.BlockSpec((1, d), lambda i: (0, 0)),
        ],
        out_specs=pl.BlockSpec((tm, d), lambda i: (i, 0)),
        out_shape=jax.ShapeDtypeStruct((m, d), F32),
        compiler_params=_params(("parallel",)),
        name="out_ln",
    )(a, w, x, g, b)


def _gelu_tanh(x):
    c = float(np.sqrt(2.0 / np.pi))
    return 0.5 * x * (1.0 + jnp.tanh(c * (x + 0.044715 * (x * x * x))))


def _cmp_kernel(kc_ref, vc_ref, pos_ref, w1_ref, w2_ref, ko_ref, vo_ref):
    seq, hd = kc_ref.shape
    nrow = seq // CMP_STRIDE
    half = CMP_STRIDE * hd
    for idx, (src, dst) in enumerate(((kc_ref, ko_ref), (vc_ref, vo_ref))):
        first, second = [], []
        for l in range(CMP_STRIDE):
            xl = src[pl.ds(l, nrow, stride=CMP_STRIDE), :]
            first.append(xl + pos_ref[idx, l:l + 1, :])
            second.append(xl + pos_ref[idx, CMP_STRIDE + l:CMP_STRIDE + l + 1, :])
        ya = _dot(jnp.concatenate(first, axis=1).astype(BF16), w1_ref[idx, 0:half, :])
        yb = _dot(jnp.concatenate(second, axis=1).astype(BF16), w1_ref[idx, half:2 * half, :])
        h = _gelu_tanh(ya + pltpu.roll(yb, nrow - 1, axis=0))
        out = _dot(h.astype(BF16), w2_ref[idx])
        if idx == 0:
            dst[0, 0] = out.astype(BF16)
        else:
            dst[0, 0] = out.astype(BF16).astype(F32).T.astype(BF16)


def nsa_compress(pb, pos, w1, w2, *, bsz, seq):
    g = NSA_KV_GROUPS
    hd = NSA_HEAD_DIM
    nrow = seq // CMP_STRIDE
    oblk = pl.BlockSpec((1, 1, nrow, hd), lambda b, gg: (b, gg, 0, 0))
    return pl.pallas_call(
        _cmp_kernel,
        grid=(bsz, g),
        in_specs=[
            pl.BlockSpec((seq, hd), lambda b, gg: (b, gg)),
            pl.BlockSpec((seq, hd), lambda b, gg: (b, g + gg)),
            pl.BlockSpec(pos.shape, lambda b, gg: (0, 0, 0)),
            pl.BlockSpec(w1.shape, lambda b, gg: (0, 0, 0)),
            pl.BlockSpec(w2.shape, lambda b, gg: (0, 0, 0)),
        ],
        out_specs=[oblk, oblk],
        out_shape=[jax.ShapeDtypeStruct((bsz, g, nrow, hd), BF16)] * 2,
        compiler_params=_params(("parallel", "parallel")),
        name="nsa_compress",
    )(pb, pb, pos, w1, w2)


def _nsa_attn_kernel(*refs, tq, tk, seq):
    i = pl.program_id(2)
    for tile in range(seq // tq):
        @pl.when(i == tile)
        def _(tile=tile):
            _nsa_attn_tile(tile, *refs, tq=tq, tk=tk, seq=seq)


def _nsa_attn_tile(i, q_ref, ks_ref, kw_ref, vs_ref, vw_ref, kc_ref, vct_ref, gate_ref,
                   ovt_ref, hot_ref, o_ref, ksa, vst, kwp, vwt, *, tq, tk, seq):
    hd = NSA_HEAD_DIM
    rep = NSA_REP
    cols = rep * tq
    nblk = seq // tq
    wblk = WINDOW // tq

    @pl.when(i == 0)
    def _():
        ksa[:, 0:hd] = ks_ref[...]
        ksa[:, hd:2 * hd] = hot_ref[...]
        kwp[0:WINDOW, :] = jnp.zeros((WINDOW, hd), BF16)
        kwp[WINDOW:WINDOW + seq, :] = kw_ref[...]
        for blk in range(wblk):
            vwt[blk] = jnp.zeros((hd, tq), BF16)
        for blk in range(nblk):
            rs = slice(blk * tq, (blk + 1) * tq)
            vst[blk] = vs_ref[rs, :].astype(F32).T.astype(BF16)
            vwt[wblk + blk] = vw_ref[rs, :].astype(F32).T.astype(BF16)

    t0 = i * tq
    blk0 = i
    q_t = jnp.concatenate([q_ref[:, r * hd:(r + 1) * hd].astype(F32).T for r in range(rep)],
                          axis=1).astype(BF16)

    def lane_t(shape):
        return t0 + (lax.broadcasted_iota(jnp.int32, shape, 1) & (tq - 1))

    def softmax_cols(s):
        mx = jnp.max(s, axis=0, keepdims=True)
        p = jnp.exp2(s - mx)
        return mx, p, jnp.sum(p, axis=0, keepdims=True)

    ku = lax.broadcasted_iota(jnp.int32, (tq, cols), 0)
    qu = lax.broadcasted_iota(jnp.int32, (tq, cols), 1) & (tq - 1)
    bias_diag = jnp.where(ku <= qu, 0.0, NEG)
    bias_first = jnp.where(ku > qu, 0.0, NEG)

    ncmp = kc_ref.shape[2]
    band = WINDOW + tq
    dstart = t0
    s = _dot(kc_ref[0, 0], q_t)
    diag_forced = tq <= 2 * SEL_BLOCK
    if diag_forced:
        sd = _dot(ksa[pl.ds(dstart, tq), 0:hd], q_t) + bias_diag
    sw = _dot(kwp[pl.ds(dstart, band), :], q_t)

    n_idx = lax.broadcasted_iota(jnp.int32, (ncmp, cols), 0)
    valid = n_idx * CMP_STRIDE + (CMP_BLOCK - 1) <= lane_t((ncmp, cols))
    sm = jnp.where(valid, s, NEG)
    mx = jnp.max(sm, axis=0, keepdims=True)
    p = jnp.where(valid, jnp.exp2(sm - mx), 0.0)
    den = jnp.sum(p, axis=0, keepdims=True)
    p = p * (1.0 / jnp.where(den > 0.0, den, 1.0))
    o_cmp = _dot(vct_ref[0, 0], p.astype(BF16))

    pieces = []
    for jb in range(wblk + 1):
        piece = sw[jb * tq:(jb + 1) * tq]
        if jb == wblk:
            piece = piece + bias_diag
        else:
            piece = piece + jnp.where(t0 - WINDOW + jb * tq < 0, NEG, 0.0).astype(F32)
            if jb == 0:
                piece = piece + bias_first
        pieces.append(piece)
    _, pw, l_w = softmax_cols(jnp.concatenate(pieces, axis=0))
    vw_t = jnp.concatenate([vwt[blk0 + k] for k in range(wblk + 1)], axis=1)
    o_win = _dot(vw_t, pw.astype(BF16)) * (1.0 / l_w)

    psum = p[:, 0:tq]
    for r in range(1, rep):
        psum = psum + p[:, r * tq:(r + 1) * tq]
    hi, mid, lo = _split3(psum)
    ovt = ovt_ref[...]
    imp = _dot(ovt, hi) + _dot(ovt, mid) + _dot(ovt, lo)
    n_sel = seq // SEL_BLOCK
    b_idx = lax.broadcasted_iota(jnp.int32, (n_sel, tq), 0)
    cur = (t0 + lax.broadcasted_iota(jnp.int32, (n_sel, tq), 1)) // SEL_BLOCK
    forced = (b_idx == 0) | (b_idx == cur) | (b_idx == cur - 1)
    score = jnp.where(b_idx <= cur, imp[0:n_sel] + jnp.where(forced, FORCE, 0.0), NEG)
    rank = jnp.zeros((n_sel, tq), F32)
    for mp in range(n_sel):
        row = score[mp:mp + 1, :]
        beats = (row > score) | ((row == score) & (b_idx > mp))
        rank = rank + jnp.where(beats, 1.0, 0.0)
    selected = rank < float(SEL_TOPK)
    before_tile = b_idx < t0 // SEL_BLOCK

    def augment(keep):
        bias_blk = jnp.where(keep, 0.0, NEG)
        bias_blk = jnp.concatenate([bias_blk, jnp.zeros((hd - n_sel, tq), F32)], axis=0).astype(BF16)
        return jnp.concatenate([q_t, jnp.concatenate([bias_blk] * rep, axis=1)], axis=0)

    q_aug = augment(selected & before_tile)
    if not diag_forced:
        sd = _dot(ksa[pl.ds(dstart, tq), :], augment(selected & jnp.logical_not(before_tile))) + bias_diag

    per = tk // tq
    n_chunks = (t0 + tk - 1) // tk
    scores = [sd] + [_dot(ksa[c * tk:(c + 1) * tk, :], q_aug) for c in range(n_chunks)]
    _, ps, l_s = softmax_cols(jnp.concatenate(scores, axis=0))
    v_all = jnp.concatenate(
        [vst[blk0]] + [vst[c * per + k] for c in range(n_chunks) for k in range(per)], axis=1)
    o_sel = _dot(v_all, ps.astype(BF16)) * (1.0 / l_s)

    g_t = gate_ref[...].T
    for r in range(rep):
        cs = slice(r * tq, (r + 1) * tq)
        o_r = (g_t[r:r + 1, :] * o_cmp[:, cs]
               + g_t[rep + r:rep + r + 1, :] * o_sel[:, cs]
               + g_t[2 * rep + r:2 * rep + r + 1, :] * o_win[:, cs])
        o_ref[:, r * hd:(r + 1) * hd] = o_r.T.astype(BF16)


def nsa_attention(pa, kcmp, vcmp_t, gates, overlap_t, onehot, *, bsz, seq, tq=512, tk=512):
    m = pa.shape[0]
    hd = NSA_HEAD_DIM
    g = NSA_KV_GROUPS
    nq = seq // tq
    gw = NSA_REP * hd
    c_ks = NSA_Q // hd
    c_kw = (NSA_Q + NSA_KV) // hd
    c_vs = (NSA_Q + 2 * NSA_KV) // hd
    c_vw = (NSA_Q + 3 * NSA_KV) // hd
    ncmp = kcmp.shape[2]

    def kv_spec(c0):
        return pl.BlockSpec((seq, hd), lambda b, gg, i: (b, c0 + gg))

    kern = functools.partial(_nsa_attn_kernel, tq=tq, tk=tk, seq=seq)
    return pl.pallas_call(
        kern,
        grid=(bsz, g, nq),
        in_specs=[
            pl.BlockSpec((tq, gw), lambda b, gg, i: (b * nq + i, gg)),
            kv_spec(c_ks), kv_spec(c_kw), kv_spec(c_vs), kv_spec(c_vw),
            pl.BlockSpec((1, 1, ncmp, hd), lambda b, gg, i: (b, gg, 0, 0)),
            pl.BlockSpec((1, 1, hd, ncmp), lambda b, gg, i: (b, gg, 0, 0)),
            pl.BlockSpec((tq, LANES), lambda b, gg, i: (b * nq + i, gg)),
            pl.BlockSpec(overlap_t.shape, lambda b, gg, i: (0, 0)),
            pl.BlockSpec(onehot.shape, lambda b, gg, i: (0, 0)),
        ],
        out_specs=pl.BlockSpec((tq, gw), lambda b, gg, i: (b * nq + i, gg)),
        out_shape=jax.ShapeDtypeStruct((m, NSA_Q), BF16),
        scratch_shapes=[
            pltpu.VMEM((seq, 2 * hd), BF16),
            pltpu.VMEM((seq // tq, hd, tq), BF16),
            pltpu.VMEM((seq + WINDOW, hd), BF16),
            pltpu.VMEM((seq // tq + WINDOW // tq, hd, tq), BF16),
        ],
        compiler_params=_params(("parallel", "parallel", "arbitrary")),
        name="nsa_attention",
    )(pa, pa, pa, pa, pa, kcmp, vcmp_t, gates, overlap_t, onehot)


def _nsa_constants(seq):
    mm = np.arange(LANES)[:, None]
    n = np.arange(LANES)[None, :]
    c_start = n * CMP_STRIDE
    s_start = mm * SEL_BLOCK
    n_cmp = (seq - CMP_BLOCK) // CMP_STRIDE + 1
    overlap_t = ((c_start < s_start + SEL_BLOCK) & (c_start + CMP_BLOCK > s_start)
                 & (n < n_cmp) & (mm < seq // SEL_BLOCK))
    onehot = (np.arange(seq)[:, None] // SEL_BLOCK) == np.arange(LANES)[None, :]
    return (jnp.asarray(overlap_t.astype(np.float32), dtype=BF16),
            jnp.asarray(onehot.astype(np.float32), dtype=BF16))


def nsa_mixer_ln(x, cos, sin, w_in, gate_b, cmp_pos, cmp_w1, cmp_w2, w_out, g, b, *, bsz, seq):
    hd = NSA_HEAD_DIM
    grp = NSA_KV_GROUPS
    rep = NSA_REP
    kv = NSA_KV

    def wcols(order):
        return jnp.concatenate([w_in[:, NSA_Q + i * kv:NSA_Q + (i + 1) * kv] for i in order], axis=1)

    w_a = jnp.concatenate([w_in[:, :NSA_Q], wcols((2, 4, 3, 5))], axis=1).astype(BF16)
    pa = proj(x, w_a, cos, sin, n_rope=(NSA_Q + 2 * kv) // PROJ_WIDE, n_scale=NSA_Q // PROJ_WIDE,
              scale=hd ** -0.5 * LOG2E, out_dtype=BF16, tn=PROJ_WIDE)
    pb = proj(x, wcols((0, 1)).astype(BF16), cos, sin, n_rope=1, tn=kv)

    w_gl = w_in[:, NSA_Q + 6 * kv:].reshape(-1, 3, grp, rep).transpose(0, 2, 1, 3).reshape(-1, grp, 3 * rep)
    w_gl = jnp.pad(w_gl, ((0, 0), (0, 0), (0, LANES - 3 * rep))).reshape(-1, grp * LANES).astype(BF16)
    b_gl = gate_b.reshape(3, grp, rep).transpose(1, 0, 2).reshape(grp, 3 * rep)
    b_gl = jnp.pad(b_gl, ((0, 0), (0, LANES - 3 * rep))).reshape(1, grp * LANES)
    gates = gate_proj(x, w_gl, b_gl)

    kcmp, vcmp_t = nsa_compress(pb, cmp_pos, cmp_w1.astype(BF16), cmp_w2.astype(BF16), bsz=bsz, seq=seq)
    overlap_t, onehot = _nsa_constants(seq)
    o = nsa_attention(pa, kcmp, vcmp_t, gates, overlap_t, onehot, bsz=bsz, seq=seq)
    return out_ln(o, w_out.astype(BF16), x, g, b)


def _conv_kernel(x_ref, wba_ref, wca_ref, wha_ref, cwa_ref, woa_ref, wbb_ref, wcb_ref, whb_ref, cwb_ref,
                 wob_ref, g_ref, b_ref, o_ref, xb_ref, acc_ref, tail_ref, *, tiles_per_seq):
    i = pl.program_id(0)
    j = pl.program_id(1)
    seq_start = (i % tiles_per_seq) == 0

    def channel_tile(xb, t, wb_ref, wc_ref, wh_ref, cw_ref, wo_ref):
        bg = _dot(xb, wb_ref[...])
        u = _dot(xb, wc_ref[...]) * _dot(xb, wh_ref[...])
        tm = u.shape[0]
        prev = jnp.where(seq_start, 0.0, tail_ref[t])
        tail_ref[t] = u[tm - 8:tm, :]
        rid = lax.broadcasted_iota(jnp.int32, u.shape, 0)
        u1 = jnp.where(rid == 0, prev[7:8, :], pltpu.roll(u, 1, axis=0))
        u2 = pltpu.roll(u, 2, axis=0)
        u2 = jnp.where(rid == 0, prev[6:7, :], jnp.where(rid == 1, prev[7:8, :], u2))
        cw = cw_ref[...]
        y = cw[0:1, :] * u2 + cw[1:2, :] * u1 + cw[2:3, :] * u
        return _dot((bg * y).astype(BF16), wo_ref[...])

    def pair(xb):
        return (channel_tile(xb, 2 * j, wba_ref, wca_ref, wha_ref, cwa_ref, woa_ref)
                + channel_tile(xb, 2 * j + 1, wbb_ref, wcb_ref, whb_ref, cwb_ref, wob_ref))

    @pl.when(j == 0)
    def _():
        xb = x_ref[...].astype(BF16)
        xb_ref[...] = xb
        acc_ref[...] = pair(xb)

    @pl.when(j > 0)
    def _():
        acc_ref[...] += pair(xb_ref[...])

    @pl.when(j == pl.num_programs(1) - 1)
    def _():
        yy = DEEPNORM_ALPHA * x_ref[...] + acc_ref[...]
        o_ref[...] = _layer_norm(yy, g_ref[...], b_ref[...])


def conv_mixer_ln(x, w_in, conv_w, w_out, g, b, *, seq, tm=512, tn=256):
    m, d = x.shape
    nt = d // tn
    assert nt % 2 == 0
    kern = functools.partial(_conv_kernel, tiles_per_seq=seq // tm)

    def tile_specs(off):
        return [
            pl.BlockSpec((d, tn), lambda i, j: (0, 2 * j + off)),
            pl.BlockSpec((d, tn), lambda i, j: (0, 2 * j + off + nt)),
            pl.BlockSpec((d, tn), lambda i, j: (0, 2 * j + off + 2 * nt)),
            pl.BlockSpec((CONV_WIDTH, tn), lambda i, j: (0, 2 * j + off)),
            pl.BlockSpec((tn, d), lambda i, j: (2 * j + off, 0)),
        ]

    return pl.pallas_call(
        kern,
        grid=(m // tm, nt // 2),
        in_specs=[pl.BlockSpec((tm, d), lambda i, j: (i, 0))] + tile_specs(0) + tile_specs(1) + [
            pl.BlockSpec((1, d), lambda i, j: (0, 0)),
            pl.BlockSpec((1, d), lambda i, j: (0, 0)),
        ],
        out_specs=pl.BlockSpec((tm, d), lambda i, j: (i, 0)),
        out_shape=jax.ShapeDtypeStruct((m, d), F32),
        scratch_shapes=[pltpu.VMEM((tm, d), BF16), pltpu.VMEM((tm, d), F32),
                        pltpu.VMEM((nt, 8, tn), F32)],
        compiler_params=_params(("arbitrary", "arbitrary")),
        name="conv_mixer_ln",
    )(x, w_in, w_in, w_in, conv_w, w_out, w_in, w_in, w_in, conv_w, w_out, g, b)


def _log_sigmoid(z):
    return jnp.minimum(z, 0.0) - jnp.log1p(jnp.exp(-jnp.abs(z)))


def _gla_kernel(q_ref, k_ref, v_ref, r_ref, a_ref, wa_ref, ba_ref, ng_ref, o_ref, st_ref, upd_ref):
    i = pl.program_id(2)
    c_sz = GLA_CHUNK
    dk, dv = GLA_HEAD_K, GLA_HEAD_V
    n_c = q_ref.shape[0] // c_sz
    heads = range(GLA_HEADS_PER_STEP)

    @pl.when(i == 0)
    def _():
        st_ref[...] = jnp.zeros_like(st_ref)

    rr = lax.broadcasted_iota(jnp.int32, (c_sz, c_sz), 0)
    cc = lax.broadcasted_iota(jnp.int32, (c_sz, c_sz), 1)
    causal = cc <= rr
    tril = jnp.where(causal, 1.0, 0.0).astype(BF16)
    chunks = [slice(c * c_sz, (c + 1) * c_sz) for c in range(n_c)]

    z = _dot(a_ref[...].astype(BF16), wa_ref[...]) + ba_ref[...]
    gk = _log_sigmoid(z) / GLA_GATE_NORM
    hi, mid, lo = _split3(gk)
    bcum = jnp.concatenate(
        [_dot(tril, hi[rs]) + _dot(tril, mid[rs]) + _dot(tril, lo[rs]) for rs in chunks], axis=0)
    b_last = [bcum[rs][c_sz - 1:c_sz, :] for rs in chunks]
    b_last_rows = jnp.concatenate([jnp.broadcast_to(bl, (c_sz, bl.shape[1])) for bl in b_last], axis=0)
    qd = (q_ref[...] * (dk ** -0.5) * jnp.exp(bcum)).astype(BF16)
    k = k_ref[...]
    kd = (k * jnp.exp(-bcum)).astype(BF16)
    kl = (k * jnp.exp(b_last_rows - bcum)).astype(BF16)
    v = v_ref[...].astype(BF16)

    o_intra = []
    for hh in heads:
        ks, vs = slice(hh * dk, (hh + 1) * dk), slice(hh * dv, (hh + 1) * dv)
        outs = []
        for c, rs in enumerate(chunks):
            att = jnp.where(causal, _dot_nt(qd[rs, ks], kd[rs, ks]), 0.0).astype(BF16)
            outs.append(_dot(att, v[rs, vs]))
            upd_ref[hh, c] = _dot_tn(v[rs, vs], kl[rs, ks])
        o_intra.append(outs)

    st = [st_ref[hh] for hh in heads]
    ng = ng_ref[...]
    for c, rs in enumerate(chunks):
        for hh in heads:
            ks, vs = slice(hh * dk, (hh + 1) * dk), slice(hh * dv, (hh + 1) * dv)
            o = o_intra[hh][c] + _dot_nt(qd[rs, ks], st[hh].astype(BF16))
            st[hh] = st[hh] * jnp.exp(b_last[c][:, ks]) + upd_ref[hh, c]
            o = o * lax.rsqrt(jnp.mean(o * o, axis=-1, keepdims=True) + LN_EPS) * ng
            o_ref[rs, vs] = (o * _silu(r_ref[rs, vs])).astype(BF16)
    for hh in heads:
        st_ref[hh] = st[hh]


def gla_scan(p, a, wa, ba, ng, *, bsz, seq, tt=512):
    m = p.shape[0]
    dk, dv = GLA_HEAD_K, GLA_HEAD_V
    hp = GLA_HEADS_PER_STEP
    nt = seq // tt
    c_k = GLA_KEY_DIM // (hp * dk)
    c_v = 2 * GLA_KEY_DIM // (hp * dv)
    c_r = (2 * GLA_KEY_DIM + GLA_VAL_DIM) // (hp * dv)
    row = lambda b, h, i: b * nt + i
    return pl.pallas_call(
        _gla_kernel,
        grid=(bsz, GLA_HEADS // hp, nt),
        in_specs=[
            pl.BlockSpec((tt, hp * dk), lambda b, h, i: (row(b, h, i), h)),
            pl.BlockSpec((tt, hp * dk), lambda b, h, i: (row(b, h, i), c_k + h)),
            pl.BlockSpec((tt, hp * dv), lambda b, h, i: (row(b, h, i), c_v + h)),
            pl.BlockSpec((tt, hp * dv), lambda b, h, i: (row(b, h, i), c_r + h)),
            pl.BlockSpec((tt, LANES), lambda b, h, i: (row(b, h, i), 0)),
            pl.BlockSpec((LANES, hp * dk), lambda b, h, i: (0, h)),
            pl.BlockSpec((1, hp * dk), lambda b, h, i: (0, h)),
            pl.BlockSpec((1, dv), lambda b, h, i: (0, 0)),
        ],
        out_specs=pl.BlockSpec((tt, hp * dv), lambda b, h, i: (row(b, h, i), h)),
        out_shape=jax.ShapeDtypeStruct((m, GLA_VAL_DIM), BF16),
        scratch_shapes=[pltpu.VMEM((hp, dv, dk), F32),
                        pltpu.VMEM((hp, tt // GLA_CHUNK, dv, dk), F32)],
        compiler_params=_params(("parallel", "parallel", "arbitrary")),
        name="gla_scan",
    )(p, p, p, p, a, wa, ba, ng)


def gla_mixer_ln(x, w_in, w_a2, b_a, norm_g, w_out, g, b, *, bsz, seq):
    n_main = 2 * GLA_KEY_DIM + 2 * GLA_VAL_DIM
    dummy = jnp.zeros((x.shape[0], LANES), F32)
    p = proj(x, w_in[:, :n_main].astype(BF16), dummy, dummy, tn=PROJ_WIDE)
    w_a = jnp.pad(w_in[:, n_main:], ((0, 0), (0, LANES - GLA_GATE_RANK))).astype(BF16)
    a = proj(x, w_a, dummy, dummy, tn=LANES)
    wa2 = jnp.pad(w_a2, ((0, LANES - GLA_GATE_RANK), (0, 0))).astype(BF16)
    o = gla_scan(p, a, wa2, b_a[None, :], norm_g[None, :], bsz=bsz, seq=seq)
    return out_ln(o, w_out.astype(BF16), x, g, b)


def kernel(x, positions, ln_g, ln_b, ffn_w_in, ffn_w_out, nsa_w_in, nsa_gate_b, nsa_cmp_pos,
           nsa_cmp_w1, nsa_cmp_w2, nsa_w_out, conv_w_in, conv_w, conv_w_out, gla_w_in, gla_w_a2,
           gla_b_a, gla_norm_g, gla_w_out):
    bsz, seq, d = x.shape
    m = bsz * seq
    h = x.reshape(m, d)
    cos, sin = rope_tables(positions.reshape(m, 1).astype(F32))
    ffn_in = ffn_w_in.astype(BF16)
    ffn_out = ffn_w_out.astype(BF16)
    for i in range(DEPTH):
        lg = lambda k: ln_g[i, k][None, :]
        lb = lambda k: ln_b[i, k][None, :]
        h = ffn_ln(h, ffn_in, ffn_out, lg(0), lb(0), i, 0)
        kind, j = i % N_MIXERS, i // N_MIXERS
        if kind == 0:
            h = nsa_mixer_ln(h, cos, sin, nsa_w_in[j], nsa_gate_b[j], nsa_cmp_pos[j], nsa_cmp_w1[j],
                             nsa_cmp_w2[j], nsa_w_out[j], lg(1), lb(1), bsz=bsz, seq=seq)
        elif kind == 1:
            h = conv_mixer_ln(h, conv_w_in[j].astype(BF16), conv_w[j], conv_w_out[j].astype(BF16),
                              lg(1), lb(1), seq=seq)
        else:
            h = gla_mixer_ln(h, gla_w_in[j], gla_w_a2[j], gla_b_a[j], gla_norm_g[j], gla_w_out[j],
                             lg(1), lb(1), bsz=bsz, seq=seq)
        h = ffn_ln(h, ffn_in, ffn_out, lg(2), lb(2), i, 1)
    return h.reshape(bsz, seq, d)
```

```python
import functools

import numpy as np
import jax
import jax.numpy as jnp
from jax import lax
from jax.experimental import pallas as pl
from jax.experimental.pallas import tpu as pltpu

F32 = jnp.float32
BF16 = jnp.bfloat16

D_MODEL = 2048
DEPTH = 4
N_MIXERS = 3
DEEPNORM_ALPHA = (2.0 * DEPTH) ** 0.25
LN_EPS = 1e-5
MACARON_WEIGHT = 0.5
D_FF = 5632

NSA_HEADS = 16
NSA_KV_GROUPS = 4
NSA_REP = NSA_HEADS // NSA_KV_GROUPS
NSA_HEAD_DIM = D_MODEL // NSA_HEADS
NSA_Q = NSA_HEADS * NSA_HEAD_DIM
NSA_KV = NSA_KV_GROUPS * NSA_HEAD_DIM
CMP_BLOCK = 32
CMP_STRIDE = 16
SEL_BLOCK = 64
SEL_TOPK = 16
WINDOW = 512
ROPE_THETA = 10000.0
NEG = -1e30
FORCE = 1e3
LOG2E = float(np.log2(np.e))

CONV_WIDTH = 3

GLA_HEADS = 4
GLA_KEY_DIM = D_MODEL // 2
GLA_VAL_DIM = D_MODEL
GLA_HEAD_K = GLA_KEY_DIM // GLA_HEADS
GLA_HEAD_V = GLA_VAL_DIM // GLA_HEADS
GLA_GATE_RANK = 16
GLA_GATE_NORM = 16.0
GLA_CHUNK = 64
GLA_HEADS_PER_STEP = 2

LANES = 128
PROJ_SLABS = 4
OUT_SLABS = 2
PROJ_WIDE = 1024
VMEM_LIMIT = 56 * 1024 * 1024


def _params(sem):
    return pltpu.CompilerParams(dimension_semantics=sem, vmem_limit_bytes=VMEM_LIMIT)


def _layer_norm(y, g, b):
    mu = jnp.mean(y, axis=-1, keepdims=True)
    d = y - mu
    var = jnp.mean(d * d, axis=-1, keepdims=True)
    return d * lax.rsqrt(var + LN_EPS) * g + b


def _silu(h):
    return h * jax.nn.sigmoid(h)


def _dot(a, b):
    return jnp.dot(a, b, preferred_element_type=F32)


def _dot_nt(a, b):
    return lax.dot_general(a, b, (((1,), (1,)), ((), ())), preferred_element_type=F32)


def _dot_tn(a, b):
    return lax.dot_general(a, b, (((0,), (0,)), ((), ())), preferred_element_type=F32)


def _split3(x):
    hi = x.astype(BF16)
    r1 = x - hi.astype(F32)
    mid = r1.astype(BF16)
    lo = (r1 - mid.astype(F32)).astype(BF16)
    return hi, mid, lo


def _ffn_kernel(x_ref, wga_ref, wua_ref, woa_ref, wgb_ref, wub_ref, wob_ref, g_ref, b_ref, o_ref,
                xb_ref, acc_ref, *, n_ff_tiles):
    j = pl.program_id(1)

    def ff_tile(xb, wg_ref, wu_ref, wo_ref):
        h = _dot(xb, wg_ref[...])
        u = _dot(xb, wu_ref[...])
        return _dot((_silu(h) * u).astype(BF16), wo_ref[...])

    def ff_pair(xb):
        return ff_tile(xb, wga_ref, wua_ref, woa_ref) + ff_tile(xb, wgb_ref, wub_ref, wob_ref)

    first = j == 0
    paired = 2 * j + 1 < n_ff_tiles

    @pl.when(first)
    def _():
        xb = x_ref[...].astype(BF16)
        xb_ref[...] = xb
        acc_ref[...] = ff_pair(xb)

    @pl.when(jnp.logical_not(first) & paired)
    def _():
        acc_ref[...] += ff_pair(xb_ref[...])

    @pl.when(jnp.logical_not(first) & jnp.logical_not(paired))
    def _():
        acc_ref[...] += ff_tile(xb_ref[...], wga_ref, wua_ref, woa_ref)

    @pl.when(j == pl.num_programs(1) - 1)
    def _():
        y = DEEPNORM_ALPHA * x_ref[...] + MACARON_WEIGHT * acc_ref[...]
        o_ref[...] = _layer_norm(y, g_ref[...], b_ref[...])


def ffn_ln(x, w_in, w_out, g, b, layer, half, *, tm=512, tf=512):
    m, d = x.shape
    ff = w_out.shape[2]
    nt = ff // tf
    assert nt >= 2
    nj = (nt + 1) // 2

    def tile_a(j):
        return 2 * j

    def tile_b(j):
        return jnp.minimum(2 * j + 1, nt - 1)

    def w_specs(tile):
        return [
            pl.BlockSpec((None, None, d, tf), lambda i, j: (layer, half, 0, tile(j))),
            pl.BlockSpec((None, None, d, tf), lambda i, j: (layer, half, 0, tile(j) + nt)),
            pl.BlockSpec((None, None, tf, d), lambda i, j: (layer, half, tile(j), 0)),
        ]

    return pl.pallas_call(
        functools.partial(_ffn_kernel, n_ff_tiles=nt),
        grid=(m // tm, nj),
        in_specs=[pl.BlockSpec((tm, d), lambda i, j: (i, 0))] + w_specs(tile_a) + w_specs(tile_b) + [
            pl.BlockSpec((1, d), lambda i, j: (0, 0)),
            pl.BlockSpec((1, d), lambda i, j: (0, 0)),
        ],
        out_specs=pl.BlockSpec((tm, d), lambda i, j: (i, 0)),
        out_shape=jax.ShapeDtypeStruct((m, d), F32),
        scratch_shapes=[pltpu.VMEM((tm, d), BF16), pltpu.VMEM((tm, d), F32)],
        compiler_params=_params(("parallel", "arbitrary")),
        name="ffn_ln",
    )(x, w_in, w_in, w_out, w_in, w_in, w_out, g, b)


def _proj_kernel(x_ref, w_ref, cos_ref, sin_ref, o_ref, xb_ref, *, n_rope, n_scale, scale):
    j = pl.program_id(1)

    @pl.when(j == 0)
    def _():
        xb_ref[...] = x_ref[...].astype(BF16)

    tm, tn = o_ref.shape
    slabs = [slice(r * tm // PROJ_SLABS, (r + 1) * tm // PROJ_SLABS) for r in range(PROJ_SLABS)]
    ys = [_dot(xb_ref[rs, :], w_ref[...]) for rs in slabs]

    if n_rope > 0:
        rotary = j < n_rope
        sc = jnp.where(j < n_scale, scale, 1.0).astype(F32)
        for rs, y in zip(slabs, ys):
            cos = jnp.where(rotary, cos_ref[rs, :], 1.0)
            sin = jnp.where(rotary, sin_ref[rs, :], 0.0)
            for hh in range(tn // LANES):
                t = y[:, hh * LANES:(hh + 1) * LANES]
                rot = pltpu.roll(t, LANES // 2, axis=1)
                o_ref[rs, hh * LANES:(hh + 1) * LANES] = ((t * cos + rot * sin) * sc).astype(o_ref.dtype)
    else:
        for rs, y in zip(slabs, ys):
            o_ref[rs, :] = y.astype(o_ref.dtype)


def proj(x, w, cos, sin, *, n_rope=0, n_scale=0, scale=1.0, out_dtype=F32, tm=1024, tn=512):
    m, d = x.shape
    n = w.shape[1]
    kern = functools.partial(_proj_kernel, n_rope=n_rope, n_scale=n_scale, scale=scale)
    return pl.pallas_call(
        kern,
        grid=(m // tm, n // tn),
        in_specs=[
            pl.BlockSpec((tm, d), lambda i, j: (i, 0)),
            pl.BlockSpec((d, tn), lambda i, j: (0, j)),
            pl.BlockSpec((tm, LANES), lambda i, j: (i, 0)),
            pl.BlockSpec((tm, LANES), lambda i, j: (i, 0)),
        ],
        out_specs=pl.BlockSpec((tm, tn), lambda i, j: (i, j)),
        out_shape=jax.ShapeDtypeStruct((m, n), out_dtype),
        scratch_shapes=[pltpu.VMEM((tm, d), BF16)],
        compiler_params=_params(("parallel", "arbitrary")),
        name="proj",
    )(x, w, cos, sin)


def _rope_table_kernel(pos_ref, inv_ref, sign_ref, cos_ref, sin_ref):
    ang = pos_ref[...] * inv_ref[...]
    cos_ref[...] = jnp.cos(ang)
    sin_ref[...] = jnp.sin(ang) * sign_ref[...]


def rope_tables(pos_f32, *, tm=2048):
    m = pos_f32.shape[0]
    hd = NSA_HEAD_DIM
    inv = ROPE_THETA ** (-jnp.arange(0, hd, 2, dtype=F32) / hd)
    inv_full = jnp.concatenate([inv, inv])[None, :]
    sign = jnp.concatenate([-jnp.ones((hd // 2,), F32), jnp.ones((hd // 2,), F32)])[None, :]
    return pl.pallas_call(
        _rope_table_kernel,
        grid=(m // tm,),
        in_specs=[
            pl.BlockSpec((tm, 1), lambda i: (i, 0)),
            pl.BlockSpec((1, hd), lambda i: (0, 0)),
            pl.BlockSpec((1, hd), lambda i: (0, 0)),
        ],
        out_specs=[pl.BlockSpec((tm, hd), lambda i: (i, 0))] * 2,
        out_shape=[jax.ShapeDtypeStruct((m, hd), F32)] * 2,
        compiler_params=_params(("parallel",)),
        name="rope_tables",
    )(pos_f32, inv_full, sign)


def _gate_kernel(x_ref, w_ref, b_ref, o_ref):
    z = _dot(x_ref[...].astype(BF16), w_ref[...]) + b_ref[...]
    o_ref[...] = jax.nn.sigmoid(z)


def gate_proj(x, w, b, *, tm=1024):
    m, d = x.shape
    n = w.shape[1]
    return pl.pallas_call(
        _gate_kernel,
        grid=(m // tm,),
        in_specs=[
            pl.BlockSpec((tm, d), lambda i: (i, 0)),
            pl.BlockSpec((d, n), lambda i: (0, 0)),
            pl.BlockSpec((1, n), lambda i: (0, 0)),
        ],
        out_specs=pl.BlockSpec((tm, n), lambda i: (i, 0)),
        out_shape=jax.ShapeDtypeStruct((m, n), F32),
        compiler_params=_params(("parallel",)),
        name="gate_proj",
    )(x, w, b)


def _out_ln_kernel(a_ref, w_ref, x_ref, g_ref, b_ref, o_ref):
    tm = a_ref.shape[0]
    slabs = [slice(r * tm // OUT_SLABS, (r + 1) * tm // OUT_SLABS) for r in range(OUT_SLABS)]
    ys = [_dot(a_ref[rs, :], w_ref[...]) for rs in slabs]
    for rs, y in zip(slabs, ys):
        o_ref[rs, :] = _layer_norm(DEEPNORM_ALPHA * x_ref[rs, :] + y, g_ref[...], b_ref[...])


def out_ln(a, w, x, g, b, *, tm=512):
    m, d = x.shape
    k = a.shape[1]
    return pl.pallas_call(
        _out_ln_kernel,
        grid=(m // tm,),
        in_specs=[
            pl.BlockSpec((tm, k), lambda i: (i, 0)),
            pl.BlockSpec((k, d), lambda i: (0, 0)),
            pl.BlockSpec((tm, d), lambda i: (i, 0)),
            pl.BlockSpec((1, d), lambda i: (0, 0)),
            pl.BlockSpec((1, d), lambda i: (0, 0)),
        ],
        out_specs=pl.BlockSpec((tm, d), lambda i: (i, 0)),
        out_shape=jax.ShapeDtypeStruct((m, d), F32),
        compiler_params=_params(("parallel",)),
        name="out_ln",
    )(a, w, x, g, b)


def _gelu_tanh(x):
    c = float(np.sqrt(2.0 / np.pi))
    return 0.5 * x * (1.0 + jnp.tanh(c * (x + 0.044715 * (x * x * x))))


def _cmp_kernel(kc_ref, vc_ref, pos_ref, w1_ref, w2_ref, ko_ref, vo_ref):
    seq, hd = kc_ref.shape
    nrow = seq // CMP_STRIDE
    half = CMP_STRIDE * hd
    for idx, (src, dst) in enumerate(((kc_ref, ko_ref), (vc_ref, vo_ref))):
        first, second = [], []
        for l in range(CMP_STRIDE):
            xl = src[pl.ds(l, nrow, stride=CMP_STRIDE), :]
            first.append(xl + pos_ref[idx, l:l + 1, :])
            second.append(xl + pos_ref[idx, CMP_STRIDE + l:CMP_STRIDE + l + 1, :])
        ya = _dot(jnp.concatenate(first, axis=1).astype(BF16), w1_ref[idx, 0:half, :])
        yb = _dot(jnp.concatenate(second, axis=1).astype(BF16), w1_ref[idx, half:2 * half, :])
        h = _gelu_tanh(ya + pltpu.roll(yb, nrow - 1, axis=0))
        out = _dot(h.astype(BF16), w2_ref[idx])
        if idx == 0:
            dst[0, 0] = out.astype(BF16)
        else:
            dst[0, 0] = out.astype(BF16).astype(F32).T.astype(BF16)


def nsa_compress(pb, pos, w1, w2, *, bsz, seq):
    g = NSA_KV_GROUPS
    hd = NSA_HEAD_DIM
    nrow = seq // CMP_STRIDE
    oblk = pl.BlockSpec((1, 1, nrow, hd), lambda b, gg: (b, gg, 0, 0))
    return pl.pallas_call(
        _cmp_kernel,
        grid=(bsz, g),
        in_specs=[
            pl.BlockSpec((seq, hd), lambda b, gg: (b, gg)),
            pl.BlockSpec((seq, hd), lambda b, gg: (b, g + gg)),
            pl.BlockSpec(pos.shape, lambda b, gg: (0, 0, 0)),
            pl.BlockSpec(w1.shape, lambda b, gg: (0, 0, 0)),
            pl.BlockSpec(w2.shape, lambda b, gg: (0, 0, 0)),
        ],
        out_specs=[oblk, oblk],
        out_shape=[jax.ShapeDtypeStruct((bsz, g, nrow, hd), BF16)] * 2,
        compiler_params=_params(("parallel", "parallel")),
        name="nsa_compress",
    )(pb, pb, pos, w1, w2)


def _nsa_attn_kernel(*refs, tq, tk, seq):
    i = pl.program_id(2)
    for tile in range(seq // tq):
        @pl.when(i == tile)
        def _(tile=tile):
            _nsa_attn_tile(tile, *refs, tq=tq, tk=tk, seq=seq)


def _nsa_attn_tile(i, q_ref, ks_ref, kw_ref, vs_ref, vw_ref, kc_ref, vct_ref, gate_ref,
                   ovt_ref, hot_ref, o_ref, ksa, vst, kwp, vwt, *, tq, tk, seq):
    hd = NSA_HEAD_DIM
    rep = NSA_REP
    cols = rep * tq
    nblk = seq // tq
    wblk = WINDOW // tq

    @pl.when(i == 0)
    def _():
        ksa[:, 0:hd] = ks_ref[...]
        ksa[:, hd:2 * hd] = hot_ref[...]
        kwp[0:WINDOW, :] = jnp.zeros((WINDOW, hd), BF16)
        kwp[WINDOW:WINDOW + seq, :] = kw_ref[...]
        for blk in range(wblk):
            vwt[blk] = jnp.zeros((hd, tq), BF16)
        for blk in range(nblk):
            rs = slice(blk * tq, (blk + 1) * tq)
            vst[blk] = vs_ref[rs, :].astype(F32).T.astype(BF16)
            vwt[wblk + blk] = vw_ref[rs, :].astype(F32).T.astype(BF16)

    t0 = i * tq
    blk0 = i
    q_t = jnp.concatenate([q_ref[:, r * hd:(r + 1) * hd].astype(F32).T for r in range(rep)],
                          axis=1).astype(BF16)

    def lane_t(shape):
        return t0 + (lax.broadcasted_iota(jnp.int32, shape, 1) & (tq - 1))

    def softmax_cols(s):
        mx = jnp.max(s, axis=0, keepdims=True)
        p = jnp.exp2(s - mx)
        return mx, p, jnp.sum(p, axis=0, keepdims=True)

    ku = lax.broadcasted_iota(jnp.int32, (tq, cols), 0)
    qu = lax.broadcasted_iota(jnp.int32, (tq, cols), 1) & (tq - 1)
    bias_diag = jnp.where(ku <= qu, 0.0, NEG)
    bias_first = jnp.where(ku > qu, 0.0, NEG)

    ncmp = kc_ref.shape[2]
    band = WINDOW + tq
    dstart = t0
    s = _dot(kc_ref[0, 0], q_t)
    diag_forced = tq <= 2 * SEL_BLOCK
    if diag_forced:
        sd = _dot(ksa[pl.ds(dstart, tq), 0:hd], q_t) + bias_diag
    win_blocks = [jb for jb in range(wblk + 1) if t0 - WINDOW + jb * tq >= 0]
    sw = _dot(kwp[dstart + win_blocks[0] * tq:dstart + band, :], q_t)

    n_idx = lax.broadcasted_iota(jnp.int32, (ncmp, cols), 0)
    valid = n_idx * CMP_STRIDE + (CMP_BLOCK - 1) <= lane_t((ncmp, cols))
    sm = jnp.where(valid, s, NEG)
    mx = jnp.max(sm, axis=0, keepdims=True)
    p = jnp.where(valid, jnp.exp2(sm - mx), 0.0)
    den = jnp.sum(p, axis=0, keepdims=True)
    p = p * (1.0 / jnp.where(den > 0.0, den, 1.0))
    o_cmp = _dot(vct_ref[0, 0], p.astype(BF16))

    pieces = []
    for n, jb in enumerate(win_blocks):
        piece = sw[n * tq:(n + 1) * tq]
        if jb == wblk:
            piece = piece + bias_diag
        elif jb == 0:
            piece = piece + bias_first
        pieces.append(piece)
    _, pw, l_w = softmax_cols(jnp.concatenate(pieces, axis=0))
    vw_t = jnp.concatenate([vwt[blk0 + jb] for jb in win_blocks], axis=1)
    o_win = _dot(vw_t, pw.astype(BF16)) * (1.0 / l_w)

    psum = p[:, 0:tq]
    for r in range(1, rep):
        psum = psum + p[:, r * tq:(r + 1) * tq]
    hi, mid, lo = _split3(psum)
    ovt = ovt_ref[...]
    imp = _dot(ovt, hi) + _dot(ovt, mid) + _dot(ovt, lo)
    n_sel = seq // SEL_BLOCK
    b_idx = lax.broadcasted_iota(jnp.int32, (n_sel, tq), 0)
    cur = (t0 + lax.broadcasted_iota(jnp.int32, (n_sel, tq), 1)) // SEL_BLOCK
    forced = (b_idx == 0) | (b_idx == cur) | (b_idx == cur - 1)
    score = jnp.where(b_idx <= cur, imp[0:n_sel] + jnp.where(forced, FORCE, 0.0), NEG)
    rank = jnp.zeros((n_sel, tq), F32)
    for mp in range(n_sel):
        row = score[mp:mp + 1, :]
        beats = (row > score) | ((row == score) & (b_idx > mp))
        rank = rank + jnp.where(beats, 1.0, 0.0)
    selected = rank < float(SEL_TOPK)
    before_tile = b_idx < t0 // SEL_BLOCK

    def augment(keep):
        bias_blk = jnp.where(keep, 0.0, NEG)
        bias_blk = jnp.concatenate([bias_blk, jnp.zeros((hd - n_sel, tq), F32)], axis=0).astype(BF16)
        return jnp.concatenate([q_t, jnp.concatenate([bias_blk] * rep, axis=1)], axis=0)

    q_aug = augment(selected & before_tile)
    if not diag_forced:
        sd = _dot(ksa[pl.ds(dstart, tq), :], augment(selected & jnp.logical_not(before_tile))) + bias_diag

    per = tk // tq
    n_chunks = (t0 + tk - 1) // tk
    scores = [sd] + [_dot(ksa[c * tk:(c + 1) * tk, :], q_aug) for c in range(n_chunks)]
    _, ps, l_s = softmax_cols(jnp.concatenate(scores, axis=0))
    v_all = jnp.concatenate(
        [vst[blk0]] + [vst[c * per + k] for c in range(n_chunks) for k in range(per)], axis=1)
    o_sel = _dot(v_all, ps.astype(BF16)) * (1.0 / l_s)

    g_t = gate_ref[...].T
    for r in range(rep):
        cs = slice(r * tq, (r + 1) * tq)
        o_r = (g_t[r:r + 1, :] * o_cmp[:, cs]
               + g_t[rep + r:rep + r + 1, :] * o_sel[:, cs]
               + g_t[2 * rep + r:2 * rep + r + 1, :] * o_win[:, cs])
        o_ref[:, r * hd:(r + 1) * hd] = o_r.T.astype(BF16)


def nsa_attention(pa, kcmp, vcmp_t, gates, overlap_t, onehot, *, bsz, seq, tq=512, tk=512):
    m = pa.shape[0]
    hd = NSA_HEAD_DIM
    g = NSA_KV_GROUPS
    nq = seq // tq
    gw = NSA_REP * hd
    c_ks = NSA_Q // hd
    c_kw = (NSA_Q + NSA_KV) // hd
    c_vs = (NSA_Q + 2 * NSA_KV) // hd
    c_vw = (NSA_Q + 3 * NSA_KV) // hd
    ncmp = kcmp.shape[2]

    def kv_spec(c0):
        return pl.BlockSpec((seq, hd), lambda b, gg, i: (b, c0 + gg))

    kern = functools.partial(_nsa_attn_kernel, tq=tq, tk=tk, seq=seq)
    return pl.pallas_call(
        kern,
        grid=(bsz, g, nq),
        in_specs=[
            pl.BlockSpec((tq, gw), lambda b, gg, i: (b * nq + i, gg)),
            kv_spec(c_ks), kv_spec(c_kw), kv_spec(c_vs), kv_spec(c_vw),
            pl.BlockSpec((1, 1, ncmp, hd), lambda b, gg, i: (b, gg, 0, 0)),
            pl.BlockSpec((1, 1, hd, ncmp), lambda b, gg, i: (b, gg, 0, 0)),
            pl.BlockSpec((tq, LANES), lambda b, gg, i: (b * nq + i, gg)),
            pl.BlockSpec(overlap_t.shape, lambda b, gg, i: (0, 0)),
            pl.BlockSpec(onehot.shape, lambda b, gg, i: (0, 0)),
        ],
        out_specs=pl.BlockSpec((tq, gw), lambda b, gg, i: (b * nq + i, gg)),
        out_shape=jax.ShapeDtypeStruct((m, NSA_Q), BF16),
        scratch_shapes=[
            pltpu.VMEM((seq, 2 * hd), BF16),
            pltpu.VMEM((seq // tq, hd, tq), BF16),
            pltpu.VMEM((seq + WINDOW, hd), BF16),
            pltpu.VMEM((seq // tq + WINDOW // tq, hd, tq), BF16),
        ],
        compiler_params=_params(("parallel", "parallel", "arbitrary")),
        name="nsa_attention",
    )(pa, pa, pa, pa, pa, kcmp, vcmp_t, gates, overlap_t, onehot)


def _nsa_constants(seq):
    mm = np.arange(LANES)[:, None]
    n = np.arange(LANES)[None, :]
    c_start = n * CMP_STRIDE
    s_start = mm * SEL_BLOCK
    n_cmp = (seq - CMP_BLOCK) // CMP_STRIDE + 1
    overlap_t = ((c_start < s_start + SEL_BLOCK) & (c_start + CMP_BLOCK > s_start)
                 & (n < n_cmp) & (mm < seq // SEL_BLOCK))
    onehot = (np.arange(seq)[:, None] // SEL_BLOCK) == np.arange(LANES)[None, :]
    return (jnp.asarray(overlap_t.astype(np.float32), dtype=BF16),
            jnp.asarray(onehot.astype(np.float32), dtype=BF16))


def nsa_mixer_ln(x, cos, sin, w_in, gate_b, cmp_pos, cmp_w1, cmp_w2, w_out, g, b, *, bsz, seq):
    hd = NSA_HEAD_DIM
    grp = NSA_KV_GROUPS
    rep = NSA_REP
    kv = NSA_KV

    def wcols(order):
        return jnp.concatenate([w_in[:, NSA_Q + i * kv:NSA_Q + (i + 1) * kv] for i in order], axis=1)

    w_a = jnp.concatenate([w_in[:, :NSA_Q], wcols((2, 4, 3, 5))], axis=1).astype(BF16)
    pa = proj(x, w_a, cos, sin, n_rope=(NSA_Q + 2 * kv) // PROJ_WIDE, n_scale=NSA_Q // PROJ_WIDE,
              scale=hd ** -0.5 * LOG2E, out_dtype=BF16, tn=PROJ_WIDE)
    pb = proj(x, wcols((0, 1)).astype(BF16), cos, sin, n_rope=1, tn=kv)

    w_gl = w_in[:, NSA_Q + 6 * kv:].reshape(-1, 3, grp, rep).transpose(0, 2, 1, 3).reshape(-1, grp, 3 * rep)
    w_gl = jnp.pad(w_gl, ((0, 0), (0, 0), (0, LANES - 3 * rep))).reshape(-1, grp * LANES).astype(BF16)
    b_gl = gate_b.reshape(3, grp, rep).transpose(1, 0, 2).reshape(grp, 3 * rep)
    b_gl = jnp.pad(b_gl, ((0, 0), (0, LANES - 3 * rep))).reshape(1, grp * LANES)
    gates = gate_proj(x, w_gl, b_gl)

    kcmp, vcmp_t = nsa_compress(pb, cmp_pos, cmp_w1.astype(BF16), cmp_w2.astype(BF16), bsz=bsz, seq=seq)
    overlap_t, onehot = _nsa_constants(seq)
    o = nsa_attention(pa, kcmp, vcmp_t, gates, overlap_t, onehot, bsz=bsz, seq=seq)
    return out_ln(o, w_out.astype(BF16), x, g, b)


def _conv_kernel(x_ref, wba_ref, wca_ref, wha_ref, cwa_ref, woa_ref, wbb_ref, wcb_ref, whb_ref, cwb_ref,
                 wob_ref, g_ref, b_ref, o_ref, xb_ref, acc_ref, tail_ref, *, tiles_per_seq):
    i = pl.program_id(0)
    j = pl.program_id(1)
    seq_start = (i % tiles_per_seq) == 0

    def channel_tile(xb, t, wb_ref, wc_ref, wh_ref, cw_ref, wo_ref):
        bg = _dot(xb, wb_ref[...])
        u = _dot(xb, wc_ref[...]) * _dot(xb, wh_ref[...])
        tm = u.shape[0]
        prev = jnp.where(seq_start, 0.0, tail_ref[t])
        tail_ref[t] = u[tm - 8:tm, :]
        rid = lax.broadcasted_iota(jnp.int32, u.shape, 0)
        u1 = jnp.where(rid == 0, prev[7:8, :], pltpu.roll(u, 1, axis=0))
        u2 = pltpu.roll(u, 2, axis=0)
        u2 = jnp.where(rid == 0, prev[6:7, :], jnp.where(rid == 1, prev[7:8, :], u2))
        cw = cw_ref[...]
        y = cw[0:1, :] * u2 + cw[1:2, :] * u1 + cw[2:3, :] * u
        return _dot((bg * y).astype(BF16), wo_ref[...])

    def pair(xb):
        return (channel_tile(xb, 2 * j, wba_ref, wca_ref, wha_ref, cwa_ref, woa_ref)
                + channel_tile(xb, 2 * j + 1, wbb_ref, wcb_ref, whb_ref, cwb_ref, wob_ref))

    @pl.when(j == 0)
    def _():
        xb = x_ref[...].astype(BF16)
        xb_ref[...] = xb
        acc_ref[...] = pair(xb)

    @pl.when(j > 0)
    def _():
        acc_ref[...] += pair(xb_ref[...])

    @pl.when(j == pl.num_programs(1) - 1)
    def _():
        yy = DEEPNORM_ALPHA * x_ref[...] + acc_ref[...]
        o_ref[...] = _layer_norm(yy, g_ref[...], b_ref[...])


def conv_mixer_ln(x, w_in, conv_w, w_out, g, b, *, seq, tm=512, tn=256):
    m, d = x.shape
    nt = d // tn
    assert nt % 2 == 0
    kern = functools.partial(_conv_kernel, tiles_per_seq=seq // tm)

    def tile_specs(off):
        return [
            pl.BlockSpec((d, tn), lambda i, j: (0, 2 * j + off)),
            pl.BlockSpec((d, tn), lambda i, j: (0, 2 * j + off + nt)),
            pl.BlockSpec((d, tn), lambda i, j: (0, 2 * j + off + 2 * nt)),
            pl.BlockSpec((CONV_WIDTH, tn), lambda i, j: (0, 2 * j + off)),
            pl.BlockSpec((tn, d), lambda i, j: (2 * j + off, 0)),
        ]

    return pl.pallas_call(
        kern,
        grid=(m // tm, nt // 2),
        in_specs=[pl.BlockSpec((tm, d), lambda i, j: (i, 0))] + tile_specs(0) + tile_specs(1) + [
            pl.BlockSpec((1, d), lambda i, j: (0, 0)),
            pl.BlockSpec((1, d), lambda i, j: (0, 0)),
        ],
        out_specs=pl.BlockSpec((tm, d), lambda i, j: (i, 0)),
        out_shape=jax.ShapeDtypeStruct((m, d), F32),
        scratch_shapes=[pltpu.VMEM((tm, d), BF16), pltpu.VMEM((tm, d), F32),
                        pltpu.VMEM((nt, 8, tn), F32)],
        compiler_params=_params(("arbitrary", "arbitrary")),
        name="conv_mixer_ln",
    )(x, w_in, w_in, w_in, conv_w, w_out, w_in, w_in, w_in, conv_w, w_out, g, b)


def _log_sigmoid(z):
    return jnp.minimum(z, 0.0) - jnp.log1p(jnp.exp(-jnp.abs(z)))


def _gla_kernel(q_ref, k_ref, v_ref, r_ref, a_ref, wa_ref, ba_ref, ng_ref, o_ref, st_ref, upd_ref):
    i = pl.program_id(2)
    c_sz = GLA_CHUNK
    dk, dv = GLA_HEAD_K, GLA_HEAD_V
    n_c = q_ref.shape[0] // c_sz
    heads = range(GLA_HEADS_PER_STEP)

    @pl.when(i == 0)
    def _():
        st_ref[...] = jnp.zeros_like(st_ref)

    rr = lax.broadcasted_iota(jnp.int32, (c_sz, c_sz), 0)
    cc = lax.broadcasted_iota(jnp.int32, (c_sz, c_sz), 1)
    causal = cc <= rr
    tril = jnp.where(causal, 1.0, 0.0).astype(BF16)
    chunks = [slice(c * c_sz, (c + 1) * c_sz) for c in range(n_c)]

    z = _dot(a_ref[...].astype(BF16), wa_ref[...]) + ba_ref[...]
    gk = _log_sigmoid(z) / GLA_GATE_NORM
    hi, mid, lo = _split3(gk)
    bcum = jnp.concatenate(
        [_dot(tril, hi[rs]) + _dot(tril, mid[rs]) + _dot(tril, lo[rs]) for rs in chunks], axis=0)
    b_last = [bcum[rs][c_sz - 1:c_sz, :] for rs in chunks]
    b_last_rows = jnp.concatenate([jnp.broadcast_to(bl, (c_sz, bl.shape[1])) for bl in b_last], axis=0)
    qd = (q_ref[...] * (dk ** -0.5) * jnp.exp(bcum)).astype(BF16)
    k = k_ref[...]
    kd = (k * jnp.exp(-bcum)).astype(BF16)
    kl = (k * jnp.exp(b_last_rows - bcum)).astype(BF16)
    v = v_ref[...].astype(BF16)

    o_intra = []
    for hh in heads:
        ks, vs = slice(hh * dk, (hh + 1) * dk), slice(hh * dv, (hh + 1) * dv)
        outs = []
        for c, rs in enumerate(chunks):
            att = jnp.where(causal, _dot_nt(qd[rs, ks], kd[rs, ks]), 0.0).astype(BF16)
            outs.append(_dot(att, v[rs, vs]))
            upd_ref[hh, c] = _dot_tn(v[rs, vs], kl[rs, ks])
        o_intra.append(outs)

    st = [st_ref[hh] for hh in heads]
    ng = ng_ref[...]
    for c, rs in enumerate(chunks):
        for hh in heads:
            ks, vs = slice(hh * dk, (hh + 1) * dk), slice(hh * dv, (hh + 1) * dv)
            o = o_intra[hh][c] + _dot_nt(qd[rs, ks], st[hh].astype(BF16))
            st[hh] = st[hh] * jnp.exp(b_last[c][:, ks]) + upd_ref[hh, c]
            o = o * lax.rsqrt(jnp.mean(o * o, axis=-1, keepdims=True) + LN_EPS) * ng
            o_ref[rs, vs] = (o * _silu(r_ref[rs, vs])).astype(BF16)
    for hh in heads:
        st_ref[hh] = st[hh]


def gla_scan(p, a, wa, ba, ng, *, bsz, seq, tt=512):
    m = p.shape[0]
    dk, dv = GLA_HEAD_K, GLA_HEAD_V
    hp = GLA_HEADS_PER_STEP
    nt = seq // tt
    c_k = GLA_KEY_DIM // (hp * dk)
    c_v = 2 * GLA_KEY_DIM // (hp * dv)
    c_r = (2 * GLA_KEY_DIM + GLA_VAL_DIM) // (hp * dv)
    row = lambda b, h, i: b * nt + i
    return pl.pallas_call(
        _gla_kernel,
        grid=(bsz, GLA_HEADS // hp, nt),
        in_specs=[
            pl.BlockSpec((tt, hp * dk), lambda b, h, i: (row(b, h, i), h)),
            pl.BlockSpec((tt, hp * dk), lambda b, h, i: (row(b, h, i), c_k + h)),
            pl.BlockSpec((tt, hp * dv), lambda b, h, i: (row(b, h, i), c_v + h)),
            pl.BlockSpec((tt, hp * dv), lambda b, h, i: (row(b, h, i), c_r + h)),
            pl.BlockSpec((tt, LANES), lambda b, h, i: (row(b, h, i), 0)),
            pl.BlockSpec((LANES, hp * dk), lambda b, h, i: (0, h)),
            pl.BlockSpec((1, hp * dk), lambda b, h, i: (0, h)),
            pl.BlockSpec((1, dv), lambda b, h, i: (0, 0)),
        ],
        out_specs=pl.BlockSpec((tt, hp * dv), lambda b, h, i: (row(b, h, i), h)),
        out_shape=jax.ShapeDtypeStruct((m, GLA_VAL_DIM), BF16),
        scratch_shapes=[pltpu.VMEM((hp, dv, dk), F32),
                        pltpu.VMEM((hp, tt // GLA_CHUNK, dv, dk), F32)],
        compiler_params=_params(("parallel", "parallel", "arbitrary")),
        name="gla_scan",
    )(p, p, p, p, a, wa, ba, ng)


def gla_mixer_ln(x, w_in, w_a2, b_a, norm_g, w_out, g, b, *, bsz, seq):
    n_main = 2 * GLA_KEY_DIM + 2 * GLA_VAL_DIM
    dummy = jnp.zeros((x.shape[0], LANES), F32)
    p = proj(x, w_in[:, :n_main].astype(BF16), dummy, dummy, tn=PROJ_WIDE)
    w_a = jnp.pad(w_in[:, n_main:], ((0, 0), (0, LANES - GLA_GATE_RANK))).astype(BF16)
    a = proj(x, w_a, dummy, dummy, tn=LANES)
    wa2 = jnp.pad(w_a2, ((0, LANES - GLA_GATE_RANK), (0, 0))).astype(BF16)
    o = gla_scan(p, a, wa2, b_a[None, :], norm_g[None, :], bsz=bsz, seq=seq)
    return out_ln(o, w_out.astype(BF16), x, g, b)


def kernel(x, positions, ln_g, ln_b, ffn_w_in, ffn_w_out, nsa_w_in, nsa_gate_b, nsa_cmp_pos,
           nsa_cmp_w1, nsa_cmp_w2, nsa_w_out, conv_w_in, conv_w, conv_w_out, gla_w_in, gla_w_a2,
           gla_b_a, gla_norm_g, gla_w_out):
    bsz, seq, d = x.shape
    m = bsz * seq
    h = x.reshape(m, d)
    cos, sin = rope_tables(positions.reshape(m, 1).astype(F32))
    ffn_in = ffn_w_in.astype(BF16)
    ffn_out = ffn_w_out.astype(BF16)
    for i in range(DEPTH):
        lg = lambda k: ln_g[i, k][None, :]
        lb = lambda k: ln_b[i, k][None, :]
        h = ffn_ln(h, ffn_in, ffn_out, lg(0), lb(0), i, 0)
        kind, j = i % N_MIXERS, i // N_MIXERS
        if kind == 0:
            h = nsa_mixer_ln(h, cos, sin, nsa_w_in[j], nsa_gate_b[j], nsa_cmp_pos[j], nsa_cmp_w1[j],
                             nsa_cmp_w2[j], nsa_w_out[j], lg(1), lb(1), bsz=bsz, seq=seq)
        elif kind == 1:
            h = conv_mixer_ln(h, conv_w_in[j].astype(BF16), conv_w[j], conv_w_out[j].astype(BF16),
                              lg(1), lb(1), seq=seq)
        else:
            h = gla_mixer_ln(h, gla_w_in[j], gla_w_a2[j], gla_b_a[j], gla_norm_g[j], gla_w_out[j],
                             lg(1), lb(1), bsz=bsz, seq=seq)
        h = ffn_ln(h, ffn_in, ffn_out, lg(2), lb(2), i, 1)
    return h.reshape(bsz, seq, d)
```
